```python
import jax
import jax.numpy as jnp
from jax import lax
import numpy as np

D_MODEL = 1024
BATCH = 4
SEQ = 4096
DEPTH = 4

CTX_LEN = 256
GRID_W = 64
N_MIXERS = 3
HEAD_DIM = 64
N_HEADS = D_MODEL // HEAD_DIM
N_KV_HEADS = max(1, N_HEADS // 4)
NA_WIN_H = 8
NA_WIN_W = 16
NA_QCOLS = 16
NA_KCOLS = NA_QCOLS + NA_WIN_W
SWA_RADIUS = 128
Q_BLOCK = 128
ROPE_THETA = 10000.0
D_FF = ((8 * D_MODEL // 3 + 127) // 128) * 128
CONV_W = 3
RMS_EPS = 1e-6
NEG_INF = -1e30
N_LAYERS_NA = (DEPTH + N_MIXERS - 1) // N_MIXERS
N_LAYERS_SWA = (DEPTH + N_MIXERS - 2) // N_MIXERS
N_LAYERS_GA = (DEPTH + N_MIXERS - 3) // N_MIXERS

kernel_name = 'hybrid_na_swa_axial_gqa_convffn_trunk'


def rmsnorm(x, g):
    x32 = x.astype(jnp.float32)
    y = x32 * lax.rsqrt(jnp.mean(x32 * x32, axis=-1, keepdims=True) + RMS_EPS)
    return (y * g.astype(jnp.float32)).astype(x.dtype)


def modulate(h, shift, scale):
    return h * (1 + scale) + shift


def axial_rope_tables(seq):
    t = jnp.arange(seq)
    row = (t // GRID_W).astype(jnp.float32)
    col = (t % GRID_W).astype(jnp.float32)
    n_axis = HEAD_DIM // 4
    inv = ROPE_THETA ** (-jnp.arange(n_axis, dtype=jnp.float32) / n_axis)
    ang = jnp.concatenate([row[:, None] * inv, col[:, None] * inv], axis=-1)
    return jnp.cos(ang), jnp.sin(ang)


def apply_rope(x, cos, sin):
    shp = (x.shape[1],) + (1,) * (x.ndim - 3) + (x.shape[-1] // 2,)
    cos = cos.reshape(shp).astype(x.dtype)
    sin = sin.reshape(shp).astype(x.dtype)
    x1, x2 = jnp.split(x, 2, axis=-1)
    return jnp.concatenate([x1 * cos - x2 * sin, x2 * cos + x1 * sin], axis=-1)


def attend(q, k, v, bias=None, mask=None, sink=None):
    s = jnp.einsum('...qngd,...knd->...ngqk', q, k).astype(jnp.float32) * (HEAD_DIM ** -0.5)
    if bias is not None:
        s = s + bias.astype(jnp.float32)
    if mask is not None:
        s = jnp.where(mask, s, NEG_INF)
    if sink is not None:
        sink_col = jnp.broadcast_to(sink.astype(jnp.float32)[:, :, None, None], s.shape[:-1] + (1,))
        s = jnp.concatenate([s, sink_col], axis=-1)
    p = jax.nn.softmax(s, axis=-1)
    if sink is not None:
        p = p[..., :-1]
    return jnp.einsum('...ngqk,...knd->...qngd', p.astype(v.dtype), v)


def attn_inputs(h, w_qkv, g_q, g_k, n_kv, with_q=True):
    bsz, t = h.shape[:2]
    kv_w = n_kv * HEAD_DIM
    if with_q:
        y = h @ w_qkv
        q = rmsnorm(y[..., :D_MODEL].reshape(bsz, t, n_kv, N_HEADS // n_kv, HEAD_DIM), g_q)
        kv = y[..., D_MODEL:]
    else:
        q = None
        kv = h @ w_qkv[:, D_MODEL:]
    k = rmsnorm(kv[..., :kv_w].reshape(bsz, t, n_kv, HEAD_DIM), g_k)
    v = kv[..., kv_w:].reshape(bsz, t, n_kv, HEAD_DIM)
    return q, k, v


def neighbourhood_attention(q, k, v, k_c, v_c, rpb):
    bsz, seq = q.shape[:2]
    n_ctx = k_c.shape[1]
    rows = seq // GRID_W
    kh = min(NA_WIN_H, rows)
    ncb = GRID_W // NA_QCOLS
    qg = q.reshape(bsz, rows, ncb, NA_QCOLS, N_HEADS, 1, HEAD_DIM)
    kg = k.reshape(bsz, rows, GRID_W, N_HEADS, HEAD_DIM)
    vg = v.reshape(bsz, rows, GRID_W, N_HEADS, HEAD_DIM)
    qcol = np.arange(GRID_W).reshape(ncb, NA_QCOLS)
    kcs = np.clip(np.arange(ncb) * NA_QCOLS - NA_WIN_W // 2, 0, GRID_W - NA_KCOLS)
    kcol = kcs[:, None] + np.arange(NA_KCOLS)
    cs = np.clip(qcol - NA_WIN_W // 2, 0, GRID_W - NA_WIN_W)
    col_ok = (kcol[:, None, :] >= cs[..., None]) & (kcol[:, None, :] < cs[..., None] + NA_WIN_W)
    dc_idx = np.clip(kcol[:, None, :] - qcol[..., None], -(NA_WIN_W - 1), NA_WIN_W - 1) + NA_WIN_W - 1
    lat_ok = np.broadcast_to(col_ok[:, :, None, :], (ncb, NA_QCOLS, kh, NA_KCOLS)).reshape(ncb, NA_QCOLS, kh * NA_KCOLS)
    mask = np.concatenate([lat_ok, np.ones((ncb, NA_QCOLS, n_ctx), bool)], axis=-1)[:, None, None]
    k_cb = jnp.broadcast_to(k_c[:, None], (bsz, ncb) + k_c.shape[1:])
    v_cb = jnp.broadcast_to(v_c[:, None], (bsz, ncb) + v_c.shape[1:])
    zero_ctx_bias = jnp.zeros((ncb, N_HEADS, NA_QCOLS, n_ctx), rpb.dtype)

    def row_step(r):
        rs = jnp.clip(r - kh // 2, 0, rows - kh)
        k_r = lax.dynamic_slice_in_dim(kg, rs, kh, axis=1)
        v_r = lax.dynamic_slice_in_dim(vg, rs, kh, axis=1)
        k_b = jnp.moveaxis(k_r[:, :, kcol], 1, 2).reshape(bsz, ncb, kh * NA_KCOLS, N_HEADS, HEAD_DIM)
        v_b = jnp.moveaxis(v_r[:, :, kcol], 1, 2).reshape(bsz, ncb, kh * NA_KCOLS, N_HEADS, HEAD_DIM)
        k_cat = jnp.concatenate([k_b, k_cb], axis=2)
        v_cat = jnp.concatenate([v_b, v_cb], axis=2)
        dr_idx = rs + jnp.arange(kh) - r + NA_WIN_H - 1
        b_lat = rpb[:, dr_idx[:, None, None, None], dc_idx[None]]
        b_lat = jnp.transpose(b_lat, (2, 0, 3, 1, 4)).reshape(ncb, N_HEADS, NA_QCOLS, kh * NA_KCOLS)
        bias = jnp.concatenate([b_lat, zero_ctx_bias], axis=-1)[:, :, None]
        q_r = lax.dynamic_index_in_dim(qg, r, axis=1, keepdims=False)
        return attend(q_r, k_cat, v_cat, bias=bias, mask=mask)

    out = lax.map(row_step, jnp.arange(rows))
    return jnp.moveaxis(out, 0, 1).reshape(bsz, seq, D_MODEL)


def sliding_window_attention(q, k, v, k_c, v_c, sink):
    bsz, seq = q.shape[:2]
    n_ctx = k_c.shape[1]
    nb = seq // Q_BLOCK
    span = Q_BLOCK + 2 * SWA_RADIUS
    pad = ((0, 0), (SWA_RADIUS, SWA_RADIUS), (0, 0), (0, 0))
    k_p = jnp.pad(k, pad)
    v_p = jnp.pad(v, pad)
    qoff = np.arange(Q_BLOCK)
    koff = np.arange(span) - SWA_RADIUS
    band = np.abs(qoff[:, None] - koff[None, :]) <= SWA_RADIUS
    ctx_ok = jnp.ones((Q_BLOCK, n_ctx), bool)

    def block_step(j):
        start = j * Q_BLOCK
        q_j = lax.dynamic_slice_in_dim(q, start, Q_BLOCK, axis=1)
        k_j = lax.dynamic_slice_in_dim(k_p, start, span, axis=1)
        v_j = lax.dynamic_slice_in_dim(v_p, start, span, axis=1)
        kpos = start - SWA_RADIUS + jnp.arange(span)
        valid = band & ((kpos >= 0) & (kpos < seq))[None, :]
        mask = jnp.concatenate([valid, ctx_ok], axis=-1)
        return attend(q_j, jnp.concatenate([k_j, k_c], axis=1), jnp.concatenate([v_j, v_c], axis=1),
                      mask=mask, sink=sink)

    out = lax.map(block_step, jnp.arange(nb))
    return jnp.moveaxis(out, 0, 1).reshape(bsz, seq, D_MODEL)


def blocked_global_attention(q, k, v, k_c, v_c):
    bsz, seq = q.shape[:2]
    nb = seq // Q_BLOCK
    k_all = jnp.concatenate([k, k_c], axis=1)
    v_all = jnp.concatenate([v, v_c], axis=1)

    def block_step(j):
        q_j = lax.dynamic_slice_in_dim(q, j * Q_BLOCK, Q_BLOCK, axis=1)
        return attend(q_j, k_all, v_all)

    out = lax.map(block_step, jnp.arange(nb))
    return jnp.moveaxis(out, 0, 1).reshape(bsz, seq, D_MODEL)


def conv_ffn(h, w_up, conv_w, conv_b, w_down):
    t = h.shape[1]
    u = h @ w_up
    u_p = jnp.pad(u, ((0, 0), (CONV_W // 2, CONV_W // 2), (0, 0)))
    u = sum(u_p[:, tap:tap + t] * conv_w[tap] for tap in range(CONV_W)) + conv_b
    a, g = jnp.split(u, 2, axis=-1)
    return (a * jax.nn.silu(g)) @ w_down


def setup_inputs(seed: int = 0) -> dict:
    key = jax.random.key(seed)
    ks = iter(jax.random.split(key, 32))
    d = D_MODEL
    qkv_na = d + 2 * N_HEADS * HEAD_DIM
    qkv_gqa = d + 2 * N_KV_HEADS * HEAD_DIM

    def nrm(shape, scale):
        return scale * jax.random.normal(next(ks), shape, jnp.float32)

    return {
        'x': nrm((BATCH, SEQ, d), 1.0),
        'c': nrm((BATCH, d), 1.0),
        'ctx': nrm((BATCH, CTX_LEN, d), 1.0),
        'c_ctx': nrm((d,), 1.0),
        'w_mod': nrm((DEPTH, d, 6 * d), 0.5 * d ** -0.5),
        'b_mod': nrm((DEPTH, 6 * d), 0.02),
        'g_attn': 1.0 + nrm((DEPTH, d), 0.02),
        'g_ffn': 1.0 + nrm((DEPTH, d), 0.02),
        'na_w_qkv': nrm((N_LAYERS_NA, d, qkv_na), d ** -0.5),
        'na_g_q': 1.0 + nrm((N_LAYERS_NA, HEAD_DIM), 0.02),
        'na_g_k': 1.0 + nrm((N_LAYERS_NA, HEAD_DIM), 0.02),
        'na_rpb': nrm((N_LAYERS_NA, N_HEADS, 2 * NA_WIN_H - 1, 2 * NA_WIN_W - 1), 0.5),
        'na_w_o': nrm((N_LAYERS_NA, d, d), d ** -0.5),
        'swa_w_qkv': nrm((N_LAYERS_SWA, d, qkv_gqa), d ** -0.5),
        'swa_g_q': 1.0 + nrm((N_LAYERS_SWA, HEAD_DIM), 0.02),
        'swa_g_k': 1.0 + nrm((N_LAYERS_SWA, HEAD_DIM), 0.02),
        'swa_sink': nrm((N_LAYERS_SWA, N_HEADS), 0.5),
        'swa_w_o': nrm((N_LAYERS_SWA, d, d), d ** -0.5),
        'ga_w_qkv': nrm((N_LAYERS_GA, d, qkv_gqa), d ** -0.5),
        'ga_g_q': 1.0 + nrm((N_LAYERS_GA, HEAD_DIM), 0.02),
        'ga_g_k': 1.0 + nrm((N_LAYERS_GA, HEAD_DIM), 0.02),
        'ga_w_o': nrm((N_LAYERS_GA, d, d), d ** -0.5),
        'ffn_w_up': nrm((DEPTH, d, 2 * D_FF), d ** -0.5),
        'ffn_conv_w': nrm((DEPTH, CONV_W, 2 * D_FF), CONV_W ** -0.5),
        'ffn_conv_b': nrm((DEPTH, 2 * D_FF), 0.02),
        'ffn_w_down': nrm((DEPTH, D_FF, d), D_FF ** -0.5),
    }


def reference(x, c, ctx, c_ctx, w_mod, b_mod, g_attn, g_ffn,
              na_w_qkv, na_g_q, na_g_k, na_rpb, na_w_o,
              swa_w_qkv, swa_g_q, swa_g_k, swa_sink, swa_w_o,
              ga_w_qkv, ga_g_q, ga_g_k, ga_w_o,
              ffn_w_up, ffn_conv_w, ffn_conv_b, ffn_w_down):
    bsz, seq = x.shape[:2]
    n_ctx = ctx.shape[1]
    cos, sin = axial_rope_tables(seq)
    silu_c = jax.nn.silu(c)
    silu_cc = jax.nn.silu(c_ctx)
    for i in range(DEPTH):
        last = i == DEPTH - 1
        kind, j = i % N_MIXERS, i // N_MIXERS
        mod = silu_c @ w_mod[i] + b_mod[i]
        mod_c = silu_cc @ w_mod[i] + b_mod[i]
        sh1, sc1, gt1, sh2, sc2, gt2 = jnp.split(mod[:, None, :], 6, axis=-1)
        sh1c, sc1c, gt1c, sh2c, sc2c, gt2c = jnp.split(mod_c, 6, axis=-1)

        h = modulate(rmsnorm(x, g_attn[i]), sh1, sc1)
        hc = modulate(rmsnorm(ctx, g_attn[i]), sh1c, sc1c)
        if kind == 0:
            w_in, w_out, gq, gk, n_kv, sink = na_w_qkv[j], na_w_o[j], na_g_q[j], na_g_k[j], N_HEADS, None
        elif kind == 1:
            w_in, w_out, gq, gk, n_kv = swa_w_qkv[j], swa_w_o[j], swa_g_q[j], swa_g_k[j], N_KV_HEADS
            sink = swa_sink[j].reshape(n_kv, N_HEADS // n_kv)
        else:
            w_in, w_out, gq, gk, n_kv, sink = ga_w_qkv[j], ga_w_o[j], ga_g_q[j], ga_g_k[j], N_KV_HEADS, None
        q, k, v = attn_inputs(h, w_in, gq, gk, n_kv)
        q_c, k_c, v_c = attn_inputs(hc, w_in, gq, gk, n_kv, with_q=not last)
        if kind == 0:
            o = neighbourhood_attention(q, k, v, k_c, v_c, na_rpb[j])
        else:
            q = apply_rope(q, cos, sin)
            k = apply_rope(k, cos, sin)
            if kind == 1:
                o = sliding_window_attention(q, k, v, k_c, v_c, sink)
            else:
                o = blocked_global_attention(q, k, v, k_c, v_c)
        x = x + gt1 * (o @ w_out)
        if not last:
            o_c = attend(q_c, k_c, v_c, sink=sink).reshape(bsz, n_ctx, D_MODEL)
            ctx = ctx + gt1c * (o_c @ w_out)

        h = modulate(rmsnorm(x, g_ffn[i]), sh2, sc2)
        x = x + gt2 * conv_ffn(h, ffn_w_up[i], ffn_conv_w[i], ffn_conv_b[i], ffn_w_down[i])
        if not last:
            hc = modulate(rmsnorm(ctx, g_ffn[i]), sh2c, sc2c)
            ctx = ctx + gt2c * conv_ffn(hc, ffn_w_up[i], ffn_conv_w[i], ffn_conv_b[i], ffn_w_down[i])
    return x
```

```python
import functools

import numpy as np
import jax
import jax.numpy as jnp
from jax import lax
from jax.experimental import pallas as pl
from jax.experimental.pallas import tpu as pltpu

D_MODEL = 1024
GRID_W = 64
HEAD_DIM = 64
N_HEADS = D_MODEL // HEAD_DIM
N_KV_HEADS = 4
N_MIXERS = 3
NA_WIN_H = 8
NA_WIN_W = 16
SWA_RADIUS = 128
ROPE_THETA = 10000.0
CONV_W = 3
RMS_EPS = 1e-6
NEG_INF = -1e30

TM = 256
HALO = 8
FF_CHUNK = 256
HEAD_PAIR = 2 * HEAD_DIM
HEADS_PER_STEP = 4
VMEM_LIMIT = 56 * 1024 * 1024

BF16 = jnp.bfloat16
F32 = jnp.float32


def _cparams(n_axes):
    return pltpu.CompilerParams(dimension_semantics=("arbitrary",) * n_axes,
                                vmem_limit_bytes=VMEM_LIMIT)


def _resident(block_shape, index_map):
    return pl.BlockSpec(block_shape, index_map, pipeline_mode=pl.Buffered(1))


def _sigmoid(z):
    return 1.0 / (1.0 + jnp.exp(-z))


def _norm_mod(x, g, shift, scale):
    var = jnp.mean(x * x, axis=-1, keepdims=True)
    return (x * lax.rsqrt(var + RMS_EPS) * g) * (1.0 + scale) + shift


def _mod_kernel(c_ref, w_ref, b_ref, o_ref):
    c = c_ref[...]
    sc = (c * _sigmoid(c)).astype(BF16)
    o_ref[0] = jnp.dot(sc, w_ref[0].astype(BF16), preferred_element_type=F32) + b_ref[0]


def _modulation(cvec, w_mod, b_mod):
    depth, d, n = w_mod.shape
    bn = 1536
    return pl.pallas_call(
        _mod_kernel,
        grid=(depth, n // bn),
        in_specs=[pl.BlockSpec((8, d), lambda i, j: (0, 0)),
                  pl.BlockSpec((1, d, bn), lambda i, j: (i, 0, j)),
                  pl.BlockSpec((1, 1, bn), lambda i, j: (i, 0, j))],
        out_specs=pl.BlockSpec((1, 8, bn), lambda i, j: (i, 0, j)),
        out_shape=jax.ShapeDtypeStruct((depth, 8, n), F32),
        compiler_params=_cparams(2),
        name="modulation",
    )(cvec, w_mod, b_mod.reshape(depth, 1, n))


def _qkv_kernel(x_ref, mod_ref, g_ref, wt_ref, gq_ref, gk_ref, cos_ref, sin_ref,
                q_ref, k_ref, v_ref, *, kvw, rope):
    m = mod_ref[0, 0]
    h = _norm_mod(x_ref[0], g_ref[...], m[0:1], m[1:2]).astype(BF16)
    yt = lax.dot_general(wt_ref[...], h, (((1,), (1,)), ((), ())), preferred_element_type=F32)

    def head_norm(y, gain, n_heads):
        y3 = y.reshape(n_heads, HEAD_DIM, TM)
        ms = jnp.mean(y3 * y3, axis=1, keepdims=True)
        y3 = y3 * lax.rsqrt(ms + RMS_EPS) * gain
        if rope:
            half = HEAD_DIM // 2
            x1, x2 = y3[:, :half], y3[:, half:]
            c, s = cos_ref[0], sin_ref[0]
            y3 = jnp.concatenate([x1 * c - x2 * s, x2 * c + x1 * s], axis=1)
        return y3.reshape(n_heads * HEAD_DIM, TM)

    q = head_norm(yt[:D_MODEL], gq_ref[...], N_HEADS) * (HEAD_DIM ** -0.5)
    q_ref[0, 0] = q.astype(BF16)
    k = head_norm(yt[D_MODEL:D_MODEL + kvw], gk_ref[...], kvw // HEAD_DIM)
    k_ref[0, 0] = k.T.astype(BF16)
    v_ref[0, 0] = yt[D_MODEL + kvw:].astype(BF16)


def _qkv(x, mod, g, w_qkv, g_q, g_k, cos_t, sin_t, *, rope):
    bsz, t_all, d = x.shape
    nt = t_all // TM
    n = w_qkv.shape[1]
    kvw = (n - d) // 2
    wt = w_qkv.T.astype(BF16)
    gq = jnp.broadcast_to(g_q[:, None], (HEAD_DIM, TM))
    gk = jnp.broadcast_to(g_k[:, None], (HEAD_DIM, TM))
    half = HEAD_DIM // 2
    return pl.pallas_call(
        functools.partial(_qkv_kernel, kvw=kvw, rope=rope),
        grid=(bsz, nt),
        in_specs=[pl.BlockSpec((1, TM, d), lambda b, t: (b, t, 0)),
                  pl.BlockSpec((1, 1, 6, d), lambda b, t: (b, t // (nt - 1), 0, 0)),
                  pl.BlockSpec((1, d), lambda b, t: (0, 0)),
                  _resident((n, d), lambda b, t: (0, 0)),
                  pl.BlockSpec((HEAD_DIM, TM), lambda b, t: (0, 0)),
                  pl.BlockSpec((HEAD_DIM, TM), lambda b, t: (0, 0)),
                  pl.BlockSpec((1, half, TM), lambda b, t: (t, 0, 0)),
                  pl.BlockSpec((1, half, TM), lambda b, t: (t, 0, 0))],
        out_specs=[pl.BlockSpec((1, 1, d, TM), lambda b, t: (b, t, 0, 0)),
                   pl.BlockSpec((1, 1, TM, kvw), lambda b, t: (b, t, 0, 0)),
                   pl.BlockSpec((1, 1, kvw, TM), lambda b, t: (b, t, 0, 0))],
        out_shape=[jax.ShapeDtypeStruct((bsz, nt, d, TM), BF16),
                   jax.ShapeDtypeStruct((bsz, nt, TM, kvw), BF16),
                   jax.ShapeDtypeStruct((bsz, nt, kvw, TM), BF16)],
        compiler_params=_cparams(2),
        name="qkv",
    )(x, mod, g.reshape(1, d), wt, gq, gk, cos_t, sin_t)


def _padded_queries(q_ref, first_head, n_q_heads, kv_head):
    qs = [q_ref[0, 0, (first_head + j) * HEAD_DIM:(first_head + j + 1) * HEAD_DIM, :]
          for j in range(n_q_heads)]
    qcat = qs[0] if n_q_heads == 1 else jnp.concatenate(qs, axis=1)
    zeros = jnp.zeros_like(qcat)
    return jnp.concatenate([qcat, zeros] if kv_head % 2 == 0 else [zeros, qcat], axis=0)


def _store_heads(o_ref, o, first_head, n_q_heads):
    for j in range(n_q_heads):
        o_ref[0, 0, (first_head + j) * HEAD_DIM:(first_head + j + 1) * HEAD_DIM, :] = (
            o[:, j * TM:(j + 1) * TM].astype(o_ref.dtype))


def _local_attn_kernel(*refs, kind, group, n_lat):
    q_ref = refs[0]
    k_refs = refs[1:5]
    v_refs = refs[5:9]
    extra_ref, o_ref = refs[9], refs[10]
    t = pl.program_id(2 if kind == 0 else 1)
    is_ctx = t >= n_lat
    valid = [jnp.logical_not(is_ctx),
             jnp.logical_and(t >= 1, jnp.logical_not(is_ctx)),
             t <= n_lat - 2,
             None]
    pen = [None if v is None else jnp.where(v, 0.0, NEG_INF).astype(F32) for v in valid]

    if kind == 1:
        key_i = lax.broadcasted_iota(jnp.int32, (TM, TM), 0)
        qry_j = lax.broadcasted_iota(jnp.int32, (TM, TM), 1)
        band = [jnp.where(jnp.abs(key_i + off * TM - qry_j) <= SWA_RADIUS, 0.0, NEG_INF).astype(F32)
                for off in (0, -1, 1)]
    else:
        tab = [jnp.where(t == 0, 0, jnp.where(t >= n_lat - 1, 6, 3)),
               jnp.where(t >= n_lat - 1, 5, 2),
               jnp.where(t == 0, 1, 4)]

    n_kv = HEADS_PER_STEP if kind == 0 else N_KV_HEADS
    for n in range(n_kv):
        pair = n // 2
        qp = _padded_queries(q_ref, n * group, group, n)
        scores = []
        for c in range(4):
            kblk = k_refs[c][0, 0, :, pair * HEAD_PAIR:(pair + 1) * HEAD_PAIR]
            s = jnp.dot(kblk, qp, preferred_element_type=F32)
            if c < 3:
                if kind == 1:
                    add = band[c] + pen[c]
                    s = s + (add if group == 1 else jnp.concatenate([add] * group, axis=1))
                else:
                    s = s + (extra_ref[n, tab[c]] + pen[c])
            scores.append(s)
        m = functools.reduce(jnp.maximum, [jnp.max(s, axis=0, keepdims=True) for s in scores])
        if kind == 1:
            sink = extra_ref[n]
            m = jnp.maximum(m, sink)
        ps = [jnp.exp(s - m) for s in scores]
        l = functools.reduce(jnp.add, [jnp.sum(p, axis=0, keepdims=True) for p in ps])
        if kind == 1:
            l = l + jnp.exp(sink - m)
        pv = functools.reduce(jnp.add, [
            jnp.dot(v_refs[c][0, 0, n * HEAD_DIM:(n + 1) * HEAD_DIM, :], ps[c].astype(BF16),
                    preferred_element_type=F32) for c in range(4)])
        _store_heads(o_ref, pv / l, n * group, group)


def _local_attention(q_t, k_tok, v_t, extra, *, kind, nt_out):
    bsz, nt, d, _ = q_t.shape
    n_lat = nt - 1
    kvw = k_tok.shape[-1]
    ctx_idx = nt - 1
    kern = functools.partial(_local_attn_kernel, kind=kind,
                             group=1 if kind == 0 else N_HEADS // N_KV_HEADS, n_lat=n_lat)
    if kind == 0:
        hs = HEADS_PER_STEP
        grid = (N_HEADS // hs, bsz, nt_out)
        blk = hs * HEAD_DIM
        q_spec = pl.BlockSpec((1, 1, blk, TM), lambda g, b, t: (b, t, g, 0))
        chunk_maps = [lambda g, b, t: (b, t, 0, g),
                      lambda g, b, t: (b, jnp.maximum(t - 1, 0), 0, g),
                      lambda g, b, t: (b, jnp.minimum(t + 1, n_lat - 1), 0, g),
                      lambda g, b, t: (b, ctx_idx, 0, g)]
        k_specs = [pl.BlockSpec((1, 1, TM, blk), f) for f in chunk_maps]
        vmaps = [lambda g, b, t: (b, t, g, 0),
                 lambda g, b, t: (b, jnp.maximum(t - 1, 0), g, 0),
                 lambda g, b, t: (b, jnp.minimum(t + 1, n_lat - 1), g, 0),
                 lambda g, b, t: (b, ctx_idx, g, 0)]
        v_specs = [pl.BlockSpec((1, 1, blk, TM), f) for f in vmaps]
        extra_spec = pl.BlockSpec((hs, 7, TM, TM), lambda g, b, t: (g, 0, 0, 0))
        o_spec = pl.BlockSpec((1, 1, blk, TM), lambda g, b, t: (b, t, g, 0))
    else:
        grid = (bsz, nt_out)
        q_spec = pl.BlockSpec((1, 1, d, TM), lambda b, t: (b, t, 0, 0))
        chunk_maps = [lambda b, t: (b, t, 0, 0),
                      lambda b, t: (b, jnp.maximum(t - 1, 0), 0, 0),
                      lambda b, t: (b, jnp.minimum(t + 1, n_lat - 1), 0, 0),
                      lambda b, t: (b, ctx_idx, 0, 0)]
        k_specs = [pl.BlockSpec((1, 1, TM, kvw), f) for f in chunk_maps]
        v_specs = [pl.BlockSpec((1, 1, kvw, TM), f) for f in chunk_maps]
        extra_spec = pl.BlockSpec(extra.shape, lambda b, t: (0, 0, 0))
        o_spec = pl.BlockSpec((1, 1, d, TM), lambda b, t: (b, t, 0, 0))
    return pl.pallas_call(
        kern,
        grid=grid,
        in_specs=[q_spec] + k_specs + v_specs + [extra_spec],
        out_specs=o_spec,
        out_shape=jax.ShapeDtypeStruct((bsz, nt_out, d, TM), BF16),
        compiler_params=_cparams(len(grid)),
        name="na_attention" if kind == 0 else "swa_attention",
    )(q_t, k_tok, k_tok, k_tok, k_tok, v_t, v_t, v_t, v_t, extra)


def _na_bias_tables(rpb, n_lat):
    rows_per_tile = TM // GRID_W
    rows = n_lat * rows_per_tile
    combos = [(0, 0), (0, 1), (1, -1), (1, 0), (1, 1), (n_lat - 1, -1), (n_lat - 1, 0)]
    kr, kc = np.divmod(np.arange(TM), GRID_W)
    qr, qc = np.divmod(np.arange(TM), GRID_W)
    dr_idx, dc_idx, ok = [], [], []
    for tile, off in combos:
        q_row = tile * rows_per_tile + qr[None, :]
        k_row = (tile + off) * rows_per_tile + kr[:, None]
        rs = np.clip(q_row - NA_WIN_H // 2, 0, rows - NA_WIN_H)
        cs = np.clip(qc[None, :] - NA_WIN_W // 2, 0, GRID_W - NA_WIN_W)
        row_ok = (k_row >= rs) & (k_row < rs + NA_WIN_H)
        col_ok = (kc[:, None] >= cs) & (kc[:, None] < cs + NA_WIN_W)
        ok.append(row_ok & col_ok)
        dr_idx.append(np.clip(k_row - q_row + NA_WIN_H - 1, 0, 2 * NA_WIN_H - 2))
        dc_idx.append(np.clip(kc[:, None] - qc[None, :], -(NA_WIN_W - 1), NA_WIN_W - 1) + NA_WIN_W - 1)
    dr_idx, dc_idx, ok = np.stack(dr_idx), np.stack(dc_idx), np.stack(ok)
    return jnp.where(ok[None], rpb[:, dr_idx, dc_idx], NEG_INF).astype(F32)


def _global_attn_kernel(q_ref, k_ref, v_ref, o_ref, *, n_lat):
    t = pl.program_id(1)
    nt = k_ref.shape[1]
    lo = jnp.where(t >= n_lat, n_lat, 0)
    group = N_HEADS // N_KV_HEADS
    width = group * TM
    for n in range(N_KV_HEADS):
        pair = n // 2
        qp = _padded_queries(q_ref, n * group, group, n)

        def step(c, carry, pair=pair, qp=qp, n=n):
            m, l, acc = carry
            kblk = k_ref[0, c, :, pair * HEAD_PAIR:(pair + 1) * HEAD_PAIR]
            s = jnp.dot(kblk, qp, preferred_element_type=F32)
            m_new = jnp.maximum(m, jnp.max(s, axis=0, keepdims=True))
            alpha = jnp.exp(m - m_new)
            p = jnp.exp(s - m_new)
            l = alpha * l + jnp.sum(p, axis=0, keepdims=True)
            pv = jnp.dot(v_ref[0, c, n * HEAD_DIM:(n + 1) * HEAD_DIM, :], p.astype(BF16),
                         preferred_element_type=F32)
            return m_new, l, alpha * acc + pv

        init = (jnp.full((1, width), NEG_INF, F32), jnp.zeros((1, width), F32),
                jnp.zeros((HEAD_DIM, width), F32))
        m, l, acc = lax.fori_loop(lo, nt, step, init)
        _store_heads(o_ref, acc / l, n * group, group)


def _global_attention(q_t, k_tok, v_t, *, nt_out):
    bsz, nt, d, _ = q_t.shape
    kvw = k_tok.shape[-1]
    return pl.pallas_call(
        functools.partial(_global_attn_kernel, n_lat=nt - 1),
        grid=(bsz, nt_out),
        in_specs=[pl.BlockSpec((1, 1, d, TM), lambda b, t: (b, t, 0, 0)),
                  pl.BlockSpec((1, nt, TM, kvw), lambda b, t: (b, 0, 0, 0)),
                  pl.BlockSpec((1, nt, kvw, TM), lambda b, t: (b, 0, 0, 0))],
        out_specs=pl.BlockSpec((1, 1, d, TM), lambda b, t: (b, t, 0, 0)),
        out_shape=jax.ShapeDtypeStruct((bsz, nt_out, d, TM), BF16),
        compiler_params=_cparams(2),
        name="global_attention",
    )(q_t, k_tok, v_t)


def _out_proj_kernel(o_ref, x_ref, mod_ref, w_ref, y_ref):
    out_t = jnp.dot(w_ref[...], o_ref[0, 0], preferred_element_type=F32)
    y_ref[0] = x_ref[0] + mod_ref[0, 0][2:3] * out_t.T


def _out_proj(o_t, x, mod, w_o):
    bsz, nt_out, d, _ = o_t.shape
    nt = x.shape[1] // TM
    return pl.pallas_call(
        _out_proj_kernel,
        grid=(bsz, nt_out),
        in_specs=[pl.BlockSpec((1, 1, d, TM), lambda b, t: (b, t, 0, 0)),
                  pl.BlockSpec((1, TM, d), lambda b, t: (b, t, 0)),
                  pl.BlockSpec((1, 1, 6, d), lambda b, t: (b, t // (nt - 1), 0, 0)),
                  _resident((d, d), lambda b, t: (0, 0))],
        out_specs=pl.BlockSpec((1, TM, d), lambda b, t: (b, t, 0)),
        out_shape=jax.ShapeDtypeStruct((bsz, nt_out * TM, d), F32),
        compiler_params=_cparams(2),
        name="out_proj",
    )(o_t, x, mod, w_o.T.astype(BF16))


def _ffn_kernel(x_ref, xp_ref, xn_ref, mod_ref, g_ref, wup_ref, cw_ref, cb_ref, wdn_ref,
                y_ref, h_scr, a_scr, *, n_lat):
    t = pl.program_id(1)
    m = mod_ref[0, 0]
    g = g_ref[...]
    x = x_ref[0]
    h_scr[0:TM] = _norm_mod(x, g, m[3:4], m[4:5]).astype(BF16)
    halo = jnp.concatenate([xp_ref[0], xn_ref[0]], axis=0)
    h_scr[TM:TM + 2 * HALO] = _norm_mod(halo, g, m[3:4], m[4:5]).astype(BF16)
    is_ctx = t >= n_lat
    prev_ok = jnp.logical_and(t >= 1, jnp.logical_not(is_ctx))
    next_ok = t <= n_lat - 2
    row = lax.broadcasted_iota(jnp.int32, (TM, 2 * FF_CHUNK), 0)
    h = h_scr[...]
    d_ff = wdn_ref.shape[0]
    for c in range(d_ff // FF_CHUNK):
        cols = slice(c * 2 * FF_CHUNK, (c + 1) * 2 * FF_CHUNK)
        u = jnp.dot(h, wup_ref[:, cols], preferred_element_type=F32)
        um = u[0:TM]
        u_before = jnp.where(prev_ok, u[TM + HALO - 1:TM + HALO], 0.0)
        u_after = jnp.where(next_ok, u[TM + HALO:TM + HALO + 1], 0.0)
        up = jnp.where(row == 0, u_before, pltpu.roll(um, 1, 0))
        dn = jnp.where(row == TM - 1, u_after, pltpu.roll(um, TM - 1, 0))
        cw = cw_ref[:, cols]
        conv = up * cw[0:1] + um * cw[1:2] + dn * cw[2:3] + cb_ref[:, cols]
        a, gate = conv[:, :FF_CHUNK], conv[:, FF_CHUNK:]
        a_scr[:, c * FF_CHUNK:(c + 1) * FF_CHUNK] = (a * (gate * _sigmoid(gate))).astype(BF16)
    down = jnp.dot(a_scr[...], wdn_ref[...], preferred_element_type=F32)
    y_ref[0] = x + m[5:6] * down


def _chunk_interleave(w, d_ff):
    lead = w.shape[:-1]
    w = w.reshape(lead + (2, d_ff // FF_CHUNK, FF_CHUNK))
    return jnp.swapaxes(w, -3, -2).reshape(lead + (2 * d_ff,))


def _conv_ffn(x, mod, g, w_up, conv_w, conv_b, w_down, *, n_lat, nt_out):
    bsz, t_all, d = x.shape
    d_ff = w_down.shape[0]
    blocks_per_tile = TM // HALO
    last_halo = t_all // HALO - 1
    return pl.pallas_call(
        functools.partial(_ffn_kernel, n_lat=n_lat),
        grid=(bsz, nt_out),
        in_specs=[pl.BlockSpec((1, TM, d), lambda b, t: (b, t, 0)),
                  pl.BlockSpec((1, HALO, d), lambda b, t: (b, jnp.maximum(t * blocks_per_tile - 1, 0), 0)),
                  pl.BlockSpec((1, HALO, d),
                               lambda b, t: (b, jnp.minimum((t + 1) * blocks_per_tile, last_halo), 0)),
                  pl.BlockSpec((1, 1, 6, d), lambda b, t: (b, t // n_lat, 0, 0)),
                  pl.BlockSpec((1, d), lambda b, t: (0, 0)),
                  _resident((d, 2 * d_ff), lambda b, t: (0, 0)),
                  pl.BlockSpec((CONV_W, 2 * d_ff), lambda b, t: (0, 0)),
                  pl.BlockSpec((1, 2 * d_ff), lambda b, t: (0, 0)),
                  _resident((d_ff, d), lambda b, t: (0, 0))],
        out_specs=pl.BlockSpec((1, TM, d), lambda b, t: (b, t, 0)),
        out_shape=jax.ShapeDtypeStruct((bsz, nt_out * TM, d), F32),
        scratch_shapes=[pltpu.VMEM((TM + 2 * HALO, d), BF16), pltpu.VMEM((TM, d_ff), BF16)],
        compiler_params=_cparams(2),
        name="conv_ffn",
    )(x, x, x, mod, g.reshape(1, d),
      _chunk_interleave(w_up, d_ff).astype(BF16), _chunk_interleave(conv_w, d_ff),
      _chunk_interleave(conv_b, d_ff).reshape(1, 2 * d_ff), w_down.astype(BF16))


def _rope_tables(seq, n_ctx):
    t = jnp.arange(seq)
    row = (t // GRID_W).astype(F32)
    col = (t % GRID_W).astype(F32)
    n_axis = HEAD_DIM // 4
    inv = ROPE_THETA ** (-jnp.arange(n_axis, dtype=F32) / n_axis)
    ang = jnp.concatenate([row[:, None] * inv, col[:, None] * inv], axis=-1)
    cos = jnp.concatenate([jnp.cos(ang), jnp.ones((n_ctx, HEAD_DIM // 2), F32)], axis=0)
    sin = jnp.concatenate([jnp.sin(ang), jnp.zeros((n_ctx, HEAD_DIM // 2), F32)], axis=0)
    nt = (seq + n_ctx) // TM
    to_tiles = lambda a: a.reshape(nt, TM, HEAD_DIM // 2).transpose(0, 2, 1)
    return to_tiles(cos), to_tiles(sin)


def kernel(x, c, ctx, c_ctx, w_mod, b_mod, g_attn, g_ffn, na_w_qkv, na_g_q, na_g_k, na_rpb, na_w_o, swa_w_qkv, swa_g_q, swa_g_k, swa_sink, swa_w_o, ga_w_qkv, ga_g_q, ga_g_k, ga_w_o, ffn_w_up, ffn_conv_w, ffn_conv_b, ffn_w_down):
    bsz, seq, d = x.shape
    n_ctx = ctx.shape[1]
    depth = w_mod.shape[0]
    assert d == D_MODEL and seq % TM == 0 and n_ctx == TM and seq == GRID_W * GRID_W
    n_lat = seq // TM
    nt = n_lat + 1
    group = N_HEADS // N_KV_HEADS

    cvec = jnp.zeros((8, d), F32).at[:bsz].set(c).at[bsz].set(c_ctx)
    mods = _modulation(cvec, w_mod, b_mod)
    mod_lat = mods[:, :bsz].reshape(depth, bsz, 1, 6, d)
    mod_ctx = jnp.broadcast_to(mods[:, bsz].reshape(depth, 1, 1, 6, d), (depth, bsz, 1, 6, d))
    mod_all = jnp.concatenate([mod_lat, mod_ctx], axis=2)

    cos_t, sin_t = _rope_tables(seq, n_ctx)
    stream = jnp.concatenate([x, ctx], axis=1)

    for i in range(depth):
        last = i == depth - 1
        kind, j = i % N_MIXERS, i // N_MIXERS
        nt_out = n_lat if last else nt
        mod = mod_all[i]
        if kind == 0:
            w_in, w_out, gq, gk = na_w_qkv[j], na_w_o[j], na_g_q[j], na_g_k[j]
        elif kind == 1:
            w_in, w_out, gq, gk = swa_w_qkv[j], swa_w_o[j], swa_g_q[j], swa_g_k[j]
        else:
            w_in, w_out, gq, gk = ga_w_qkv[j], ga_w_o[j], ga_g_q[j], ga_g_k[j]
        q_t, k_tok, v_t = _qkv(stream, mod, g_attn[i], w_in, gq, gk, cos_t, sin_t, rope=kind != 0)
        if kind == 0:
            o_t = _local_attention(q_t, k_tok, v_t, _na_bias_tables(na_rpb[j], n_lat),
                                   kind=0, nt_out=nt_out)
        elif kind == 1:
            sink = jnp.broadcast_to(swa_sink[j].reshape(N_KV_HEADS, 1, group, 1),
                                    (N_KV_HEADS, 1, group, TM)).reshape(N_KV_HEADS, 1, group * TM)
            o_t = _local_attention(q_t, k_tok, v_t, sink, kind=1, nt_out=nt_out)
        else:
            o_t = _global_attention(q_t, k_tok, v_t, nt_out=nt_out)
        stream = _out_proj(o_t, stream, mod, w_out)
        stream = _conv_ffn(stream, mod, g_ffn[i], ffn_w_up[i], ffn_conv_w[i], ffn_conv_b[i],
                           ffn_w_down[i], n_lat=n_lat, nt_out=nt_out)
    return stream
```

```python
import functools

import numpy as np
import jax
import jax.numpy as jnp
from jax import lax
from jax.experimental import pallas as pl
from jax.experimental.pallas import tpu as pltpu

D_MODEL = 1024
GRID_W = 64
HEAD_DIM = 64
N_HEADS = D_MODEL // HEAD_DIM
N_KV_HEADS = 4
N_MIXERS = 3
NA_WIN_H = 8
NA_WIN_W = 16
SWA_RADIUS = 128
ROPE_THETA = 10000.0
CONV_W = 3
RMS_EPS = 1e-6
NEG_INF = -1e30

TM = 256
HALO = 8
FF_CHUNK = 256
HEAD_PAIR = 2 * HEAD_DIM
HEADS_PER_STEP = 4
ONES_ROWS = 16
LOG2E = 1.4426950408889634
VMEM_LIMIT = 56 * 1024 * 1024

BF16 = jnp.bfloat16
F32 = jnp.float32


def _cparams(n_axes):
    return pltpu.CompilerParams(dimension_semantics=("arbitrary",) * n_axes,
                                vmem_limit_bytes=VMEM_LIMIT)


def _resident(block_shape, index_map):
    return pl.BlockSpec(block_shape, index_map, pipeline_mode=pl.Buffered(1))


def _sigmoid(z):
    return 1.0 / (1.0 + jnp.exp(-z))


def _norm_mod(x, g, shift, scale):
    var = jnp.mean(x * x, axis=-1, keepdims=True)
    return (x * lax.rsqrt(var + RMS_EPS) * g) * (1.0 + scale) + shift


def _mod_kernel(c_ref, w_ref, b_ref, o_ref):
    c = c_ref[...]
    sc = (c * _sigmoid(c)).astype(BF16)
    o_ref[0] = jnp.dot(sc, w_ref[0].astype(BF16), preferred_element_type=F32) + b_ref[0]


def _modulation(cvec, w_mod, b_mod):
    depth, d, n = w_mod.shape
    bn = 1536
    return pl.pallas_call(
        _mod_kernel,
        grid=(depth, n // bn),
        in_specs=[pl.BlockSpec((8, d), lambda i, j: (0, 0)),
                  pl.BlockSpec((1, d, bn), lambda i, j: (i, 0, j)),
                  pl.BlockSpec((1, 1, bn), lambda i, j: (i, 0, j))],
        out_specs=pl.BlockSpec((1, 8, bn), lambda i, j: (i, 0, j)),
        out_shape=jax.ShapeDtypeStruct((depth, 8, n), F32),
        compiler_params=_cparams(2),
        name="modulation",
    )(cvec, w_mod, b_mod.reshape(depth, 1, n))


def _qkv_kernel(x_ref, mod_ref, g_ref, wt_ref, gq_ref, gk_ref, cos_ref, sin_ref,
                q_ref, k_ref, v_ref, *, kvw, rope):
    m = mod_ref[0, 0]
    h = _norm_mod(x_ref[0], g_ref[...], m[0:1], m[1:2]).astype(BF16)
    yt = lax.dot_general(wt_ref[...], h, (((1,), (1,)), ((), ())), preferred_element_type=F32)

    def head_norm(y, gain, n_heads):
        y3 = y.reshape(n_heads, HEAD_DIM, TM)
        ms = jnp.mean(y3 * y3, axis=1, keepdims=True)
        y3 = y3 * lax.rsqrt(ms + RMS_EPS) * gain
        if rope:
            half = HEAD_DIM // 2
            x1, x2 = y3[:, :half], y3[:, half:]
            c, s = cos_ref[0], sin_ref[0]
            y3 = jnp.concatenate([x1 * c - x2 * s, x2 * c + x1 * s], axis=1)
        return y3.reshape(n_heads * HEAD_DIM, TM)

    q = head_norm(yt[:D_MODEL], gq_ref[...], N_HEADS) * (HEAD_DIM ** -0.5 * LOG2E)
    q_ref[0, 0] = q.astype(BF16)
    k = head_norm(yt[D_MODEL:D_MODEL + kvw], gk_ref[...], kvw // HEAD_DIM)
    k_ref[0, 0] = k.T.astype(BF16)
    v_ref[0, 0] = yt[D_MODEL + kvw:].astype(BF16)


def _qkv(x, mod, g, w_qkv, g_q, g_k, cos_t, sin_t, *, rope):
    bsz, t_all, d = x.shape
    nt = t_all // TM
    n = w_qkv.shape[1]
    kvw = (n - d) // 2
    wt = w_qkv.T.astype(BF16)
    gq = jnp.broadcast_to(g_q[:, None], (HEAD_DIM, TM))
    gk = jnp.broadcast_to(g_k[:, None], (HEAD_DIM, TM))
    half = HEAD_DIM // 2
    return pl.pallas_call(
        functools.partial(_qkv_kernel, kvw=kvw, rope=rope),
        grid=(bsz, nt),
        in_specs=[pl.BlockSpec((1, TM, d), lambda b, t: (b, t, 0)),
                  pl.BlockSpec((1, 1, 6, d), lambda b, t: (b, t // (nt - 1), 0, 0)),
                  pl.BlockSpec((1, d), lambda b, t: (0, 0)),
                  _resident((n, d), lambda b, t: (0, 0)),
                  pl.BlockSpec((HEAD_DIM, TM), lambda b, t: (0, 0)),
                  pl.BlockSpec((HEAD_DIM, TM), lambda b, t: (0, 0)),
                  pl.BlockSpec((1, half, TM), lambda b, t: (t, 0, 0)),
                  pl.BlockSpec((1, half, TM), lambda b, t: (t, 0, 0))],
        out_specs=[pl.BlockSpec((1, 1, d, TM), lambda b, t: (b, t, 0, 0)),
                   pl.BlockSpec((1, 1, TM, kvw), lambda b, t: (b, t, 0, 0)),
                   pl.BlockSpec((1, 1, kvw, TM), lambda b, t: (b, t, 0, 0))],
        out_shape=[jax.ShapeDtypeStruct((bsz, nt, d, TM), BF16),
                   jax.ShapeDtypeStruct((bsz, nt, TM, kvw), BF16),
                   jax.ShapeDtypeStruct((bsz, nt, kvw, TM), BF16)],
        compiler_params=_cparams(2),
        name="qkv",
    )(x, mod, g.reshape(1, d), wt, gq, gk, cos_t, sin_t)


def _padded_queries(q_ref, first_head, n_q_heads, kv_head):
    qs = [q_ref[0, 0, (first_head + j) * HEAD_DIM:(first_head + j + 1) * HEAD_DIM, :]
          for j in range(n_q_heads)]
    qcat = qs[0] if n_q_heads == 1 else jnp.concatenate(qs, axis=1)
    zeros = jnp.zeros_like(qcat)
    return jnp.concatenate([qcat, zeros] if kv_head % 2 == 0 else [zeros, qcat], axis=0)


def _store_heads(o_ref, o, first_head, n_q_heads):
    for j in range(n_q_heads):
        o_ref[0, 0, (first_head + j) * HEAD_DIM:(first_head + j + 1) * HEAD_DIM, :] = (
            o[:, j * TM:(j + 1) * TM].astype(o_ref.dtype))


def _values_and_ones(v_rows):
    return jnp.concatenate([v_rows, jnp.ones((ONES_ROWS, v_rows.shape[1]), v_rows.dtype)], axis=0)


def _local_attn_kernel(*refs, kind, group, n_lat):
    q_ref = refs[0]
    k_refs = refs[1:5]
    v_refs = refs[5:9]
    extra_ref, o_ref = refs[9], refs[10]
    t = pl.program_id(2 if kind == 0 else 1)
    is_ctx = t >= n_lat
    valid = [jnp.logical_not(is_ctx),
             jnp.logical_and(t >= 1, jnp.logical_not(is_ctx)),
             t <= n_lat - 2,
             None]
    pen = [None if v is None else jnp.where(v, 0.0, NEG_INF).astype(F32) for v in valid]

    if kind == 1:
        rows = [(0, TM), (TM - SWA_RADIUS, SWA_RADIUS), (0, SWA_RADIUS), (0, TM)]
        band = []
        for (r0, nr), off in zip(rows[:3], (0, -1, 1)):
            key_i = lax.broadcasted_iota(jnp.int32, (nr, TM), 0) + (r0 + off * TM)
            qry_j = lax.broadcasted_iota(jnp.int32, (nr, TM), 1)
            band.append(jnp.where(jnp.abs(key_i - qry_j) <= SWA_RADIUS, 0.0, NEG_INF).astype(F32))
    else:
        rows = [(0, TM)] * 4
        tab = [jnp.where(t == 0, 0, jnp.where(t >= n_lat - 1, 6, 3)),
               jnp.where(t >= n_lat - 1, 5, 2),
               jnp.where(t == 0, 1, 4)]

    n_kv = HEADS_PER_STEP if kind == 0 else N_KV_HEADS
    for n in range(n_kv):
        pair = n // 2
        qp = _padded_queries(q_ref, n * group, group, n)
        scores = []
        for c, (r0, nr) in enumerate(rows):
            kblk = k_refs[c][0, 0, r0:r0 + nr, pair * HEAD_PAIR:(pair + 1) * HEAD_PAIR]
            s = jnp.dot(kblk, qp, preferred_element_type=F32)
            if c < 3:
                if kind == 1:
                    add = band[c] + pen[c]
                    s = s + (add if group == 1 else jnp.concatenate([add] * group, axis=1))
                else:
                    s = s + (extra_ref[n, tab[c]] + pen[c])
            scores.append(s)
        m = functools.reduce(jnp.maximum, [jnp.max(s, axis=0, keepdims=True) for s in scores])
        if kind == 1:
            sink = extra_ref[n]
            m = jnp.maximum(m, sink)
        pv = functools.reduce(jnp.add, [
            jnp.dot(_values_and_ones(v_refs[c][0, 0, n * HEAD_DIM:(n + 1) * HEAD_DIM, r0:r0 + nr]),
                    jnp.exp2(s - m).astype(BF16), preferred_element_type=F32)
            for c, ((r0, nr), s) in enumerate(zip(rows, scores))])
        l = pv[HEAD_DIM:HEAD_DIM + 1]
        if kind == 1:
            l = l + jnp.exp2(sink - m)
        _store_heads(o_ref, pv[:HEAD_DIM] / l, n * group, group)


def _local_attention(q_t, k_tok, v_t, extra, *, kind, nt_out):
    bsz, nt, d, _ = q_t.shape
    n_lat = nt - 1
    kvw = k_tok.shape[-1]
    ctx_idx = nt - 1
    kern = functools.partial(_local_attn_kernel, kind=kind,
                             group=1 if kind == 0 else N_HEADS // N_KV_HEADS, n_lat=n_lat)
    if kind == 0:
        hs = HEADS_PER_STEP
        grid = (N_HEADS // hs, bsz, nt_out)
        blk = hs * HEAD_DIM
        q_spec = pl.BlockSpec((1, 1, blk, TM), lambda g, b, t: (b, t, g, 0))
        chunk_maps = [lambda g, b, t: (b, t, 0, g),
                      lambda g, b, t: (b, jnp.maximum(t - 1, 0), 0, g),
                      lambda g, b, t: (b, jnp.minimum(t + 1, n_lat - 1), 0, g),
                      lambda g, b, t: (b, ctx_idx, 0, g)]
        k_specs = [pl.BlockSpec((1, 1, TM, blk), f) for f in chunk_maps]
        vmaps = [lambda g, b, t: (b, t, g, 0),
                 lambda g, b, t: (b, jnp.maximum(t - 1, 0), g, 0),
                 lambda g, b, t: (b, jnp.minimum(t + 1, n_lat - 1), g, 0),
                 lambda g, b, t: (b, ctx_idx, g, 0)]
        v_specs = [pl.BlockSpec((1, 1, blk, TM), f) for f in vmaps]
        extra_spec = pl.BlockSpec((hs, 7, TM, TM), lambda g, b, t: (g, 0, 0, 0))
        o_spec = pl.BlockSpec((1, 1, blk, TM), lambda g, b, t: (b, t, g, 0))
    else:
        grid = (bsz, nt_out)
        q_spec = pl.BlockSpec((1, 1, d, TM), lambda b, t: (b, t, 0, 0))
        chunk_maps = [lambda b, t: (b, t, 0, 0),
                      lambda b, t: (b, jnp.maximum(t - 1, 0), 0, 0),
                      lambda b, t: (b, jnp.minimum(t + 1, n_lat - 1), 0, 0),
                      lambda b, t: (b, ctx_idx, 0, 0)]
        k_specs = [pl.BlockSpec((1, 1, TM, kvw), f) for f in chunk_maps]
        v_specs = [pl.BlockSpec((1, 1, kvw, TM), f) for f in chunk_maps]
        extra_spec = pl.BlockSpec(extra.shape, lambda b, t: (0, 0, 0))
        o_spec = pl.BlockSpec((1, 1, d, TM), lambda b, t: (b, t, 0, 0))
    return pl.pallas_call(
        kern,
        grid=grid,
        in_specs=[q_spec] + k_specs + v_specs + [extra_spec],
        out_specs=o_spec,
        out_shape=jax.ShapeDtypeStruct((bsz, nt_out, d, TM), BF16),
        compiler_params=_cparams(len(grid)),
        name="na_attention" if kind == 0 else "swa_attention",
    )(q_t, k_tok, k_tok, k_tok, k_tok, v_t, v_t, v_t, v_t, extra)


def _na_bias_kernel(rpb_ref, o_ref, col_scr, *, combos, grid_rows):
    h = pl.program_id(0)
    n_dr, n_dc = 2 * NA_WIN_H - 1, 2 * NA_WIN_W - 1
    kc = lax.broadcasted_iota(jnp.int32, (GRID_W, 2 * GRID_W), 0)
    lane = lax.broadcasted_iota(jnp.int32, (GRID_W, 2 * GRID_W), 1)
    qc = lane & (GRID_W - 1)
    dc = kc - qc + (NA_WIN_W - 1)
    cs = jnp.clip(qc - NA_WIN_W // 2, 0, GRID_W - NA_WIN_W)
    col_ok = jnp.logical_and(kc >= cs, kc < cs + NA_WIN_W)

    def fill(a, carry):
        base = (h * n_dr + a) * n_dc
        g = jnp.full(kc.shape, rpb_ref[base], F32)
        for b in range(1, n_dc):
            g = jnp.where(dc == b, rpb_ref[base + b], g)
        col_scr[a] = jnp.where(col_ok, g * LOG2E, NEG_INF)
        return carry

    lax.fori_loop(0, n_dr, fill, 0)

    rows_per_tile = TM // GRID_W
    left = lane < GRID_W
    masked = jnp.full(kc.shape, NEG_INF, F32)
    for ci, (tile, off) in enumerate(combos):
        for kr in range(rows_per_tile):
            k_row = (tile + off) * rows_per_tile + kr
            for j in range(rows_per_tile // 2):
                halves = []
                for qr in (2 * j, 2 * j + 1):
                    q_row = tile * rows_per_tile + qr
                    rs = min(max(q_row - NA_WIN_H // 2, 0), grid_rows - NA_WIN_H)
                    ok = rs <= k_row < rs + NA_WIN_H
                    halves.append(col_scr[k_row - q_row + NA_WIN_H - 1] if ok else masked)
                o_ref[0, ci, kr * GRID_W:(kr + 1) * GRID_W, j * 2 * GRID_W:(j + 1) * 2 * GRID_W] = (
                    jnp.where(left, halves[0], halves[1]))


def _na_bias_tables(rpb, n_lat):
    n_heads = rpb.shape[0]
    combos = ((0, 0), (0, 1), (1, -1), (1, 0), (1, 1), (n_lat - 1, -1), (n_lat - 1, 0))
    return pl.pallas_call(
        functools.partial(_na_bias_kernel, combos=combos, grid_rows=n_lat * TM // GRID_W),
        grid=(n_heads,),
        in_specs=[pl.BlockSpec(memory_space=pltpu.SMEM)],
        out_specs=pl.BlockSpec((1, len(combos), TM, TM), lambda h: (h, 0, 0, 0)),
        out_shape=jax.ShapeDtypeStruct((n_heads, len(combos), TM, TM), F32),
        scratch_shapes=[pltpu.VMEM((2 * NA_WIN_H - 1, GRID_W, 2 * GRID_W), F32)],
        compiler_params=_cparams(1),
        name="na_bias",
    )(rpb.reshape(-1))


def _global_attn_kernel(q_ref, k_ref, v_ref, o_ref, s_scr, *, n_lat):
    t = pl.program_id(2)
    group = N_HEADS // N_KV_HEADS
    width = group * TM
    for n in range(2):
        qp = _padded_queries(q_ref, n * group, group, n)

        def scores(c, qp=qp):
            return jnp.dot(k_ref[0, c], qp, preferred_element_type=F32)

        def absorb(s, c, m, acc, n=n):
            m_new = jnp.maximum(m, jnp.max(s, axis=0, keepdims=True))
            p = jnp.exp2(s - m_new).astype(BF16)
            pv = jnp.dot(_values_and_ones(v_ref[0, c, n * HEAD_DIM:(n + 1) * HEAD_DIM, :]), p,
                         preferred_element_type=F32)
            return m_new, jnp.exp2(m - m_new) * acc + pv

        def finish(acc, n=n):
            _store_heads(o_ref, acc[:HEAD_DIM] / acc[HEAD_DIM:HEAD_DIM + 1], n * group, group)

        m0 = jnp.full((1, width), NEG_INF, F32)
        acc0 = jnp.zeros((HEAD_DIM + ONES_ROWS, width), F32)

        @pl.when(t < n_lat)
        def _():
            s_scr[0] = scores(0)

            def two_tiles(i, carry):
                m, acc = carry
                s_scr[1] = scores(2 * i + 1)
                m, acc = absorb(s_scr[0], 2 * i, m, acc)
                s_scr[0] = scores(2 * i + 2)
                m, acc = absorb(s_scr[1], 2 * i + 1, m, acc)
                return m, acc

            m, acc = lax.fori_loop(0, n_lat // 2, two_tiles, (m0, acc0))
            m, acc = absorb(s_scr[0], n_lat, m, acc)
            finish(acc)

        @pl.when(t >= n_lat)
        def _():
            m, acc = absorb(scores(n_lat), n_lat, m0, acc0)
            finish(acc)


def _global_attention(q_t, k_tok, v_t, *, nt_out):
    bsz, nt, d, _ = q_t.shape
    kvw = k_tok.shape[-1]
    n_pairs = kvw // HEAD_PAIR
    n_lat = nt - 1
    assert n_lat % 2 == 0
    group = N_HEADS // N_KV_HEADS
    q_rows = 2 * group * HEAD_DIM
    return pl.pallas_call(
        functools.partial(_global_attn_kernel, n_lat=n_lat),
        grid=(bsz, n_pairs, nt_out),
        in_specs=[pl.BlockSpec((1, 1, q_rows, TM), lambda b, p, t: (b, t, p, 0)),
                  pl.BlockSpec((1, nt, TM, HEAD_PAIR), lambda b, p, t: (b, 0, 0, p)),
                  pl.BlockSpec((1, nt, HEAD_PAIR, TM), lambda b, p, t: (b, 0, p, 0))],
        out_specs=pl.BlockSpec((1, 1, q_rows, TM), lambda b, p, t: (b, t, p, 0)),
        out_shape=jax.ShapeDtypeStruct((bsz, nt_out, d, TM), BF16),
        scratch_shapes=[pltpu.VMEM((2, TM, group * TM), F32)],
        compiler_params=_cparams(3),
        name="global_attention",
    )(q_t, k_tok, v_t)


def _out_proj_kernel(o_ref, x_ref, mod_ref, w_ref, y_ref):
    out_t = jnp.dot(w_ref[...], o_ref[0, 0], preferred_element_type=F32)
    y_ref[0] = x_ref[0] + mod_ref[0, 0][2:3] * out_t.T


def _out_proj(o_t, x, mod, w_o):
    bsz, nt_out, d, _ = o_t.shape
    nt = x.shape[1] // TM
    return pl.pallas_call(
        _out_proj_kernel,
        grid=(bsz, nt_out),
        in_specs=[pl.BlockSpec((1, 1, d, TM), lambda b, t: (b, t, 0, 0)),
                  pl.BlockSpec((1, TM, d), lambda b, t: (b, t, 0)),
                  pl.BlockSpec((1, 1, 6, d), lambda b, t: (b, t // (nt - 1), 0, 0)),
                  _resident((d, d), lambda b, t: (0, 0))],
        out_specs=pl.BlockSpec((1, TM, d), lambda b, t: (b, t, 0)),
        out_shape=jax.ShapeDtypeStruct((bsz, nt_out * TM, d), F32),
        compiler_params=_cparams(2),
        name="out_proj",
    )(o_t, x, mod, w_o.T.astype(BF16))


def _ffn_kernel(x_ref, xp_ref, xn_ref, mod_ref, g_ref, wup_ref, cw_ref, cb_ref, wdn_ref,
                y_ref, h_scr, a_scr, *, n_lat):
    t = pl.program_id(1)
    m = mod_ref[0, 0]
    g = g_ref[...]
    x = x_ref[0]
    h_scr[0:TM] = _norm_mod(x, g, m[3:4], m[4:5]).astype(BF16)
    halo = jnp.concatenate([xp_ref[0], xn_ref[0]], axis=0)
    h_scr[TM:TM + 2 * HALO] = _norm_mod(halo, g, m[3:4], m[4:5]).astype(BF16)
    is_ctx = t >= n_lat
    prev_ok = jnp.logical_and(t >= 1, jnp.logical_not(is_ctx))
    next_ok = t <= n_lat - 2
    row = lax.broadcasted_iota(jnp.int32, (TM, 2 * FF_CHUNK), 0)
    h = h_scr[...]
    d_ff = wdn_ref.shape[0]
    for c in range(d_ff // FF_CHUNK):
        cols = slice(c * 2 * FF_CHUNK, (c + 1) * 2 * FF_CHUNK)
        u = jnp.dot(h, wup_ref[:, cols], preferred_element_type=F32)
        um = u[0:TM]
        u_before = jnp.where(prev_ok, u[TM + HALO - 1:TM + HALO], 0.0)
        u_after = jnp.where(next_ok, u[TM + HALO:TM + HALO + 1], 0.0)
        up = jnp.where(row == 0, u_before, pltpu.roll(um, 1, 0))
        dn = jnp.where(row == TM - 1, u_after, pltpu.roll(um, TM - 1, 0))
        cw = cw_ref[:, cols]
        conv = up * cw[0:1] + um * cw[1:2] + dn * cw[2:3] + cb_ref[:, cols]
        a, gate = conv[:, :FF_CHUNK], conv[:, FF_CHUNK:]
        a_scr[:, c * FF_CHUNK:(c + 1) * FF_CHUNK] = (a * (gate * _sigmoid(gate))).astype(BF16)
    down = jnp.dot(a_scr[...], wdn_ref[...], preferred_element_type=F32)
    y_ref[0] = x + m[5:6] * down


def _chunk_interleave(w, d_ff):
    lead = w.shape[:-1]
    w = w.reshape(lead + (2, d_ff // FF_CHUNK, FF_CHUNK))
    return jnp.swapaxes(w, -3, -2).reshape(lead + (2 * d_ff,))


def _conv_ffn(x, mod, g, w_up, conv_w, conv_b, w_down, *, n_lat, nt_out):
    bsz, t_all, d = x.shape
    d_ff = w_down.shape[0]
    blocks_per_tile = TM // HALO
    last_halo = t_all // HALO - 1
    return pl.pallas_call(
        functools.partial(_ffn_kernel, n_lat=n_lat),
        grid=(bsz, nt_out),
        in_specs=[pl.BlockSpec((1, TM, d), lambda b, t: (b, t, 0)),
                  pl.BlockSpec((1, HALO, d), lambda b, t: (b, jnp.maximum(t * blocks_per_tile - 1, 0), 0)),
                  pl.BlockSpec((1, HALO, d),
                               lambda b, t: (b, jnp.minimum((t + 1) * blocks_per_tile, last_halo), 0)),
                  pl.BlockSpec((1, 1, 6, d), lambda b, t: (b, t // n_lat, 0, 0)),
                  pl.BlockSpec((1, d), lambda b, t: (0, 0)),
                  _resident((d, 2 * d_ff), lambda b, t: (0, 0)),
                  pl.BlockSpec((CONV_W, 2 * d_ff), lambda b, t: (0, 0)),
                  pl.BlockSpec((1, 2 * d_ff), lambda b, t: (0, 0)),
                  _resident((d_ff, d), lambda b, t: (0, 0))],
        out_specs=pl.BlockSpec((1, TM, d), lambda b, t: (b, t, 0)),
        out_shape=jax.ShapeDtypeStruct((bsz, nt_out * TM, d), F32),
        scratch_shapes=[pltpu.VMEM((TM + 2 * HALO, d), BF16), pltpu.VMEM((TM, d_ff), BF16)],
        compiler_params=_cparams(2),
        name="conv_ffn",
    )(x, x, x, mod, g.reshape(1, d),
      _chunk_interleave(w_up, d_ff).astype(BF16), _chunk_interleave(conv_w, d_ff),
      _chunk_interleave(conv_b, d_ff).reshape(1, 2 * d_ff), w_down.astype(BF16))


def _rope_tables(seq, n_ctx):
    t = jnp.arange(seq)
    row = (t // GRID_W).astype(F32)
    col = (t % GRID_W).astype(F32)
    n_axis = HEAD_DIM // 4
    inv = ROPE_THETA ** (-jnp.arange(n_axis, dtype=F32) / n_axis)
    ang = jnp.concatenate([row[:, None] * inv, col[:, None] * inv], axis=-1)
    cos = jnp.concatenate([jnp.cos(ang), jnp.ones((n_ctx, HEAD_DIM // 2), F32)], axis=0)
    sin = jnp.concatenate([jnp.sin(ang), jnp.zeros((n_ctx, HEAD_DIM // 2), F32)], axis=0)
    nt = (seq + n_ctx) // TM
    to_tiles = lambda a: a.reshape(nt, TM, HEAD_DIM // 2).transpose(0, 2, 1)
    return to_tiles(cos), to_tiles(sin)


def kernel(x, c, ctx, c_ctx, w_mod, b_mod, g_attn, g_ffn, na_w_qkv, na_g_q, na_g_k, na_rpb, na_w_o, swa_w_qkv, swa_g_q, swa_g_k, swa_sink, swa_w_o, ga_w_qkv, ga_g_q, ga_g_k, ga_w_o, ffn_w_up, ffn_conv_w, ffn_conv_b, ffn_w_down):
    bsz, seq, d = x.shape
    n_ctx = ctx.shape[1]
    depth = w_mod.shape[0]
    assert d == D_MODEL and seq % TM == 0 and n_ctx == TM and seq == GRID_W * GRID_W
    n_lat = seq // TM
    nt = n_lat + 1
    group = N_HEADS // N_KV_HEADS

    cvec = jnp.zeros((8, d), F32).at[:bsz].set(c).at[bsz].set(c_ctx)
    mods = _modulation(cvec, w_mod, b_mod)
    mod_lat = mods[:, :bsz].reshape(depth, bsz, 1, 6, d)
    mod_ctx = jnp.broadcast_to(mods[:, bsz].reshape(depth, 1, 1, 6, d), (depth, bsz, 1, 6, d))
    mod_all = jnp.concatenate([mod_lat, mod_ctx], axis=2)

    cos_t, sin_t = _rope_tables(seq, n_ctx)
    stream = jnp.concatenate([x, ctx], axis=1)

    for i in range(depth):
        last = i == depth - 1
        kind, j = i % N_MIXERS, i // N_MIXERS
        nt_out = n_lat if last else nt
        mod = mod_all[i]
        if kind == 0:
            w_in, w_out, gq, gk = na_w_qkv[j], na_w_o[j], na_g_q[j], na_g_k[j]
        elif kind == 1:
            w_in, w_out, gq, gk = swa_w_qkv[j], swa_w_o[j], swa_g_q[j], swa_g_k[j]
        else:
            w_in, w_out, gq, gk = ga_w_qkv[j], ga_w_o[j], ga_g_q[j], ga_g_k[j]
        q_t, k_tok, v_t = _qkv(stream, mod, g_attn[i], w_in, gq, gk, cos_t, sin_t, rope=kind != 0)
        if kind == 0:
            o_t = _local_attention(q_t, k_tok, v_t, _na_bias_tables(na_rpb[j], n_lat),
                                   kind=0, nt_out=nt_out)
        elif kind == 1:
            sink = jnp.broadcast_to((swa_sink[j] * LOG2E).reshape(N_KV_HEADS, 1, group, 1),
                                    (N_KV_HEADS, 1, group, TM)).reshape(N_KV_HEADS, 1, group * TM)
            o_t = _local_attention(q_t, k_tok, v_t, sink, kind=1, nt_out=nt_out)
        else:
            o_t = _global_attention(q_t, k_tok, v_t, nt_out=nt_out)
        stream = _out_proj(o_t, stream, mod, w_out)
        stream = _conv_ffn(stream, mod, g_ffn[i], ffn_w_up[i], ffn_conv_w[i], ffn_conv_b[i],
                           ffn_w_down[i], n_lat=n_lat, nt_out=nt_out)
    return stream
```

```python
import functools

import jax
import jax.numpy as jnp
from jax import lax
from jax.experimental import pallas as pl
from jax.experimental.pallas import tpu as pltpu

D_MODEL = 1024
GRID_W = 64
HEAD_DIM = 64
N_HEADS = D_MODEL // HEAD_DIM
N_KV_HEADS = 4
N_MIXERS = 3
NA_WIN_H = 8
NA_WIN_W = 16
SWA_RADIUS = 128
ROPE_THETA = 10000.0
CONV_W = 3
RMS_EPS = 1e-6
NEG_INF = -1e30

TM = 256
HALO = 8
FF_CHUNK = 256
HEAD_PAIR = 2 * HEAD_DIM
ONES_ROWS = 16
LOG2E = 1.4426950408889634
VMEM_LIMIT = 56 * 1024 * 1024

BF16 = jnp.bfloat16
F32 = jnp.float32


def _cparams(n_axes):
    return pltpu.CompilerParams(dimension_semantics=("arbitrary",) * n_axes,
                                vmem_limit_bytes=VMEM_LIMIT)


def _resident(block_shape, index_map):
    return pl.BlockSpec(block_shape, index_map, pipeline_mode=pl.Buffered(1))


def _sigmoid(z):
    return 1.0 / (1.0 + jnp.exp(-z))


def _norm_mod(x, g, shift, scale):
    var = jnp.mean(x * x, axis=-1, keepdims=True)
    return (x * lax.rsqrt(var + RMS_EPS) * g) * (1.0 + scale) + shift


def _mod_kernel(c_ref, w_ref, b_ref, o_ref):
    c = c_ref[...]
    sc = (c * _sigmoid(c)).astype(BF16)
    o_ref[0] = jnp.dot(sc, w_ref[0].astype(BF16), preferred_element_type=F32) + b_ref[0]


def _modulation(cvec, w_mod, b_mod):
    depth, d, n = w_mod.shape
    bn = 1536
    return pl.pallas_call(
        _mod_kernel,
        grid=(depth, n // bn),
        in_specs=[pl.BlockSpec((8, d), lambda i, j: (0, 0)),
                  pl.BlockSpec((1, d, bn), lambda i, j: (i, 0, j)),
                  pl.BlockSpec((1, 1, bn), lambda i, j: (i, 0, j))],
        out_specs=pl.BlockSpec((1, 8, bn), lambda i, j: (i, 0, j)),
        out_shape=jax.ShapeDtypeStruct((depth, 8, n), F32),
        compiler_params=_cparams(2),
        name="modulation",
    )(cvec, w_mod, b_mod.reshape(depth, 1, n))


def _qkv_kernel(x_ref, mod_ref, g_ref, wt_ref, gq_ref, gk_ref, cos_ref, sin_ref,
                q_ref, k_ref, v_ref, *, kvw, rope):
    m = mod_ref[0, 0]
    h = _norm_mod(x_ref[0], g_ref[...], m[0:1], m[1:2]).astype(BF16)
    yt = lax.dot_general(wt_ref[...], h, (((1,), (1,)), ((), ())), preferred_element_type=F32)

    def head_norm(y, gain, n_heads):
        y3 = y.reshape(n_heads, HEAD_DIM, TM)
        ms = jnp.mean(y3 * y3, axis=1, keepdims=True)
        y3 = y3 * lax.rsqrt(ms + RMS_EPS) * gain
        if rope:
            half = HEAD_DIM // 2
            x1, x2 = y3[:, :half], y3[:, half:]
            c, s = cos_ref[0], sin_ref[0]
            y3 = jnp.concatenate([x1 * c - x2 * s, x2 * c + x1 * s], axis=1)
        return y3.reshape(n_heads * HEAD_DIM, TM)

    q = head_norm(yt[:D_MODEL], gq_ref[...], N_HEADS) * (HEAD_DIM ** -0.5 * LOG2E)
    q_ref[0, 0] = q.astype(BF16)
    k = head_norm(yt[D_MODEL:D_MODEL + kvw], gk_ref[...], kvw // HEAD_DIM)
    k_ref[0, 0] = k.T.astype(BF16)
    v_ref[0, 0] = yt[D_MODEL + kvw:].astype(BF16)


def _qkv(x, mod, g, w_qkv, g_q, g_k, cos_t, sin_t, *, rope):
    bsz, t_all, d = x.shape
    nt = t_all // TM
    n = w_qkv.shape[1]
    kvw = (n - d) // 2
    wt = w_qkv.T.astype(BF16)
    gq = jnp.broadcast_to(g_q[:, None], (HEAD_DIM, TM))
    gk = jnp.broadcast_to(g_k[:, None], (HEAD_DIM, TM))
    half = HEAD_DIM // 2
    return pl.pallas_call(
        functools.partial(_qkv_kernel, kvw=kvw, rope=rope),
        grid=(bsz, nt),
        in_specs=[pl.BlockSpec((1, TM, d), lambda b, t: (b, t, 0)),
                  pl.BlockSpec((1, 1, 6, d), lambda b, t: (b, t // (nt - 1), 0, 0)),
                  pl.BlockSpec((1, d), lambda b, t: (0, 0)),
                  _resident((n, d), lambda b, t: (0, 0)),
                  pl.BlockSpec((HEAD_DIM, TM), lambda b, t: (0, 0)),
                  pl.BlockSpec((HEAD_DIM, TM), lambda b, t: (0, 0)),
                  pl.BlockSpec((1, half, TM), lambda b, t: (t, 0, 0)),
                  pl.BlockSpec((1, half, TM), lambda b, t: (t, 0, 0))],
        out_specs=[pl.BlockSpec((1, 1, d, TM), lambda b, t: (b, t, 0, 0)),
                   pl.BlockSpec((1, 1, TM, kvw), lambda b, t: (b, t, 0, 0)),
                   pl.BlockSpec((1, 1, kvw, TM), lambda b, t: (b, t, 0, 0))],
        out_shape=[jax.ShapeDtypeStruct((bsz, nt, d, TM), BF16),
                   jax.ShapeDtypeStruct((bsz, nt, TM, kvw), BF16),
                   jax.ShapeDtypeStruct((bsz, nt, kvw, TM), BF16)],
        compiler_params=_cparams(2),
        name="qkv",
    )(x, mod, g.reshape(1, d), wt, gq, gk, cos_t, sin_t)


def _padded_queries(q_ref, first_head, n_q_heads, kv_head):
    qs = [q_ref[0, 0, (first_head + j) * HEAD_DIM:(first_head + j + 1) * HEAD_DIM, :]
          for j in range(n_q_heads)]
    qcat = qs[0] if n_q_heads == 1 else jnp.concatenate(qs, axis=1)
    zeros = jnp.zeros_like(qcat)
    return jnp.concatenate([qcat, zeros] if kv_head % 2 == 0 else [zeros, qcat], axis=0)


def _store_heads(o_ref, o, first_head, n_q_heads):
    for j in range(n_q_heads):
        o_ref[0, 0, (first_head + j) * HEAD_DIM:(first_head + j + 1) * HEAD_DIM, :] = (
            o[:, j * TM:(j + 1) * TM].astype(o_ref.dtype))


def _values_and_ones(v_rows):
    return jnp.concatenate([v_rows, jnp.ones((ONES_ROWS, v_rows.shape[1]), v_rows.dtype)], axis=0)


def _project_out(o_scr, x_ref, mod_ref, wo_ref, y_ref):
    out_t = jnp.dot(wo_ref[...], o_scr[0, 0], preferred_element_type=F32)
    y_ref[0] = x_ref[0] + mod_ref[0, 0][2:3] * out_t.T


def _local_attn_kernel(*refs, kind, group, n_lat):
    q_ref = refs[0]
    k_refs = refs[1:5]
    v_refs = refs[5:9]
    extra_ref, x_ref, mod_ref, wo_ref, y_ref, o_ref = refs[9:15]
    t = pl.program_id(1)
    is_ctx = t >= n_lat
    valid = [jnp.logical_not(is_ctx),
             jnp.logical_and(t >= 1, jnp.logical_not(is_ctx)),
             t <= n_lat - 2,
             None]
    pen = [None if v is None else jnp.where(v, 0.0, NEG_INF).astype(F32) for v in valid]

    if kind == 1:
        rows = [(0, TM), (TM - SWA_RADIUS, SWA_RADIUS), (0, SWA_RADIUS), (0, TM)]
        band = []
        for (r0, nr), off in zip(rows[:3], (0, -1, 1)):
            key_i = lax.broadcasted_iota(jnp.int32, (nr, TM), 0) + (r0 + off * TM)
            qry_j = lax.broadcasted_iota(jnp.int32, (nr, TM), 1)
            band.append(jnp.where(jnp.abs(key_i - qry_j) <= SWA_RADIUS, 0.0, NEG_INF).astype(F32))
    else:
        rows = [(0, TM)] * 4
        tab = [jnp.where(t == 0, 0, jnp.where(t >= n_lat - 1, 6, 3)),
               jnp.where(t >= n_lat - 1, 5, 2),
               jnp.where(t == 0, 1, 4)]

    n_kv = N_HEADS // group

    def head_scores(n):
        pair = n // 2
        qp = _padded_queries(q_ref, n * group, group, n)
        scores = []
        for c, (r0, nr) in enumerate(rows):
            kblk = k_refs[c][0, 0, r0:r0 + nr, pair * HEAD_PAIR:(pair + 1) * HEAD_PAIR]
            s = jnp.dot(kblk, qp, preferred_element_type=F32)
            if c < 3:
                if kind == 1:
                    add = band[c] + pen[c]
                    s = s + (add if group == 1 else jnp.concatenate([add] * group, axis=1))
                else:
                    s = s + (extra_ref[n, tab[c]] + pen[c])
            scores.append(s)
        return scores

    next_scores = head_scores(0)
    for n in range(n_kv):
        scores = next_scores
        if n + 1 < n_kv:
            next_scores = head_scores(n + 1)
        m = functools.reduce(jnp.maximum, [jnp.max(s, axis=0, keepdims=True) for s in scores])
        if kind == 1:
            sink = extra_ref[n]
            m = jnp.maximum(m, sink)
        pv = functools.reduce(jnp.add, [
            jnp.dot(_values_and_ones(v_refs[c][0, 0, n * HEAD_DIM:(n + 1) * HEAD_DIM, r0:r0 + nr]),
                    jnp.exp2(s - m).astype(BF16), preferred_element_type=F32)
            for c, ((r0, nr), s) in enumerate(zip(rows, scores))])
        l = pv[HEAD_DIM:HEAD_DIM + 1]
        if kind == 1:
            l = l + jnp.exp2(sink - m)
        _store_heads(o_ref, pv[:HEAD_DIM] / l, n * group, group)
    _project_out(o_ref, x_ref, mod_ref, wo_ref, y_ref)


def _local_attention(q_t, k_tok, v_t, extra, x, mod, w_o, *, kind, nt_out):
    bsz, nt, d, _ = q_t.shape
    n_lat = nt - 1
    kvw = k_tok.shape[-1]
    ctx_idx = nt - 1
    kern = functools.partial(_local_attn_kernel, kind=kind, group=N_HEADS * HEAD_DIM // kvw, n_lat=n_lat)
    chunk_maps = [lambda b, t: (b, t, 0, 0),
                  lambda b, t: (b, jnp.maximum(t - 1, 0), 0, 0),
                  lambda b, t: (b, jnp.minimum(t + 1, n_lat - 1), 0, 0),
                  lambda b, t: (b, ctx_idx, 0, 0)]
    return pl.pallas_call(
        kern,
        grid=(bsz, nt_out),
        in_specs=([pl.BlockSpec((1, 1, d, TM), lambda b, t: (b, t, 0, 0))]
                  + [pl.BlockSpec((1, 1, TM, kvw), f) for f in chunk_maps]
                  + [pl.BlockSpec((1, 1, kvw, TM), f) for f in chunk_maps]
                  + [_resident(extra.shape, lambda b, t: (0,) * extra.ndim),
                     pl.BlockSpec((1, TM, d), lambda b, t: (b, t, 0)),
                     pl.BlockSpec((1, 1, 6, d), lambda b, t: (b, t // n_lat, 0, 0)),
                     _resident((d, d), lambda b, t: (0, 0))]),
        out_specs=pl.BlockSpec((1, TM, d), lambda b, t: (b, t, 0)),
        out_shape=jax.ShapeDtypeStruct((bsz, nt_out * TM, d), F32),
        scratch_shapes=[pltpu.VMEM((1, 1, d, TM), BF16)],
        compiler_params=_cparams(2),
        name="na_attention" if kind == 0 else "swa_attention",
    )(q_t, k_tok, k_tok, k_tok, k_tok, v_t, v_t, v_t, v_t, extra, x, mod, w_o.T.astype(BF16))


def _na_bias_kernel(rpb_ref, o_ref, col_scr, *, combos, grid_rows):
    h = pl.program_id(0)
    n_dr, n_dc = 2 * NA_WIN_H - 1, 2 * NA_WIN_W - 1
    kc = lax.broadcasted_iota(jnp.int32, (GRID_W, 2 * GRID_W), 0)
    lane = lax.broadcasted_iota(jnp.int32, (GRID_W, 2 * GRID_W), 1)
    qc = lane & (GRID_W - 1)
    dc = kc - qc + (NA_WIN_W - 1)
    cs = jnp.clip(qc - NA_WIN_W // 2, 0, GRID_W - NA_WIN_W)
    col_ok = jnp.logical_and(kc >= cs, kc < cs + NA_WIN_W)

    def fill(a, carry):
        base = (h * n_dr + a) * n_dc
        g = jnp.full(kc.shape, rpb_ref[base], F32)
        for b in range(1, n_dc):
            g = jnp.where(dc == b, rpb_ref[base + b], g)
        col_scr[a] = jnp.where(col_ok, g * LOG2E, NEG_INF)
        return carry

    lax.fori_loop(0, n_dr, fill, 0)

    rows_per_tile = TM // GRID_W
    left = lane < GRID_W
    masked = jnp.full(kc.shape, NEG_INF, F32)
    for ci, (tile, off) in enumerate(combos):
        for kr in range(rows_per_tile):
            k_row = (tile + off) * rows_per_tile + kr
            for j in range(rows_per_tile // 2):
                halves = []
                for qr in (2 * j, 2 * j + 1):
                    q_row = tile * rows_per_tile + qr
                    rs = min(max(q_row - NA_WIN_H // 2, 0), grid_rows - NA_WIN_H)
                    ok = rs <= k_row < rs + NA_WIN_H
                    halves.append(col_scr[k_row - q_row + NA_WIN_H - 1] if ok else masked)
                o_ref[0, ci, kr * GRID_W:(kr + 1) * GRID_W, j * 2 * GRID_W:(j + 1) * 2 * GRID_W] = (
                    jnp.where(left, halves[0], halves[1]))


def _na_bias_tables(rpb, n_lat):
    n_heads = rpb.shape[0]
    combos = ((0, 0), (0, 1), (1, -1), (1, 0), (1, 1), (n_lat - 1, -1), (n_lat - 1, 0))
    return pl.pallas_call(
        functools.partial(_na_bias_kernel, combos=combos, grid_rows=n_lat * TM // GRID_W),
        grid=(n_heads,),
        in_specs=[pl.BlockSpec(memory_space=pltpu.SMEM)],
        out_specs=pl.BlockSpec((1, len(combos), TM, TM), lambda h: (h, 0, 0, 0)),
        out_shape=jax.ShapeDtypeStruct((n_heads, len(combos), TM, TM), F32),
        scratch_shapes=[pltpu.VMEM((2 * NA_WIN_H - 1, GRID_W, 2 * GRID_W), F32)],
        compiler_params=_cparams(1),
        name="na_bias",
    )(rpb.reshape(-1))


def _global_attn_kernel(q_ref, k_ref, v_ref, x_ref, mod_ref, wo_ref, y_ref, o_ref, s_scr, *, n_lat):
    t = pl.program_id(1)
    group = N_HEADS // N_KV_HEADS
    width = group * TM
    for n in range(N_KV_HEADS):
        qp = _padded_queries(q_ref, n * group, group, n)
        lanes = slice((n // 2) * HEAD_PAIR, (n // 2 + 1) * HEAD_PAIR)

        def scores(c, qp=qp, lanes=lanes):
            return jnp.dot(k_ref[0, c, :, lanes], qp, preferred_element_type=F32)

        def absorb(s, c, m, acc, n=n):
            m_new = jnp.maximum(m, jnp.max(s, axis=0, keepdims=True))
            p = jnp.exp2(s - m_new).astype(BF16)
            pv = jnp.dot(_values_and_ones(v_ref[0, c, n * HEAD_DIM:(n + 1) * HEAD_DIM, :]), p,
                         preferred_element_type=F32)
            return m_new, jnp.exp2(m - m_new) * acc + pv

        def finish(acc, n=n):
            _store_heads(o_ref, acc[:HEAD_DIM] / acc[HEAD_DIM:HEAD_DIM + 1], n * group, group)

        m0 = jnp.full((1, width), NEG_INF, F32)
        acc0 = jnp.zeros((HEAD_DIM + ONES_ROWS, width), F32)

        @pl.when(t < n_lat)
        def _():
            s_scr[0] = scores(0)

            def two_tiles(i, carry):
                m, acc = carry
                s_scr[1] = scores(2 * i + 1)
                m, acc = absorb(s_scr[0], 2 * i, m, acc)
                s_scr[0] = scores(2 * i + 2)
                m, acc = absorb(s_scr[1], 2 * i + 1, m, acc)
                return m, acc

            m, acc = lax.fori_loop(0, n_lat // 2, two_tiles, (m0, acc0), unroll=True)
            m, acc = absorb(s_scr[0], n_lat, m, acc)
            finish(acc)

        @pl.when(t >= n_lat)
        def _():
            m, acc = absorb(scores(n_lat), n_lat, m0, acc0)
            finish(acc)

    _project_out(o_ref, x_ref, mod_ref, wo_ref, y_ref)


def _global_attention(q_t, k_tok, v_t, x, mod, w_o, *, nt_out):
    bsz, nt, d, _ = q_t.shape
    kvw = k_tok.shape[-1]
    n_lat = nt - 1
    assert n_lat % 2 == 0 and kvw == N_KV_HEADS * HEAD_DIM
    return pl.pallas_call(
        functools.partial(_global_attn_kernel, n_lat=n_lat),
        grid=(bsz, nt_out),
        in_specs=[pl.BlockSpec((1, 1, d, TM), lambda b, t: (b, t, 0, 0)),
                  pl.BlockSpec((1, nt, TM, kvw), lambda b, t: (b, 0, 0, 0)),
                  pl.BlockSpec((1, nt, kvw, TM), lambda b, t: (b, 0, 0, 0)),
                  pl.BlockSpec((1, TM, d), lambda b, t: (b, t, 0)),
                  pl.BlockSpec((1, 1, 6, d), lambda b, t: (b, t // n_lat, 0, 0)),
                  _resident((d, d), lambda b, t: (0, 0))],
        out_specs=pl.BlockSpec((1, TM, d), lambda b, t: (b, t, 0)),
        out_shape=jax.ShapeDtypeStruct((bsz, nt_out * TM, d), F32),
        scratch_shapes=[pltpu.VMEM((1, 1, d, TM), BF16),
                        pltpu.VMEM((2, TM, N_HEADS // N_KV_HEADS * TM), F32)],
        compiler_params=_cparams(2),
        name="global_attention",
    )(q_t, k_tok, v_t, x, mod, w_o.T.astype(BF16))


def _ffn_kernel(x_ref, xp_ref, xn_ref, mod_ref, g_ref, wup_ref, cw_ref, cb_ref, wdn_ref,
                y_ref, h_scr, a_scr, *, n_lat):
    t = pl.program_id(1)
    m = mod_ref[0, 0]
    g = g_ref[...]
    x = x_ref[0]
    h_scr[0:TM] = _norm_mod(x, g, m[3:4], m[4:5]).astype(BF16)
    halo = jnp.concatenate([xp_ref[0], xn_ref[0]], axis=0)
    h_scr[TM:TM + 2 * HALO] = _norm_mod(halo, g, m[3:4], m[4:5]).astype(BF16)
    is_ctx = t >= n_lat
    prev_ok = jnp.logical_and(t >= 1, jnp.logical_not(is_ctx))
    next_ok = t <= n_lat - 2
    row = lax.broadcasted_iota(jnp.int32, (TM, FF_CHUNK), 0)
    h = h_scr[...]
    d_ff = wdn_ref.shape[0]

    def up_conv(cols):
        u = jnp.dot(h, wup_ref[:, cols], preferred_element_type=F32)
        um = u[0:TM]
        u_before = jnp.where(prev_ok, u[TM + HALO - 1:TM + HALO], 0.0)
        u_after = jnp.where(next_ok, u[TM + HALO:TM + HALO + 1], 0.0)
        up = jnp.where(row == 0, u_before, pltpu.roll(um, 1, 0))
        dn = jnp.where(row == TM - 1, u_after, pltpu.roll(um, TM - 1, 0))
        cw = cw_ref[:, cols]
        return up * cw[0:1] + um * cw[1:2] + dn * cw[2:3] + cb_ref[:, cols]

    for c in range(d_ff // FF_CHUNK):
        a = up_conv(slice(c * FF_CHUNK, (c + 1) * FF_CHUNK))
        gate = up_conv(slice(d_ff + c * FF_CHUNK, d_ff + (c + 1) * FF_CHUNK))
        a_scr[:, c * FF_CHUNK:(c + 1) * FF_CHUNK] = (a * (gate * _sigmoid(gate))).astype(BF16)
    down = jnp.dot(a_scr[...], wdn_ref[...], preferred_element_type=F32)
    y_ref[0] = x + m[5:6] * down


def _conv_ffn(x, mod, g, w_up, conv_w, conv_b, w_down, *, n_lat, nt_out):
    bsz, t_all, d = x.shape
    d_ff = w_down.shape[0]
    blocks_per_tile = TM // HALO
    last_halo = t_all // HALO - 1
    return pl.pallas_call(
        functools.partial(_ffn_kernel, n_lat=n_lat),
        grid=(bsz, nt_out),
        in_specs=[pl.BlockSpec((1, TM, d), lambda b, t: (b, t, 0)),
                  pl.BlockSpec((1, HALO, d), lambda b, t: (b, jnp.maximum(t * blocks_per_tile - 1, 0), 0)),
                  pl.BlockSpec((1, HALO, d),
                               lambda b, t: (b, jnp.minimum((t + 1) * blocks_per_tile, last_halo), 0)),
                  pl.BlockSpec((1, 1, 6, d), lambda b, t: (b, t // n_lat, 0, 0)),
                  pl.BlockSpec((1, d), lambda b, t: (0, 0)),
                  _resident((d, 2 * d_ff), lambda b, t: (0, 0)),
                  pl.BlockSpec((CONV_W, 2 * d_ff), lambda b, t: (0, 0)),
                  pl.BlockSpec((1, 2 * d_ff), lambda b, t: (0, 0)),
                  _resident((d_ff, d), lambda b, t: (0, 0))],
        out_specs=pl.BlockSpec((1, TM, d), lambda b, t: (b, t, 0)),
        out_shape=jax.ShapeDtypeStruct((bsz, nt_out * TM, d), F32),
        scratch_shapes=[pltpu.VMEM((TM + 2 * HALO, d), BF16), pltpu.VMEM((TM, d_ff), BF16)],
        compiler_params=_cparams(2),
        name="conv_ffn",
    )(x, x, x, mod, g.reshape(1, d),
      w_up.astype(BF16), conv_w, conv_b.reshape(1, 2 * d_ff), w_down.astype(BF16))


def _rope_tables(seq, n_ctx):
    t = jnp.arange(seq)
    row = (t // GRID_W).astype(F32)
    col = (t % GRID_W).astype(F32)
    n_axis = HEAD_DIM // 4
    inv = ROPE_THETA ** (-jnp.arange(n_axis, dtype=F32) / n_axis)
    ang = jnp.concatenate([row[:, None] * inv, col[:, None] * inv], axis=-1)
    cos = jnp.concatenate([jnp.cos(ang), jnp.ones((n_ctx, HEAD_DIM // 2), F32)], axis=0)
    sin = jnp.concatenate([jnp.sin(ang), jnp.zeros((n_ctx, HEAD_DIM // 2), F32)], axis=0)
    nt = (seq + n_ctx) // TM
    to_tiles = lambda a: a.reshape(nt, TM, HEAD_DIM // 2).transpose(0, 2, 1)
    return to_tiles(cos), to_tiles(sin)


def kernel(x, c, ctx, c_ctx, w_mod, b_mod, g_attn, g_ffn, na_w_qkv, na_g_q, na_g_k, na_rpb, na_w_o, swa_w_qkv, swa_g_q, swa_g_k, swa_sink, swa_w_o, ga_w_qkv, ga_g_q, ga_g_k, ga_w_o, ffn_w_up, ffn_conv_w, ffn_conv_b, ffn_w_down):
    bsz, seq, d = x.shape
    n_ctx = ctx.shape[1]
    depth = w_mod.shape[0]
    assert d == D_MODEL and seq % TM == 0 and n_ctx == TM and seq == GRID_W * GRID_W
    n_lat = seq // TM
    nt = n_lat + 1
    group = N_HEADS // N_KV_HEADS

    cvec = jnp.zeros((8, d), F32).at[:bsz].set(c).at[bsz].set(c_ctx)
    mods = _modulation(cvec, w_mod, b_mod)
    mod_lat = mods[:, :bsz].reshape(depth, bsz, 1, 6, d)
    mod_ctx = jnp.broadcast_to(mods[:, bsz].reshape(depth, 1, 1, 6, d), (depth, bsz, 1, 6, d))
    mod_all = jnp.concatenate([mod_lat, mod_ctx], axis=2)

    cos_t, sin_t = _rope_tables(seq, n_ctx)
    stream = jnp.concatenate([x, ctx], axis=1)

    for i in range(depth):
        last = i == depth - 1
        kind, j = i % N_MIXERS, i // N_MIXERS
        nt_out = n_lat if last else nt
        mod = mod_all[i]
        if kind == 0:
            w_in, w_out, gq, gk = na_w_qkv[j], na_w_o[j], na_g_q[j], na_g_k[j]
        elif kind == 1:
            w_in, w_out, gq, gk = swa_w_qkv[j], swa_w_o[j], swa_g_q[j], swa_g_k[j]
        else:
            w_in, w_out, gq, gk = ga_w_qkv[j], ga_w_o[j], ga_g_q[j], ga_g_k[j]
        q_t, k_tok, v_t = _qkv(stream, mod, g_attn[i], w_in, gq, gk, cos_t, sin_t, rope=kind != 0)
        if kind == 0:
            stream = _local_attention(q_t, k_tok, v_t, _na_bias_tables(na_rpb[j], n_lat),
                                      stream, mod, w_out, kind=0, nt_out=nt_out)
        elif kind == 1:
            sink = jnp.broadcast_to((swa_sink[j] * LOG2E).reshape(N_KV_HEADS, 1, group, 1),
                                    (N_KV_HEADS, 1, group, TM)).reshape(N_KV_HEADS, 1, group * TM)
            stream = _local_attention(q_t, k_tok, v_t, sink, stream, mod, w_out, kind=1, nt_out=nt_out)
        else:
            stream = _global_attention(q_t, k_tok, v_t, stream, mod, w_out, nt_out=nt_out)
        stream = _conv_ffn(stream, mod, g_ffn[i], ffn_w_up[i], ffn_conv_w[i], ffn_conv_b[i],
                           ffn_w_down[i], n_lat=n_lat, nt_out=nt_out)
    return stream
```

```python
import functools

import jax
import jax.numpy as jnp
from jax import lax
from jax.experimental import pallas as pl
from jax.experimental.pallas import tpu as pltpu

D_MODEL = 1024
GRID_W = 64
HEAD_DIM = 64
N_HEADS = D_MODEL // HEAD_DIM
N_KV_HEADS = 4
N_MIXERS = 3
NA_WIN_H = 8
NA_WIN_W = 16
SWA_RADIUS = 128
ROPE_THETA = 10000.0
CONV_W = 3
RMS_EPS = 1e-6
NEG_INF = -1e30

TM = 256
HALO = 8
FF_CHUNK = 256
HEAD_PAIR = 2 * HEAD_DIM
SCORES_AHEAD = 2
ONES_ROWS = 16
LOG2E = 1.4426950408889634
VMEM_LIMIT = 56 * 1024 * 1024

BF16 = jnp.bfloat16
F32 = jnp.float32


def _cparams(n_axes):
    return pltpu.CompilerParams(dimension_semantics=("arbitrary",) * n_axes,
                                vmem_limit_bytes=VMEM_LIMIT)


def _resident(block_shape, index_map):
    return pl.BlockSpec(block_shape, index_map, pipeline_mode=pl.Buffered(1))


def _sigmoid(z):
    return 1.0 / (1.0 + jnp.exp(-z))


def _norm_mod(x, g, shift, scale):
    var = jnp.mean(x * x, axis=-1, keepdims=True)
    return (x * lax.rsqrt(var + RMS_EPS) * g) * (1.0 + scale) + shift


def _mod_kernel(c_ref, w_ref, b_ref, o_ref):
    c = c_ref[...]
    sc = (c * _sigmoid(c)).astype(BF16)
    o_ref[0] = jnp.dot(sc, w_ref[0].astype(BF16), preferred_element_type=F32) + b_ref[0]


def _modulation(cvec, w_mod, b_mod):
    depth, d, n = w_mod.shape
    bn = 1536
    return pl.pallas_call(
        _mod_kernel,
        grid=(depth, n // bn),
        in_specs=[pl.BlockSpec((8, d), lambda i, j: (0, 0)),
                  pl.BlockSpec((1, d, bn), lambda i, j: (i, 0, j)),
                  pl.BlockSpec((1, 1, bn), lambda i, j: (i, 0, j))],
        out_specs=pl.BlockSpec((1, 8, bn), lambda i, j: (i, 0, j)),
        out_shape=jax.ShapeDtypeStruct((depth, 8, n), F32),
        compiler_params=_cparams(2),
        name="modulation",
    )(cvec, w_mod, b_mod.reshape(depth, 1, n))


def _qkv_kernel(x_ref, mod_ref, g_ref, wt_ref, gq_ref, gk_ref, cos_ref, sin_ref,
                q_ref, k_ref, v_ref, *, kvw, rope):
    m = mod_ref[0, 0]
    h = _norm_mod(x_ref[0], g_ref[...], m[0:1], m[1:2]).astype(BF16)
    yt = lax.dot_general(wt_ref[...], h, (((1,), (1,)), ((), ())), preferred_element_type=F32)

    def head_norm(y, gain, n_heads):
        y3 = y.reshape(n_heads, HEAD_DIM, TM)
        ms = jnp.mean(y3 * y3, axis=1, keepdims=True)
        y3 = y3 * lax.rsqrt(ms + RMS_EPS) * gain
        if rope:
            half = HEAD_DIM // 2
            x1, x2 = y3[:, :half], y3[:, half:]
            c, s = cos_ref[0], sin_ref[0]
            y3 = jnp.concatenate([x1 * c - x2 * s, x2 * c + x1 * s], axis=1)
        return y3.reshape(n_heads * HEAD_DIM, TM)

    q = head_norm(yt[:D_MODEL], gq_ref[...], N_HEADS) * (HEAD_DIM ** -0.5 * LOG2E)
    q_ref[0, 0] = q.astype(BF16)
    k = head_norm(yt[D_MODEL:D_MODEL + kvw], gk_ref[...], kvw // HEAD_DIM)
    k_ref[0, 0] = k.T.astype(BF16)
    v_ref[0, 0] = yt[D_MODEL + kvw:].astype(BF16)


def _qkv(x, mod, g, w_qkv, g_q, g_k, cos_t, sin_t, *, rope):
    bsz, t_all, d = x.shape
    nt = t_all // TM
    n = w_qkv.shape[1]
    kvw = (n - d) // 2
    wt = w_qkv.T.astype(BF16)
    gq = jnp.broadcast_to(g_q[:, None], (HEAD_DIM, TM))
    gk = jnp.broadcast_to(g_k[:, None], (HEAD_DIM, TM))
    half = HEAD_DIM // 2
    return pl.pallas_call(
        functools.partial(_qkv_kernel, kvw=kvw, rope=rope),
        grid=(bsz, nt),
        in_specs=[pl.BlockSpec((1, TM, d), lambda b, t: (b, t, 0)),
                  pl.BlockSpec((1, 1, 6, d), lambda b, t: (b, t // (nt - 1), 0, 0)),
                  pl.BlockSpec((1, d), lambda b, t: (0, 0)),
                  _resident((n, d), lambda b, t: (0, 0)),
                  pl.BlockSpec((HEAD_DIM, TM), lambda b, t: (0, 0)),
                  pl.BlockSpec((HEAD_DIM, TM), lambda b, t: (0, 0)),
                  pl.BlockSpec((1, half, TM), lambda b, t: (t, 0, 0)),
                  pl.BlockSpec((1, half, TM), lambda b, t: (t, 0, 0))],
        out_specs=[pl.BlockSpec((1, 1, d, TM), lambda b, t: (b, t, 0, 0)),
                   pl.BlockSpec((1, 1, TM, kvw), lambda b, t: (b, t, 0, 0)),
                   pl.BlockSpec((1, 1, kvw, TM), lambda b, t: (b, t, 0, 0))],
        out_shape=[jax.ShapeDtypeStruct((bsz, nt, d, TM), BF16),
                   jax.ShapeDtypeStruct((bsz, nt, TM, kvw), BF16),
                   jax.ShapeDtypeStruct((bsz, nt, kvw, TM), BF16)],
        compiler_params=_cparams(2),
        name="qkv",
    )(x, mod, g.reshape(1, d), wt, gq, gk, cos_t, sin_t)


def _padded_queries(q_ref, first_head, n_q_heads, kv_head):
    qs = [q_ref[0, 0, (first_head + j) * HEAD_DIM:(first_head + j + 1) * HEAD_DIM, :]
          for j in range(n_q_heads)]
    qcat = qs[0] if n_q_heads == 1 else jnp.concatenate(qs, axis=1)
    zeros = jnp.zeros_like(qcat)
    return jnp.concatenate([qcat, zeros] if kv_head % 2 == 0 else [zeros, qcat], axis=0)


def _store_heads(o_ref, o, first_head, n_q_heads):
    for j in range(n_q_heads):
        o_ref[0, 0, (first_head + j) * HEAD_DIM:(first_head + j + 1) * HEAD_DIM, :] = (
            o[:, j * TM:(j + 1) * TM].astype(o_ref.dtype))


def _values_and_ones(v_rows):
    return jnp.concatenate([v_rows, jnp.ones((ONES_ROWS, v_rows.shape[1]), v_rows.dtype)], axis=0)


def _project_out(o_scr, x_ref, mod_ref, wo_ref, y_ref):
    out_t = jnp.dot(wo_ref[...], o_scr[0, 0], preferred_element_type=F32)
    y_ref[0] = x_ref[0] + mod_ref[0, 0][2:3] * out_t.T


def _local_attn_kernel(*refs, kind, group, n_lat):
    q_ref = refs[0]
    k_refs = refs[1:5]
    v_refs = refs[5:9]
    extra_ref, x_ref, mod_ref, wo_ref, y_ref, o_ref = refs[9:15]
    t = pl.program_id(1)
    is_ctx = t >= n_lat
    valid = [jnp.logical_not(is_ctx),
             jnp.logical_and(t >= 1, jnp.logical_not(is_ctx)),
             t <= n_lat - 2,
             None]
    pen = [None if v is None else jnp.where(v, 0.0, NEG_INF).astype(F32) for v in valid]

    if kind == 1:
        rows = [(0, TM), (TM - SWA_RADIUS, SWA_RADIUS), (0, SWA_RADIUS), (0, TM)]
        band = []
        for (r0, nr), off in zip(rows[:3], (0, -1, 1)):
            key_i = lax.broadcasted_iota(jnp.int32, (nr, TM), 0) + (r0 + off * TM)
            qry_j = lax.broadcasted_iota(jnp.int32, (nr, TM), 1)
            band.append(jnp.where(jnp.abs(key_i - qry_j) <= SWA_RADIUS, 0.0, NEG_INF).astype(F32))
    else:
        rows = [(0, TM)] * 4
        tab = [jnp.where(t == 0, 0, jnp.where(t >= n_lat - 1, 6, 3)),
               jnp.where(t >= n_lat - 1, 5, 2),
               jnp.where(t == 0, 1, 4)]

    n_kv = N_HEADS // group

    def head_scores(n):
        pair = n // 2
        qp = _padded_queries(q_ref, n * group, group, n)
        scores = []
        for c, (r0, nr) in enumerate(rows):
            kblk = k_refs[c][0, 0, r0:r0 + nr, pair * HEAD_PAIR:(pair + 1) * HEAD_PAIR]
            s = jnp.dot(kblk, qp, preferred_element_type=F32)
            if c < 3:
                if kind == 1:
                    add = band[c] + pen[c]
                    s = s + (add if group == 1 else jnp.concatenate([add] * group, axis=1))
                else:
                    s = s + (extra_ref[n, tab[c]] + pen[c])
            scores.append(s)
        return scores

    pending = [head_scores(n) for n in range(min(SCORES_AHEAD, n_kv))]
    for n in range(n_kv):
        scores = pending.pop(0)
        if n + SCORES_AHEAD < n_kv:
            pending.append(head_scores(n + SCORES_AHEAD))
        m = functools.reduce(jnp.maximum, [jnp.max(s, axis=0, keepdims=True) for s in scores])
        if kind == 1:
            sink = extra_ref[n]
            m = jnp.maximum(m, sink)
        pv = functools.reduce(jnp.add, [
            jnp.dot(_values_and_ones(v_refs[c][0, 0, n * HEAD_DIM:(n + 1) * HEAD_DIM, r0:r0 + nr]),
                    jnp.exp2(s - m).astype(BF16), preferred_element_type=F32)
            for c, ((r0, nr), s) in enumerate(zip(rows, scores))])
        l = pv[HEAD_DIM:HEAD_DIM + 1]
        if kind == 1:
            l = l + jnp.exp2(sink - m)
        _store_heads(o_ref, pv[:HEAD_DIM] / l, n * group, group)
    _project_out(o_ref, x_ref, mod_ref, wo_ref, y_ref)


def _local_attention(q_t, k_tok, v_t, extra, x, mod, w_o, *, kind, nt_out):
    bsz, nt, d, _ = q_t.shape
    n_lat = nt - 1
    kvw = k_tok.shape[-1]
    ctx_idx = nt - 1
    kern = functools.partial(_local_attn_kernel, kind=kind, group=N_HEADS * HEAD_DIM // kvw, n_lat=n_lat)
    chunk_maps = [lambda b, t: (b, t, 0, 0),
                  lambda b, t: (b, jnp.maximum(t - 1, 0), 0, 0),
                  lambda b, t: (b, jnp.minimum(t + 1, n_lat - 1), 0, 0),
                  lambda b, t: (b, ctx_idx, 0, 0)]
    return pl.pallas_call(
        kern,
        grid=(bsz, nt_out),
        in_specs=([pl.BlockSpec((1, 1, d, TM), lambda b, t: (b, t, 0, 0))]
                  + [pl.BlockSpec((1, 1, TM, kvw), f) for f in chunk_maps]
                  + [pl.BlockSpec((1, 1, kvw, TM), f) for f in chunk_maps]
                  + [_resident(extra.shape, lambda b, t: (0,) * extra.ndim),
                     pl.BlockSpec((1, TM, d), lambda b, t: (b, t, 0)),
                     pl.BlockSpec((1, 1, 6, d), lambda b, t: (b, t // n_lat, 0, 0)),
                     _resident((d, d), lambda b, t: (0, 0))]),
        out_specs=pl.BlockSpec((1, TM, d), lambda b, t: (b, t, 0)),
        out_shape=jax.ShapeDtypeStruct((bsz, nt_out * TM, d), F32),
        scratch_shapes=[pltpu.VMEM((1, 1, d, TM), BF16)],
        compiler_params=_cparams(2),
        name="na_attention" if kind == 0 else "swa_attention",
    )(q_t, k_tok, k_tok, k_tok, k_tok, v_t, v_t, v_t, v_t, extra, x, mod, w_o.T.astype(BF16))


def _na_bias_kernel(rpb_ref, o_ref, col_scr, *, combos, grid_rows):
    h = pl.program_id(0)
    n_dr, n_dc = 2 * NA_WIN_H - 1, 2 * NA_WIN_W - 1
    kc = lax.broadcasted_iota(jnp.int32, (GRID_W, 2 * GRID_W), 0)
    lane = lax.broadcasted_iota(jnp.int32, (GRID_W, 2 * GRID_W), 1)
    qc = lane & (GRID_W - 1)
    dc = kc - qc + (NA_WIN_W - 1)
    cs = jnp.clip(qc - NA_WIN_W // 2, 0, GRID_W - NA_WIN_W)
    col_ok = jnp.logical_and(kc >= cs, kc < cs + NA_WIN_W)

    def fill(a, carry):
        base = (h * n_dr + a) * n_dc
        g = jnp.full(kc.shape, rpb_ref[base], F32)
        for b in range(1, n_dc):
            g = jnp.where(dc == b, rpb_ref[base + b], g)
        col_scr[a] = jnp.where(col_ok, g * LOG2E, NEG_INF)
        return carry

    lax.fori_loop(0, n_dr, fill, 0)

    rows_per_tile = TM // GRID_W
    left = lane < GRID_W
    masked = jnp.full(kc.shape, NEG_INF, F32)
    for ci, (tile, off) in enumerate(combos):
        for kr in range(rows_per_tile):
            k_row = (tile + off) * rows_per_tile + kr
            for j in range(rows_per_tile // 2):
                halves = []
                for qr in (2 * j, 2 * j + 1):
                    q_row = tile * rows_per_tile + qr
                    rs = min(max(q_row - NA_WIN_H // 2, 0), grid_rows - NA_WIN_H)
                    ok = rs <= k_row < rs + NA_WIN_H
                    halves.append(col_scr[k_row - q_row + NA_WIN_H - 1] if ok else masked)
                o_ref[0, ci, kr * GRID_W:(kr + 1) * GRID_W, j * 2 * GRID_W:(j + 1) * 2 * GRID_W] = (
                    jnp.where(left, halves[0], halves[1]))


def _na_bias_tables(rpb, n_lat):
    n_heads = rpb.shape[0]
    combos = ((0, 0), (0, 1), (1, -1), (1, 0), (1, 1), (n_lat - 1, -1), (n_lat - 1, 0))
    return pl.pallas_call(
        functools.partial(_na_bias_kernel, combos=combos, grid_rows=n_lat * TM // GRID_W),
        grid=(n_heads,),
        in_specs=[pl.BlockSpec(memory_space=pltpu.SMEM)],
        out_specs=pl.BlockSpec((1, len(combos), TM, TM), lambda h: (h, 0, 0, 0)),
        out_shape=jax.ShapeDtypeStruct((n_heads, len(combos), TM, TM), F32),
        scratch_shapes=[pltpu.VMEM((2 * NA_WIN_H - 1, GRID_W, 2 * GRID_W), F32)],
        compiler_params=_cparams(1),
        name="na_bias",
    )(rpb.reshape(-1))


def _global_attn_kernel(q_ref, k_ref, v_ref, x_ref, mod_ref, wo_ref, y_ref, o_ref, s_scr, *, n_lat):
    t = pl.program_id(1)
    group = N_HEADS // N_KV_HEADS
    width = group * TM
    for n in range(N_KV_HEADS):
        qp = _padded_queries(q_ref, n * group, group, n)
        lanes = slice((n // 2) * HEAD_PAIR, (n // 2 + 1) * HEAD_PAIR)

        def scores(c, qp=qp, lanes=lanes):
            return jnp.dot(k_ref[0, c, :, lanes], qp, preferred_element_type=F32)

        def absorb(s, c, m, acc, n=n):
            m_new = jnp.maximum(m, jnp.max(s, axis=0, keepdims=True))
            p = jnp.exp2(s - m_new).astype(BF16)
            pv = jnp.dot(_values_and_ones(v_ref[0, c, n * HEAD_DIM:(n + 1) * HEAD_DIM, :]), p,
                         preferred_element_type=F32)
            return m_new, jnp.exp2(m - m_new) * acc + pv

        def finish(acc, n=n):
            _store_heads(o_ref, acc[:HEAD_DIM] / acc[HEAD_DIM:HEAD_DIM + 1], n * group, group)

        m0 = jnp.full((1, width), NEG_INF, F32)
        acc0 = jnp.zeros((HEAD_DIM + ONES_ROWS, width), F32)

        @pl.when(t < n_lat)
        def _():
            s_scr[0] = scores(0)

            def two_tiles(i, carry):
                m, acc = carry
                s_scr[1] = scores(2 * i + 1)
                m, acc = absorb(s_scr[0], 2 * i, m, acc)
                s_scr[0] = scores(2 * i + 2)
                m, acc = absorb(s_scr[1], 2 * i + 1, m, acc)
                return m, acc

            m, acc = lax.fori_loop(0, n_lat // 2, two_tiles, (m0, acc0), unroll=True)
            m, acc = absorb(s_scr[0], n_lat, m, acc)
            finish(acc)

        @pl.when(t >= n_lat)
        def _():
            m, acc = absorb(scores(n_lat), n_lat, m0, acc0)
            finish(acc)

    _project_out(o_ref, x_ref, mod_ref, wo_ref, y_ref)


def _global_attention(q_t, k_tok, v_t, x, mod, w_o, *, nt_out):
    bsz, nt, d, _ = q_t.shape
    kvw = k_tok.shape[-1]
    n_lat = nt - 1
    assert n_lat % 2 == 0 and kvw == N_KV_HEADS * HEAD_DIM
    return pl.pallas_call(
        functools.partial(_global_attn_kernel, n_lat=n_lat),
        grid=(bsz, nt_out),
        in_specs=[pl.BlockSpec((1, 1, d, TM), lambda b, t: (b, t, 0, 0)),
                  pl.BlockSpec((1, nt, TM, kvw), lambda b, t: (b, 0, 0, 0)),
                  pl.BlockSpec((1, nt, kvw, TM), lambda b, t: (b, 0, 0, 0)),
                  pl.BlockSpec((1, TM, d), lambda b, t: (b, t, 0)),
                  pl.BlockSpec((1, 1, 6, d), lambda b, t: (b, t // n_lat, 0, 0)),
                  _resident((d, d), lambda b, t: (0, 0))],
        out_specs=pl.BlockSpec((1, TM, d), lambda b, t: (b, t, 0)),
        out_shape=jax.ShapeDtypeStruct((bsz, nt_out * TM, d), F32),
        scratch_shapes=[pltpu.VMEM((1, 1, d, TM), BF16),
                        pltpu.VMEM((2, TM, N_HEADS // N_KV_HEADS * TM), F32)],
        compiler_params=_cparams(2),
        name="global_attention",
    )(q_t, k_tok, v_t, x, mod, w_o.T.astype(BF16))


def _ffn_kernel(x_ref, xp_ref, xn_ref, mod0_ref, mod1_ref, g_ref, wup_ref, cw_ref, cb_ref, wdn_ref,
                y_ref, h_scr, a_scr, *, n_lat, tiles_per_batch):
    rows = 2 * TM
    j = pl.program_id(0)
    g = g_ref[...]
    mods = (mod0_ref[0, 0], mod1_ref[0, 0])
    x = x_ref[...]
    for s in range(2):
        h_scr[s * TM:(s + 1) * TM] = _norm_mod(
            x[s * TM:(s + 1) * TM], g, mods[s][3:4], mods[s][4:5]).astype(BF16)
    h_scr[rows:rows + 2 * HALO] = jnp.concatenate(
        [_norm_mod(xp_ref[...], g, mods[0][3:4], mods[0][4:5]),
         _norm_mod(xn_ref[...], g, mods[1][3:4], mods[1][4:5])], axis=0).astype(BF16)

    tile = [(2 * j + s) % tiles_per_batch for s in range(2)]
    prev_ok = [jnp.logical_and(t >= 1, t < n_lat) for t in tile]
    next_ok = [t <= n_lat - 2 for t in tile]
    row = lax.broadcasted_iota(jnp.int32, (rows, FF_CHUNK), 0)
    seam_cut_up = jnp.logical_and(row == TM, jnp.logical_not(prev_ok[1]))
    seam_cut_dn = jnp.logical_and(row == TM - 1, jnp.logical_not(next_ok[0]))
    h = h_scr[...]
    d_ff = wdn_ref.shape[0]

    def up_conv(cols):
        u = jnp.dot(h, wup_ref[:, cols], preferred_element_type=F32)
        um = u[0:rows]
        u_before = jnp.where(prev_ok[0], u[rows + HALO - 1:rows + HALO], 0.0)
        u_after = jnp.where(next_ok[1], u[rows + HALO:rows + HALO + 1], 0.0)
        up = jnp.where(row == 0, u_before, jnp.where(seam_cut_up, 0.0, pltpu.roll(um, 1, 0)))
        dn = jnp.where(row == rows - 1, u_after, jnp.where(seam_cut_dn, 0.0, pltpu.roll(um, rows - 1, 0)))
        cw = cw_ref[:, cols]
        return up * cw[0:1] + um * cw[1:2] + dn * cw[2:3] + cb_ref[:, cols]

    for c in range(d_ff // FF_CHUNK):
        a = up_conv(slice(c * FF_CHUNK, (c + 1) * FF_CHUNK))
        gate = up_conv(slice(d_ff + c * FF_CHUNK, d_ff + (c + 1) * FF_CHUNK))
        a_scr[:, c * FF_CHUNK:(c + 1) * FF_CHUNK] = (a * (gate * _sigmoid(gate))).astype(BF16)
    down = jnp.dot(a_scr[...], wdn_ref[...], preferred_element_type=F32)
    for s in range(2):
        y_ref[s * TM:(s + 1) * TM] = x[s * TM:(s + 1) * TM] + mods[s][5:6] * down[s * TM:(s + 1) * TM]


def _conv_ffn(x, mod, g, w_up, conv_w, conv_b, w_down, *, n_lat):
    bsz, t_all, d = x.shape
    tiles_per_batch = t_all // TM
    n_steps = bsz * tiles_per_batch // 2
    assert bsz * tiles_per_batch % 2 == 0
    d_ff = w_down.shape[0]
    rows = 2 * TM
    halos_per_step = rows // HALO
    last_halo = bsz * t_all // HALO - 1

    def mod_map(s):
        def index(j):
            tile = 2 * j + s
            return (tile // tiles_per_batch, (tile % tiles_per_batch) // n_lat, 0, 0)
        return index

    out = pl.pallas_call(
        functools.partial(_ffn_kernel, n_lat=n_lat, tiles_per_batch=tiles_per_batch),
        grid=(n_steps,),
        in_specs=[pl.BlockSpec((rows, d), lambda j: (j, 0)),
                  pl.BlockSpec((HALO, d), lambda j: (jnp.maximum(j * halos_per_step - 1, 0), 0)),
                  pl.BlockSpec((HALO, d), lambda j: (jnp.minimum((j + 1) * halos_per_step, last_halo), 0)),
                  pl.BlockSpec((1, 1, 6, d), mod_map(0)),
                  pl.BlockSpec((1, 1, 6, d), mod_map(1)),
                  pl.BlockSpec((1, d), lambda j: (0, 0)),
                  _resident((d, 2 * d_ff), lambda j: (0, 0)),
                  pl.BlockSpec((CONV_W, 2 * d_ff), lambda j: (0, 0)),
                  pl.BlockSpec((1, 2 * d_ff), lambda j: (0, 0)),
                  _resident((d_ff, d), lambda j: (0, 0))],
        out_specs=pl.BlockSpec((rows, d), lambda j: (j, 0)),
        out_shape=jax.ShapeDtypeStruct((bsz * t_all, d), F32),
        scratch_shapes=[pltpu.VMEM((rows + 2 * HALO, d), BF16), pltpu.VMEM((rows, d_ff), BF16)],
        compiler_params=_cparams(1),
        name="conv_ffn",
    )(*(x.reshape(bsz * t_all, d),) * 3, mod, mod, g.reshape(1, d),
      w_up.astype(BF16), conv_w, conv_b.reshape(1, 2 * d_ff), w_down.astype(BF16))
    return out.reshape(bsz, t_all, d)


def _rope_tables(seq, n_ctx):
    t = jnp.arange(seq)
    row = (t // GRID_W).astype(F32)
    col = (t % GRID_W).astype(F32)
    n_axis = HEAD_DIM // 4
    inv = ROPE_THETA ** (-jnp.arange(n_axis, dtype=F32) / n_axis)
    ang = jnp.concatenate([row[:, None] * inv, col[:, None] * inv], axis=-1)
    cos = jnp.concatenate([jnp.cos(ang), jnp.ones((n_ctx, HEAD_DIM // 2), F32)], axis=0)
    sin = jnp.concatenate([jnp.sin(ang), jnp.zeros((n_ctx, HEAD_DIM // 2), F32)], axis=0)
    nt = (seq + n_ctx) // TM
    to_tiles = lambda a: a.reshape(nt, TM, HEAD_DIM // 2).transpose(0, 2, 1)
    return to_tiles(cos), to_tiles(sin)


def kernel(x, c, ctx, c_ctx, w_mod, b_mod, g_attn, g_ffn, na_w_qkv, na_g_q, na_g_k, na_rpb, na_w_o, swa_w_qkv, swa_g_q, swa_g_k, swa_sink, swa_w_o, ga_w_qkv, ga_g_q, ga_g_k, ga_w_o, ffn_w_up, ffn_conv_w, ffn_conv_b, ffn_w_down):
    bsz, seq, d = x.shape
    n_ctx = ctx.shape[1]
    depth = w_mod.shape[0]
    assert d == D_MODEL and seq % TM == 0 and n_ctx == TM and seq == GRID_W * GRID_W
    n_lat = seq // TM
    nt = n_lat + 1
    group = N_HEADS // N_KV_HEADS

    cvec = jnp.zeros((8, d), F32).at[:bsz].set(c).at[bsz].set(c_ctx)
    mods = _modulation(cvec, w_mod, b_mod)
    mod_lat = mods[:, :bsz].reshape(depth, bsz, 1, 6, d)
    mod_ctx = jnp.broadcast_to(mods[:, bsz].reshape(depth, 1, 1, 6, d), (depth, bsz, 1, 6, d))
    mod_all = jnp.concatenate([mod_lat, mod_ctx], axis=2)

    cos_t, sin_t = _rope_tables(seq, n_ctx)
    stream = jnp.concatenate([x, ctx], axis=1)

    for i in range(depth):
        last = i == depth - 1
        kind, j = i % N_MIXERS, i // N_MIXERS
        nt_out = n_lat if last else nt
        mod = mod_all[i]
        if kind == 0:
            w_in, w_out, gq, gk = na_w_qkv[j], na_w_o[j], na_g_q[j], na_g_k[j]
        elif kind == 1:
            w_in, w_out, gq, gk = swa_w_qkv[j], swa_w_o[j], swa_g_q[j], swa_g_k[j]
        else:
            w_in, w_out, gq, gk = ga_w_qkv[j], ga_w_o[j], ga_g_q[j], ga_g_k[j]
        q_t, k_tok, v_t = _qkv(stream, mod, g_attn[i], w_in, gq, gk, cos_t, sin_t, rope=kind != 0)
        if kind == 0:
            stream = _local_attention(q_t, k_tok, v_t, _na_bias_tables(na_rpb[j], n_lat),
                                      stream, mod, w_out, kind=0, nt_out=nt_out)
        elif kind == 1:
            sink = jnp.broadcast_to((swa_sink[j] * LOG2E).reshape(N_KV_HEADS, 1, group, 1),
                                    (N_KV_HEADS, 1, group, TM)).reshape(N_KV_HEADS, 1, group * TM)
            stream = _local_attention(q_t, k_tok, v_t, sink, stream, mod, w_out, kind=1, nt_out=nt_out)
        else:
            stream = _global_attention(q_t, k_tok, v_t, stream, mod, w_out, nt_out=nt_out)
        stream = _conv_ffn(stream, mod, g_ffn[i], ffn_w_up[i], ffn_conv_w[i], ffn_conv_b[i],
                           ffn_w_down[i], n_lat=n_lat)
    return stream
```

```python
import functools

import jax
import jax.numpy as jnp
from jax import lax
from jax.experimental import pallas as pl
from jax.experimental.pallas import tpu as pltpu

D_MODEL = 1024
GRID_W = 64
HEAD_DIM = 64
N_HEADS = D_MODEL // HEAD_DIM
N_KV_HEADS = 4
N_MIXERS = 3
NA_WIN_H = 8
NA_WIN_W = 16
SWA_RADIUS = 128
ROPE_THETA = 10000.0
CONV_W = 3
RMS_EPS = 1e-6
NEG_INF = -1e30

TM = 256
HALO = 8
FF_CHUNK = 256
HEAD_PAIR = 2 * HEAD_DIM
SCORES_AHEAD = 2
ONES_ROWS = 16
LOG2E = 1.4426950408889634
VMEM_LIMIT = 56 * 1024 * 1024

BF16 = jnp.bfloat16
F32 = jnp.float32


def _cparams(n_axes):
    return pltpu.CompilerParams(dimension_semantics=("arbitrary",) * n_axes,
                                vmem_limit_bytes=VMEM_LIMIT)


def _resident(block_shape, index_map):
    return pl.BlockSpec(block_shape, index_map, pipeline_mode=pl.Buffered(1))


def _sigmoid(z):
    return 1.0 / (1.0 + jnp.exp(-z))


def _norm_mod(x, g, shift, scale):
    var = jnp.mean(x * x, axis=-1, keepdims=True)
    return (x * lax.rsqrt(var + RMS_EPS) * g) * (1.0 + scale) + shift


def _mod_kernel(c_ref, w_ref, b_ref, o_ref):
    c = c_ref[...]
    sc = (c * _sigmoid(c)).astype(BF16)
    o_ref[0] = jnp.dot(sc, w_ref[0].astype(BF16), preferred_element_type=F32) + b_ref[0]


def _modulation(cvec, w_mod, b_mod):
    depth, d, n = w_mod.shape
    bn = 1536
    return pl.pallas_call(
        _mod_kernel,
        grid=(depth, n // bn),
        in_specs=[pl.BlockSpec((8, d), lambda i, j: (0, 0)),
                  pl.BlockSpec((1, d, bn), lambda i, j: (i, 0, j)),
                  pl.BlockSpec((1, 1, bn), lambda i, j: (i, 0, j))],
        out_specs=pl.BlockSpec((1, 8, bn), lambda i, j: (i, 0, j)),
        out_shape=jax.ShapeDtypeStruct((depth, 8, n), F32),
        compiler_params=_cparams(2),
        name="modulation",
    )(cvec, w_mod, b_mod.reshape(depth, 1, n))


def _qkv_kernel(x_ref, mod0_ref, mod1_ref, g_ref, wt_ref, gq_ref, gk_ref,
                cos0_ref, sin0_ref, cos1_ref, sin1_ref, q_ref, k_ref, v_ref, *, kvw, rope):
    g = g_ref[...]
    x = x_ref[...]
    mods = (mod0_ref[0, 0], mod1_ref[0, 0])
    h = jnp.concatenate(
        [_norm_mod(x[s * TM:(s + 1) * TM], g, mods[s][0:1], mods[s][1:2]).astype(BF16) for s in range(2)],
        axis=0)
    def project(rows):
        return lax.dot_general(wt_ref[rows], h, (((1,), (1,)), ((), ())), preferred_element_type=F32)

    yq = project(slice(0, D_MODEL))
    yk = project(slice(D_MODEL, D_MODEL + kvw))
    yv = project(slice(D_MODEL + kvw, D_MODEL + 2 * kvw))

    def head_norm(y, gain, n_heads, cos_sin):
        y3 = y.reshape(n_heads, HEAD_DIM, TM)
        ms = jnp.mean(y3 * y3, axis=1, keepdims=True)
        y3 = y3 * lax.rsqrt(ms + RMS_EPS) * gain
        if rope:
            half = HEAD_DIM // 2
            x1, x2 = y3[:, :half], y3[:, half:]
            c, s = cos_sin
            y3 = jnp.concatenate([x1 * c - x2 * s, x2 * c + x1 * s], axis=1)
        return y3.reshape(n_heads * HEAD_DIM, TM)

    tables = ((cos0_ref[0], sin0_ref[0]), (cos1_ref[0], sin1_ref[0]))
    for s in range(2):
        q = head_norm(yq[:, s * TM:(s + 1) * TM], gq_ref[...], N_HEADS, tables[s])
        q_ref[s] = (q * (HEAD_DIM ** -0.5 * LOG2E)).astype(BF16)
    for s in range(2):
        k = head_norm(yk[:, s * TM:(s + 1) * TM], gk_ref[...], kvw // HEAD_DIM, tables[s])
        k_ref[s] = k.T.astype(BF16)
    for s in range(2):
        v_ref[s] = yv[:, s * TM:(s + 1) * TM].astype(BF16)


def _qkv(x, mod, g, w_qkv, g_q, g_k, cos_t, sin_t, *, rope):
    bsz, t_all, d = x.shape
    nt = t_all // TM
    n_tiles = bsz * nt
    assert n_tiles % 2 == 0
    n = w_qkv.shape[1]
    kvw = (n - d) // 2
    wt = w_qkv.T.astype(BF16)
    gq = jnp.broadcast_to(g_q[:, None], (HEAD_DIM, TM))
    gk = jnp.broadcast_to(g_k[:, None], (HEAD_DIM, TM))
    half = HEAD_DIM // 2

    def mod_map(s):
        return lambda j: ((2 * j + s) // nt, ((2 * j + s) % nt) // (nt - 1), 0, 0)

    def rope_map(s):
        return lambda j: ((2 * j + s) % nt, 0, 0)

    q_t, k_tok, v_t = pl.pallas_call(
        functools.partial(_qkv_kernel, kvw=kvw, rope=rope),
        grid=(n_tiles // 2,),
        in_specs=[pl.BlockSpec((2 * TM, d), lambda j: (j, 0)),
                  pl.BlockSpec((1, 1, 6, d), mod_map(0)),
                  pl.BlockSpec((1, 1, 6, d), mod_map(1)),
                  pl.BlockSpec((1, d), lambda j: (0, 0)),
                  _resident((n, d), lambda j: (0, 0)),
                  pl.BlockSpec((HEAD_DIM, TM), lambda j: (0, 0)),
                  pl.BlockSpec((HEAD_DIM, TM), lambda j: (0, 0)),
                  pl.BlockSpec((1, half, TM), rope_map(0)),
                  pl.BlockSpec((1, half, TM), rope_map(0)),
                  pl.BlockSpec((1, half, TM), rope_map(1)),
                  pl.BlockSpec((1, half, TM), rope_map(1))],
        out_specs=[pl.BlockSpec((2, d, TM), lambda j: (j, 0, 0)),
                   pl.BlockSpec((2, TM, kvw), lambda j: (j, 0, 0)),
                   pl.BlockSpec((2, kvw, TM), lambda j: (j, 0, 0))],
        out_shape=[jax.ShapeDtypeStruct((n_tiles, d, TM), BF16),
                   jax.ShapeDtypeStruct((n_tiles, TM, kvw), BF16),
                   jax.ShapeDtypeStruct((n_tiles, kvw, TM), BF16)],
        compiler_params=_cparams(1),
        name="qkv",
    )(x.reshape(bsz * t_all, d), mod, mod, g.reshape(1, d), wt, gq, gk, cos_t, sin_t, cos_t, sin_t)
    return (q_t.reshape(bsz, nt, d, TM), k_tok.reshape(bsz, nt, TM, kvw), v_t.reshape(bsz, nt, kvw, TM))


def _k_lanes(kv_head):
    g = kv_head // (HEAD_PAIR // HEAD_DIM)
    return slice(g * HEAD_PAIR, (g + 1) * HEAD_PAIR)


def _padded_queries(q_ref, first_head, n_q_heads, kv_head):
    qs = [q_ref[0, 0, (first_head + j) * HEAD_DIM:(first_head + j + 1) * HEAD_DIM, :]
          for j in range(n_q_heads)]
    qcat = qs[0] if n_q_heads == 1 else jnp.concatenate(qs, axis=1)
    slots = HEAD_PAIR // HEAD_DIM
    blocks = [jnp.zeros_like(qcat)] * slots
    blocks[kv_head % slots] = qcat
    return jnp.concatenate(blocks, axis=0)


def _store_heads(o_ref, o, first_head, n_q_heads):
    for j in range(n_q_heads):
        o_ref[0, 0, (first_head + j) * HEAD_DIM:(first_head + j + 1) * HEAD_DIM, :] = (
            o[:, j * TM:(j + 1) * TM].astype(o_ref.dtype))


def _project_out(o_scr, x_ref, mod_ref, wo_ref, y_ref):
    out_t = jnp.dot(wo_ref[...], o_scr[0, 0], preferred_element_type=F32)
    y_ref[0] = x_ref[0] + mod_ref[0, 0][2:3] * out_t.T


def _values_and_ones(v_rows):
    return jnp.concatenate([v_rows, jnp.ones((ONES_ROWS, v_rows.shape[1]), v_rows.dtype)], axis=0)


def _local_attn_kernel(*refs, kind, group, n_lat):
    q_ref = refs[0]
    k_refs = refs[1:5]
    v_refs = refs[5:9]
    extra_ref, x_ref, mod_ref, wo_ref, y_ref, o_ref = refs[9:15]
    t = pl.program_id(1)
    is_ctx = t >= n_lat
    valid = [jnp.logical_not(is_ctx),
             jnp.logical_and(t >= 1, jnp.logical_not(is_ctx)),
             t <= n_lat - 2,
             None]
    pen = [None if v is None else jnp.where(v, 0.0, NEG_INF).astype(F32) for v in valid]

    if kind == 1:
        rows = [(0, TM), (TM - SWA_RADIUS, SWA_RADIUS), (0, SWA_RADIUS), (0, TM)]
        band = []
        for (r0, nr), off in zip(rows[:3], (0, -1, 1)):
            key_i = lax.broadcasted_iota(jnp.int32, (nr, TM), 0) + (r0 + off * TM)
            qry_j = lax.broadcasted_iota(jnp.int32, (nr, TM), 1)
            band.append(jnp.where(jnp.abs(key_i - qry_j) <= SWA_RADIUS, 0.0, NEG_INF).astype(F32))
    else:
        rows = [(0, TM)] * 4
        tab = [jnp.where(t == 0, 0, jnp.where(t >= n_lat - 1, 6, 3)),
               jnp.where(t >= n_lat - 1, 5, 2),
               jnp.where(t == 0, 1, 4)]

    n_kv = N_HEADS // group

    def head_scores(n):
        qp = _padded_queries(q_ref, n * group, group, n)
        scores = []
        for c, (r0, nr) in enumerate(rows):
            kblk = k_refs[c][0, 0, r0:r0 + nr, _k_lanes(n)]
            s = jnp.dot(kblk, qp, preferred_element_type=F32)
            if c < 3:
                if kind == 1:
                    add = band[c] + pen[c]
                    s = s + (add if group == 1 else jnp.concatenate([add] * group, axis=1))
                else:
                    s = (s + extra_ref[n, tab[c]]) + pen[c]
            scores.append(s)
        return scores

    pending = [head_scores(n) for n in range(min(SCORES_AHEAD, n_kv))]
    for n in range(n_kv):
        scores = pending.pop(0)
        if n + SCORES_AHEAD < n_kv:
            pending.append(head_scores(n + SCORES_AHEAD))
        m = functools.reduce(jnp.maximum, [jnp.max(s, axis=0, keepdims=True) for s in scores])
        if kind == 1:
            sink = extra_ref[n]
            m = jnp.maximum(m, sink)
        pv = functools.reduce(jnp.add, [
            jnp.dot(_values_and_ones(v_refs[c][0, 0, n * HEAD_DIM:(n + 1) * HEAD_DIM, r0:r0 + nr]),
                    jnp.exp2(s - m).astype(BF16), preferred_element_type=F32)
            for c, ((r0, nr), s) in enumerate(zip(rows, scores))])
        l = pv[HEAD_DIM:HEAD_DIM + 1]
        if kind == 1:
            l = l + jnp.exp2(sink - m)
        _store_heads(o_ref, pv[:HEAD_DIM] / l, n * group, group)
    _project_out(o_ref, x_ref, mod_ref, wo_ref, y_ref)


def _local_attention(q_t, k_tok, v_t, extra, x, mod, w_o, *, kind, nt_out):
    bsz, nt, d, _ = q_t.shape
    n_lat = nt - 1
    kvw = k_tok.shape[-1]
    ctx_idx = nt - 1
    kern = functools.partial(_local_attn_kernel, kind=kind, group=N_HEADS * HEAD_DIM // kvw, n_lat=n_lat)
    chunk_maps = [lambda b, t: (b, t, 0, 0),
                  lambda b, t: (b, jnp.maximum(t - 1, 0), 0, 0),
                  lambda b, t: (b, jnp.minimum(t + 1, n_lat - 1), 0, 0),
                  lambda b, t: (b, ctx_idx, 0, 0)]
    return pl.pallas_call(
        kern,
        grid=(bsz, nt_out),
        in_specs=([pl.BlockSpec((1, 1, d, TM), lambda b, t: (b, t, 0, 0))]
                  + [pl.BlockSpec((1, 1, TM, kvw), f) for f in chunk_maps]
                  + [pl.BlockSpec((1, 1, kvw, TM), f) for f in chunk_maps]
                  + [_resident(extra.shape, lambda b, t: (0,) * extra.ndim),
                     pl.BlockSpec((1, TM, d), lambda b, t: (b, t, 0)),
                     pl.BlockSpec((1, 1, 6, d), lambda b, t: (b, t // n_lat, 0, 0)),
                     _resident((d, d), lambda b, t: (0, 0))]),
        out_specs=pl.BlockSpec((1, TM, d), lambda b, t: (b, t, 0)),
        out_shape=jax.ShapeDtypeStruct((bsz, nt_out * TM, d), F32),
        scratch_shapes=[pltpu.VMEM((1, 1, d, TM), BF16)],
        compiler_params=_cparams(2),
        name="na_attention" if kind == 0 else "swa_attention",
    )(q_t, k_tok, k_tok, k_tok, k_tok, v_t, v_t, v_t, v_t, extra, x, mod, w_o.T.astype(BF16))


def _na_bias_kernel(rpb_ref, o_ref, col_scr, *, combos, grid_rows):
    h = pl.program_id(0)
    n_dr, n_dc = 2 * NA_WIN_H - 1, 2 * NA_WIN_W - 1
    kc = lax.broadcasted_iota(jnp.int32, (GRID_W, 2 * GRID_W), 0)
    lane = lax.broadcasted_iota(jnp.int32, (GRID_W, 2 * GRID_W), 1)
    qc = lane & (GRID_W - 1)
    dc = kc - qc + (NA_WIN_W - 1)
    cs = jnp.clip(qc - NA_WIN_W // 2, 0, GRID_W - NA_WIN_W)
    col_ok = jnp.logical_and(kc >= cs, kc < cs + NA_WIN_W)

    def fill(a, carry):
        base = (h * n_dr + a) * n_dc
        g = jnp.full(kc.shape, rpb_ref[base], F32)
        for b in range(1, n_dc):
            g = jnp.where(dc == b, rpb_ref[base + b], g)
        col_scr[a] = jnp.where(col_ok, g * LOG2E, NEG_INF)
        return carry

    lax.fori_loop(0, n_dr, fill, 0)

    rows_per_tile = TM // GRID_W
    left = lane < GRID_W
    masked = jnp.full(kc.shape, NEG_INF, F32)
    for ci, (tile, off) in enumerate(combos):
        for kr in range(rows_per_tile):
            k_row = (tile + off) * rows_per_tile + kr
            for j in range(rows_per_tile // 2):
                halves = []
                for qr in (2 * j, 2 * j + 1):
                    q_row = tile * rows_per_tile + qr
                    rs = min(max(q_row - NA_WIN_H // 2, 0), grid_rows - NA_WIN_H)
                    ok = rs <= k_row < rs + NA_WIN_H
                    halves.append(col_scr[k_row - q_row + NA_WIN_H - 1] if ok else masked)
                o_ref[0, ci, kr * GRID_W:(kr + 1) * GRID_W, j * 2 * GRID_W:(j + 1) * 2 * GRID_W] = (
                    jnp.where(left, halves[0], halves[1]))


def _na_bias_tables(rpb, n_lat):
    n_heads = rpb.shape[0]
    combos = ((0, 0), (0, 1), (1, -1), (1, 0), (1, 1), (n_lat - 1, -1), (n_lat - 1, 0))
    return pl.pallas_call(
        functools.partial(_na_bias_kernel, combos=combos, grid_rows=n_lat * TM // GRID_W),
        grid=(n_heads,),
        in_specs=[pl.BlockSpec(memory_space=pltpu.SMEM)],
        out_specs=pl.BlockSpec((1, len(combos), TM, TM), lambda h: (h, 0, 0, 0)),
        out_shape=jax.ShapeDtypeStruct((n_heads, len(combos), TM, TM), F32),
        scratch_shapes=[pltpu.VMEM((2 * NA_WIN_H - 1, GRID_W, 2 * GRID_W), F32)],
        compiler_params=_cparams(1),
        name="na_bias",
    )(rpb.reshape(-1))


def _global_attn_kernel(q_ref, k_ref, v_ref, x_ref, mod_ref, wo_ref, y_ref, o_ref, s_scr, *, n_lat):
    t = pl.program_id(1)
    group = N_HEADS // N_KV_HEADS
    width = group * TM
    for n in range(N_KV_HEADS):
        qp = _padded_queries(q_ref, n * group, group, n)
        lanes = _k_lanes(n)

        def scores(c, qp=qp, lanes=lanes):
            return jnp.dot(k_ref[0, c, :, lanes], qp, preferred_element_type=F32)

        def absorb(s, c, m, acc, n=n):
            m_new = jnp.maximum(m, jnp.max(s, axis=0, keepdims=True))
            p = jnp.exp2(s - m_new).astype(BF16)
            pv = jnp.dot(_values_and_ones(v_ref[0, c, n * HEAD_DIM:(n + 1) * HEAD_DIM, :]), p,
                         preferred_element_type=F32)
            return m_new, jnp.exp2(m - m_new) * acc + pv

        def finish(acc, n=n):
            _store_heads(o_ref, acc[:HEAD_DIM] / acc[HEAD_DIM:HEAD_DIM + 1], n * group, group)

        m0 = jnp.full((1, width), NEG_INF, F32)
        acc0 = jnp.zeros((HEAD_DIM + ONES_ROWS, width), F32)

        @pl.when(t < n_lat)
        def _():
            s_scr[0] = scores(0)

            def two_tiles(i, carry):
                m, acc = carry
                s_scr[1] = scores(2 * i + 1)
                m, acc = absorb(s_scr[0], 2 * i, m, acc)
                s_scr[0] = scores(2 * i + 2)
                m, acc = absorb(s_scr[1], 2 * i + 1, m, acc)
                return m, acc

            m, acc = lax.fori_loop(0, n_lat // 2, two_tiles, (m0, acc0), unroll=True)
            m, acc = absorb(s_scr[0], n_lat, m, acc)
            finish(acc)

        @pl.when(t >= n_lat)
        def _():
            m, acc = absorb(scores(n_lat), n_lat, m0, acc0)
            finish(acc)

    _project_out(o_ref, x_ref, mod_ref, wo_ref, y_ref)


def _global_attention(q_t, k_tok, v_t, x, mod, w_o, *, nt_out):
    bsz, nt, d, _ = q_t.shape
    kvw = k_tok.shape[-1]
    n_lat = nt - 1
    assert n_lat % 2 == 0 and kvw == N_KV_HEADS * HEAD_DIM
    return pl.pallas_call(
        functools.partial(_global_attn_kernel, n_lat=n_lat),
        grid=(bsz, nt_out),
        in_specs=[pl.BlockSpec((1, 1, d, TM), lambda b, t: (b, t, 0, 0)),
                  pl.BlockSpec((1, nt, TM, kvw), lambda b, t: (b, 0, 0, 0)),
                  pl.BlockSpec((1, nt, kvw, TM), lambda b, t: (b, 0, 0, 0)),
                  pl.BlockSpec((1, TM, d), lambda b, t: (b, t, 0)),
                  pl.BlockSpec((1, 1, 6, d), lambda b, t: (b, t // n_lat, 0, 0)),
                  _resident((d, d), lambda b, t: (0, 0))],
        out_specs=pl.BlockSpec((1, TM, d), lambda b, t: (b, t, 0)),
        out_shape=jax.ShapeDtypeStruct((bsz, nt_out * TM, d), F32),
        scratch_shapes=[pltpu.VMEM((1, 1, d, TM), BF16),
                        pltpu.VMEM((2, TM, N_HEADS // N_KV_HEADS * TM), F32)],
        compiler_params=_cparams(2),
        name="global_attention",
    )(q_t, k_tok, v_t, x, mod, w_o.T.astype(BF16))


def _ffn_kernel(x_ref, xp_ref, xn_ref, mod0_ref, mod1_ref, g_ref, wup_ref, cw_ref, cb_ref, wdn_ref,
                y_ref, h_scr, a_scr, *, n_lat, tiles_per_batch):
    rows = 2 * TM
    j = pl.program_id(0)
    g = g_ref[...]
    mods = (mod0_ref[0, 0], mod1_ref[0, 0])
    x = x_ref[...]
    for s in range(2):
        h_scr[s * TM:(s + 1) * TM] = _norm_mod(
            x[s * TM:(s + 1) * TM], g, mods[s][3:4], mods[s][4:5]).astype(BF16)
    h_scr[rows:rows + 2 * HALO] = jnp.concatenate(
        [_norm_mod(xp_ref[...], g, mods[0][3:4], mods[0][4:5]),
         _norm_mod(xn_ref[...], g, mods[1][3:4], mods[1][4:5])], axis=0).astype(BF16)

    tile = [(2 * j + s) % tiles_per_batch for s in range(2)]
    prev_ok = [jnp.logical_and(t >= 1, t < n_lat) for t in tile]
    next_ok = [t <= n_lat - 2 for t in tile]
    row = lax.broadcasted_iota(jnp.int32, (rows, FF_CHUNK), 0)
    seam_cut_up = jnp.logical_and(row == TM, jnp.logical_not(prev_ok[1]))
    seam_cut_dn = jnp.logical_and(row == TM - 1, jnp.logical_not(next_ok[0]))
    h = h_scr[...]
    d_ff = wdn_ref.shape[0]

    def up_conv(cols):
        u = jnp.dot(h, wup_ref[:, cols], preferred_element_type=F32)
        um = u[0:rows]
        u_before = jnp.where(prev_ok[0], u[rows + HALO - 1:rows + HALO], 0.0)
        u_after = jnp.where(next_ok[1], u[rows + HALO:rows + HALO + 1], 0.0)
        up = jnp.where(row == 0, u_before, jnp.where(seam_cut_up, 0.0, pltpu.roll(um, 1, 0)))
        dn = jnp.where(row == rows - 1, u_after, jnp.where(seam_cut_dn, 0.0, pltpu.roll(um, rows - 1, 0)))
        cw = cw_ref[:, cols]
        return up * cw[0:1] + um * cw[1:2] + dn * cw[2:3] + cb_ref[:, cols]

    for c in range(d_ff // FF_CHUNK):
        a = up_conv(slice(c * FF_CHUNK, (c + 1) * FF_CHUNK))
        gate = up_conv(slice(d_ff + c * FF_CHUNK, d_ff + (c + 1) * FF_CHUNK))
        a_scr[:, c * FF_CHUNK:(c + 1) * FF_CHUNK] = (a * (gate * _sigmoid(gate))).astype(BF16)
    down = jnp.dot(a_scr[...], wdn_ref[...], preferred_element_type=F32)
    for s in range(2):
        y_ref[s * TM:(s + 1) * TM] = x[s * TM:(s + 1) * TM] + mods[s][5:6] * down[s * TM:(s + 1) * TM]


def _conv_ffn(x, mod, g, w_up, conv_w, conv_b, w_down, *, n_lat):
    bsz, t_all, d = x.shape
    tiles_per_batch = t_all // TM
    n_steps = bsz * tiles_per_batch // 2
    assert bsz * tiles_per_batch % 2 == 0
    d_ff = w_down.shape[0]
    rows = 2 * TM
    halos_per_step = rows // HALO
    last_halo = bsz * t_all // HALO - 1

    def mod_map(s):
        def index(j):
            tile = 2 * j + s
            return (tile // tiles_per_batch, (tile % tiles_per_batch) // n_lat, 0, 0)
        return index

    out = pl.pallas_call(
        functools.partial(_ffn_kernel, n_lat=n_lat, tiles_per_batch=tiles_per_batch),
        grid=(n_steps,),
        in_specs=[pl.BlockSpec((rows, d), lambda j: (j, 0)),
                  pl.BlockSpec((HALO, d), lambda j: (jnp.maximum(j * halos_per_step - 1, 0), 0)),
                  pl.BlockSpec((HALO, d), lambda j: (jnp.minimum((j + 1) * halos_per_step, last_halo), 0)),
                  pl.BlockSpec((1, 1, 6, d), mod_map(0)),
                  pl.BlockSpec((1, 1, 6, d), mod_map(1)),
                  pl.BlockSpec((1, d), lambda j: (0, 0)),
                  _resident((d, 2 * d_ff), lambda j: (0, 0)),
                  pl.BlockSpec((CONV_W, 2 * d_ff), lambda j: (0, 0)),
                  pl.BlockSpec((1, 2 * d_ff), lambda j: (0, 0)),
                  _resident((d_ff, d), lambda j: (0, 0))],
        out_specs=pl.BlockSpec((rows, d), lambda j: (j, 0)),
        out_shape=jax.ShapeDtypeStruct((bsz * t_all, d), F32),
        scratch_shapes=[pltpu.VMEM((rows + 2 * HALO, d), BF16), pltpu.VMEM((rows, d_ff), BF16)],
        compiler_params=_cparams(1),
        name="conv_ffn",
    )(*(x.reshape(bsz * t_all, d),) * 3, mod, mod, g.reshape(1, d),
      w_up.astype(BF16), conv_w, conv_b.reshape(1, 2 * d_ff), w_down.astype(BF16))
    return out.reshape(bsz, t_all, d)


def _rope_tables(seq, n_ctx):
    t = jnp.arange(seq)
    row = (t // GRID_W).astype(F32)
    col = (t % GRID_W).astype(F32)
    n_axis = HEAD_DIM // 4
    inv = ROPE_THETA ** (-jnp.arange(n_axis, dtype=F32) / n_axis)
    ang = jnp.concatenate([row[:, None] * inv, col[:, None] * inv], axis=-1)
    cos = jnp.concatenate([jnp.cos(ang), jnp.ones((n_ctx, HEAD_DIM // 2), F32)], axis=0)
    sin = jnp.concatenate([jnp.sin(ang), jnp.zeros((n_ctx, HEAD_DIM // 2), F32)], axis=0)
    nt = (seq + n_ctx) // TM
    to_tiles = lambda a: a.reshape(nt, TM, HEAD_DIM // 2).transpose(0, 2, 1)
    return to_tiles(cos), to_tiles(sin)


def kernel(x, c, ctx, c_ctx, w_mod, b_mod, g_attn, g_ffn, na_w_qkv, na_g_q, na_g_k, na_rpb, na_w_o, swa_w_qkv, swa_g_q, swa_g_k, swa_sink, swa_w_o, ga_w_qkv, ga_g_q, ga_g_k, ga_w_o, ffn_w_up, ffn_conv_w, ffn_conv_b, ffn_w_down):
    bsz, seq, d = x.shape
    n_ctx = ctx.shape[1]
    depth = w_mod.shape[0]
    assert d == D_MODEL and seq % TM == 0 and n_ctx == TM and seq == GRID_W * GRID_W
    n_lat = seq // TM
    nt = n_lat + 1
    group = N_HEADS // N_KV_HEADS

    cvec = jnp.zeros((8, d), F32).at[:bsz].set(c).at[bsz].set(c_ctx)
    mods = _modulation(cvec, w_mod, b_mod)
    mod_lat = mods[:, :bsz].reshape(depth, bsz, 1, 6, d)
    mod_ctx = jnp.broadcast_to(mods[:, bsz].reshape(depth, 1, 1, 6, d), (depth, bsz, 1, 6, d))
    mod_all = jnp.concatenate([mod_lat, mod_ctx], axis=2)

    cos_t, sin_t = _rope_tables(seq, n_ctx)
    stream = jnp.concatenate([x, ctx], axis=1)

    for i in range(depth):
        last = i == depth - 1
        kind, j = i % N_MIXERS, i // N_MIXERS
        nt_out = n_lat if last else nt
        mod = mod_all[i]
        if kind == 0:
            w_in, w_out, gq, gk = na_w_qkv[j], na_w_o[j], na_g_q[j], na_g_k[j]
        elif kind == 1:
            w_in, w_out, gq, gk = swa_w_qkv[j], swa_w_o[j], swa_g_q[j], swa_g_k[j]
        else:
            w_in, w_out, gq, gk = ga_w_qkv[j], ga_w_o[j], ga_g_q[j], ga_g_k[j]
        q_t, k_tok, v_t = _qkv(stream, mod, g_attn[i], w_in, gq, gk, cos_t, sin_t, rope=kind != 0)
        if kind == 0:
            stream = _local_attention(q_t, k_tok, v_t, _na_bias_tables(na_rpb[j], n_lat),
                                      stream, mod, w_out, kind=0, nt_out=nt_out)
        elif kind == 1:
            sink = jnp.broadcast_to((swa_sink[j] * LOG2E).reshape(N_KV_HEADS, 1, group, 1),
                                    (N_KV_HEADS, 1, group, TM)).reshape(N_KV_HEADS, 1, group * TM)
            stream = _local_attention(q_t, k_tok, v_t, sink, stream, mod, w_out, kind=1, nt_out=nt_out)
        else:
            stream = _global_attention(q_t, k_tok, v_t, stream, mod, w_out, nt_out=nt_out)
        stream = _conv_ffn(stream, mod, g_ffn[i], ffn_w_up[i], ffn_conv_w[i], ffn_conv_b[i],
                           ffn_w_down[i], n_lat=n_lat)
    return stream
```

```python
import functools

import jax
import jax.numpy as jnp
from jax import lax
from jax.experimental import pallas as pl
from jax.experimental.pallas import tpu as pltpu

D_MODEL = 1024
GRID_W = 64
HEAD_DIM = 64
N_HEADS = D_MODEL // HEAD_DIM
N_KV_HEADS = 4
N_MIXERS = 3
NA_WIN_H = 8
NA_WIN_W = 16
SWA_RADIUS = 128
ROPE_THETA = 10000.0
CONV_W = 3
RMS_EPS = 1e-6
NEG_INF = -1e30

TM = 256
HALO = 8
FF_CHUNK = 256
MOD_BLOCK = 1536
HEAD_PAIR = 2 * HEAD_DIM
QUAD = 4
ONES_ROWS = 16
LOG2E = 1.4426950408889634
VMEM_LIMIT = 56 * 1024 * 1024

BF16 = jnp.bfloat16
F32 = jnp.float32


def _cparams(n_axes):
    return pltpu.CompilerParams(dimension_semantics=("arbitrary",) * n_axes,
                                vmem_limit_bytes=VMEM_LIMIT)


def _resident(block_shape, index_map):
    return pl.BlockSpec(block_shape, index_map, pipeline_mode=pl.Buffered(1))


def _sigmoid(z):
    return 1.0 / (1.0 + jnp.exp(-z))


def _norm_mod(x, g, shift, scale):
    var = jnp.mean(x * x, axis=-1, keepdims=True)
    return (x * lax.rsqrt(var + RMS_EPS) * g) * (1.0 + scale) + shift


def _mod_kernel(c_ref, w_ref, b_ref, o_ref):
    c = c_ref[...]
    sc = (c * _sigmoid(c)).astype(BF16)
    o_ref[0] = jnp.dot(sc, w_ref[0].astype(BF16), preferred_element_type=F32) + b_ref[0]


def _modulation(cvec, w_mod, b_mod):
    depth, d, n = w_mod.shape
    bn = MOD_BLOCK
    return pl.pallas_call(
        _mod_kernel,
        grid=(depth, n // bn),
        in_specs=[pl.BlockSpec((8, d), lambda i, j: (0, 0)),
                  pl.BlockSpec((1, d, bn), lambda i, j: (i, 0, j)),
                  pl.BlockSpec((1, 1, bn), lambda i, j: (i, 0, j))],
        out_specs=pl.BlockSpec((1, 8, bn), lambda i, j: (i, 0, j)),
        out_shape=jax.ShapeDtypeStruct((depth, 8, n), F32),
        compiler_params=_cparams(2),
        name="modulation",
    )(cvec, w_mod, b_mod.reshape(depth, 1, n))


def _qkv_kernel(x_ref, mod0_ref, mod1_ref, g_ref, wt_ref, gq_ref, gk_ref,
                cos0_ref, sin0_ref, cos1_ref, sin1_ref, q_ref, k_ref, v_ref, *, kvw, rope):
    g = g_ref[...]
    x = x_ref[...]
    mods = (mod0_ref[0, 0], mod1_ref[0, 0])
    h = jnp.concatenate(
        [_norm_mod(x[s * TM:(s + 1) * TM], g, mods[s][0:1], mods[s][1:2]).astype(BF16) for s in range(2)],
        axis=0)
    def project(rows):
        return lax.dot_general(wt_ref[rows], h, (((1,), (1,)), ((), ())), preferred_element_type=F32)

    yq = project(slice(0, D_MODEL))
    yk = project(slice(D_MODEL, D_MODEL + kvw))
    yv = project(slice(D_MODEL + kvw, D_MODEL + 2 * kvw))

    def head_norm(y, gain, n_heads, cos_sin):
        y3 = y.reshape(n_heads, HEAD_DIM, TM)
        ms = jnp.mean(y3 * y3, axis=1, keepdims=True)
        y3 = y3 * lax.rsqrt(ms + RMS_EPS) * gain
        if rope:
            half = HEAD_DIM // 2
            x1, x2 = y3[:, :half], y3[:, half:]
            c, s = cos_sin
            y3 = jnp.concatenate([x1 * c - x2 * s, x2 * c + x1 * s], axis=1)
        return y3.reshape(n_heads * HEAD_DIM, TM)

    tables = ((cos0_ref[0], sin0_ref[0]), (cos1_ref[0], sin1_ref[0]))
    for s in range(2):
        q = head_norm(yq[:, s * TM:(s + 1) * TM], gq_ref[...], N_HEADS, tables[s])
        q_ref[s] = (q * (HEAD_DIM ** -0.5 * LOG2E)).astype(BF16)
    for s in range(2):
        k = head_norm(yk[:, s * TM:(s + 1) * TM], gk_ref[...], kvw // HEAD_DIM, tables[s])
        k_ref[s] = k.T.astype(BF16)
    for s in range(2):
        v_ref[s] = yv[:, s * TM:(s + 1) * TM].astype(BF16)


def _qkv(x, mod, g, w_qkv, g_q, g_k, cos_t, sin_t, *, rope):
    bsz, t_all, d = x.shape
    nt = t_all // TM
    n_tiles = bsz * nt
    assert n_tiles % 2 == 0
    n = w_qkv.shape[1]
    kvw = (n - d) // 2
    wt = w_qkv.T.astype(BF16)
    gq = jnp.broadcast_to(g_q[:, None], (HEAD_DIM, TM))
    gk = jnp.broadcast_to(g_k[:, None], (HEAD_DIM, TM))
    half = HEAD_DIM // 2

    def mod_map(s):
        return lambda j: ((2 * j + s) // nt, ((2 * j + s) % nt) // (nt - 1), 0, 0)

    def rope_map(s):
        return lambda j: ((2 * j + s) % nt, 0, 0)

    q_t, k_tok, v_t = pl.pallas_call(
        functools.partial(_qkv_kernel, kvw=kvw, rope=rope),
        grid=(n_tiles // 2,),
        in_specs=[pl.BlockSpec((2 * TM, d), lambda j: (j, 0)),
                  pl.BlockSpec((1, 1, 6, d), mod_map(0)),
                  pl.BlockSpec((1, 1, 6, d), mod_map(1)),
                  pl.BlockSpec((1, d), lambda j: (0, 0)),
                  _resident((n, d), lambda j: (0, 0)),
                  pl.BlockSpec((HEAD_DIM, TM), lambda j: (0, 0)),
                  pl.BlockSpec((HEAD_DIM, TM), lambda j: (0, 0)),
                  pl.BlockSpec((1, half, TM), rope_map(0)),
                  pl.BlockSpec((1, half, TM), rope_map(0)),
                  pl.BlockSpec((1, half, TM), rope_map(1)),
                  pl.BlockSpec((1, half, TM), rope_map(1))],
        out_specs=[pl.BlockSpec((2, d, TM), lambda j: (j, 0, 0)),
                   pl.BlockSpec((2, TM, kvw), lambda j: (j, 0, 0)),
                   pl.BlockSpec((2, kvw, TM), lambda j: (j, 0, 0))],
        out_shape=[jax.ShapeDtypeStruct((n_tiles, d, TM), BF16),
                   jax.ShapeDtypeStruct((n_tiles, TM, kvw), BF16),
                   jax.ShapeDtypeStruct((n_tiles, kvw, TM), BF16)],
        compiler_params=_cparams(1),
        name="qkv",
    )(x.reshape(bsz * t_all, d), mod, mod, g.reshape(1, d), wt, gq, gk, cos_t, sin_t, cos_t, sin_t)
    return (q_t.reshape(bsz, nt, d, TM), k_tok.reshape(bsz, nt, TM, kvw), v_t.reshape(bsz, nt, kvw, TM))


def _k_lanes(kv_head):
    g = kv_head // (HEAD_PAIR // HEAD_DIM)
    return slice(g * HEAD_PAIR, (g + 1) * HEAD_PAIR)


def _padded_queries(q_ref, first_head, n_q_heads, kv_head):
    qs = [q_ref[0, 0, (first_head + j) * HEAD_DIM:(first_head + j + 1) * HEAD_DIM, :]
          for j in range(n_q_heads)]
    qcat = qs[0] if n_q_heads == 1 else jnp.concatenate(qs, axis=1)
    slots = HEAD_PAIR // HEAD_DIM
    blocks = [jnp.zeros_like(qcat)] * slots
    blocks[kv_head % slots] = qcat
    return jnp.concatenate(blocks, axis=0)


def _store_heads(o_ref, o, first_head, n_q_heads):
    for j in range(n_q_heads):
        o_ref[0, 0, (first_head + j) * HEAD_DIM:(first_head + j + 1) * HEAD_DIM, :] = (
            o[:, j * TM:(j + 1) * TM].astype(o_ref.dtype))


def _project_out(o_scr, x_ref, mod_ref, wo_ref, y_ref):
    out_t = jnp.dot(wo_ref[...], o_scr[0, 0], preferred_element_type=F32)
    y_ref[0] = x_ref[0] + mod_ref[0, 0][2:3] * out_t.T


def _values_and_ones(v_rows):
    return jnp.concatenate([v_rows, jnp.ones((ONES_ROWS, v_rows.shape[1]), v_rows.dtype)], axis=0)


def _local_attn_kernel(*refs, kind, group, n_lat):
    q_ref = refs[0]
    k_refs = refs[1:5]
    v_refs = refs[5:9]
    extra_ref, x_ref, mod_ref, wo_ref, y_ref, o_ref = refs[9:15]
    t = pl.program_id(1)
    is_ctx = t >= n_lat
    valid = [jnp.logical_not(is_ctx),
             jnp.logical_and(t >= 1, jnp.logical_not(is_ctx)),
             t <= n_lat - 2,
             None]
    pen = [None if v is None else jnp.where(v, 0.0, NEG_INF).astype(F32) for v in valid]

    if kind == 1:
        rows = [(0, TM), (TM - SWA_RADIUS, SWA_RADIUS), (0, SWA_RADIUS), (0, TM)]
        band = []
        for (r0, nr), off in zip(rows[:3], (0, -1, 1)):
            key_i = lax.broadcasted_iota(jnp.int32, (nr, TM), 0) + (r0 + off * TM)
            qry_j = lax.broadcasted_iota(jnp.int32, (nr, TM), 1)
            band.append(jnp.where(jnp.abs(key_i - qry_j) <= SWA_RADIUS, 0.0, NEG_INF).astype(F32))
    else:
        rows = [(0, TM)] * 4
        tab = [jnp.where(t == 0, 0, jnp.where(t >= n_lat - 1, 6, 3)),
               jnp.where(t >= n_lat - 1, 5, 2),
               jnp.where(t == 0, 1, 4)]

    quad = QUAD
    width = quad * TM
    own_kv = group == 1

    def quad_queries(g):
        if not own_kv:
            return _padded_queries(q_ref, g * quad, quad, g)
        zeros = jnp.zeros((HEAD_DIM, TM), BF16)
        return jnp.concatenate(
            [jnp.concatenate([q_ref[0, 0, (g * quad + h) * HEAD_DIM:(g * quad + h + 1) * HEAD_DIM, :]
                              if j == h else zeros for j in range(quad)], axis=1)
             for h in range(quad)], axis=0)

    queries = {}

    def scores(g, c):
        if g not in queries:
            queries[g] = quad_queries(g)
        r0, nr = rows[c]
        lanes = slice(g * quad * HEAD_DIM, (g + 1) * quad * HEAD_DIM) if own_kv else _k_lanes(g)
        s = jnp.dot(k_refs[c][0, 0, r0:r0 + nr, lanes], queries[g], preferred_element_type=F32)
        if c < 3:
            if kind == 1:
                s = s + jnp.concatenate([band[c] + pen[c]] * quad, axis=1)
            else:
                s = jnp.concatenate([s[:, h * TM:(h + 1) * TM] + extra_ref[g * quad + h, tab[c]]
                                     for h in range(quad)], axis=1) + pen[c]
        return s

    def values(g, h, c):
        r0, nr = rows[c]
        n = g * quad + h if own_kv else g
        return _values_and_ones(v_refs[c][0, 0, n * HEAD_DIM:(n + 1) * HEAD_DIM, r0:r0 + nr])

    slots = (3, 0, 1, 2)
    order = [(g, c) for g in range(N_HEADS // quad) for c in slots]
    s_next = scores(*order[0])
    for i, (g, c) in enumerate(order):
        s = s_next
        if i + 1 < len(order):
            s_next = scores(*order[i + 1])
        if c == slots[0]:
            m = jnp.full((1, width), NEG_INF, F32)
            acc = jnp.zeros((HEAD_DIM + ONES_ROWS, width), F32)
        m_new = jnp.maximum(m, jnp.max(s, axis=0, keepdims=True))
        p = jnp.exp2(s - m_new).astype(BF16)
        if own_kv:
            pv = jnp.concatenate([jnp.dot(values(g, h, c), p[:, h * TM:(h + 1) * TM],
                                          preferred_element_type=F32) for h in range(quad)], axis=1)
        else:
            pv = jnp.dot(values(g, 0, c), p, preferred_element_type=F32)
        acc = jnp.exp2(m - m_new) * acc + pv
        m = m_new
        if c == slots[-1]:
            l = acc[HEAD_DIM:HEAD_DIM + 1]
            if kind == 1:
                sink = extra_ref[g]
                l = l + jnp.exp2(sink - m)
            _store_heads(o_ref, acc[:HEAD_DIM] / l, g * quad, quad)
    _project_out(o_ref, x_ref, mod_ref, wo_ref, y_ref)


def _local_attention(q_t, k_tok, v_t, extra, x, mod, w_o, *, kind, nt_out):
    bsz, nt, d, _ = q_t.shape
    n_lat = nt - 1
    kvw = k_tok.shape[-1]
    ctx_idx = nt - 1
    kern = functools.partial(_local_attn_kernel, kind=kind, group=N_HEADS * HEAD_DIM // kvw, n_lat=n_lat)
    chunk_maps = [lambda b, t: (b, t, 0, 0),
                  lambda b, t: (b, jnp.maximum(t - 1, 0), 0, 0),
                  lambda b, t: (b, jnp.minimum(t + 1, n_lat - 1), 0, 0),
                  lambda b, t: (b, ctx_idx, 0, 0)]
    return pl.pallas_call(
        kern,
        grid=(bsz, nt_out),
        in_specs=([pl.BlockSpec((1, 1, d, TM), lambda b, t: (b, t, 0, 0))]
                  + [pl.BlockSpec((1, 1, TM, kvw), f) for f in chunk_maps]
                  + [pl.BlockSpec((1, 1, kvw, TM), f) for f in chunk_maps]
                  + [_resident(extra.shape, lambda b, t: (0,) * extra.ndim),
                     pl.BlockSpec((1, TM, d), lambda b, t: (b, t, 0)),
                     pl.BlockSpec((1, 1, 6, d), lambda b, t: (b, t // n_lat, 0, 0)),
                     _resident((d, d), lambda b, t: (0, 0))]),
        out_specs=pl.BlockSpec((1, TM, d), lambda b, t: (b, t, 0)),
        out_shape=jax.ShapeDtypeStruct((bsz, nt_out * TM, d), F32),
        scratch_shapes=[pltpu.VMEM((1, 1, d, TM), BF16)],
        compiler_params=_cparams(2),
        name="na_attention" if kind == 0 else "swa_attention",
    )(q_t, k_tok, k_tok, k_tok, k_tok, v_t, v_t, v_t, v_t, extra, x, mod, w_o.T.astype(BF16))


def _na_bias_kernel(rpb_ref, o_ref, col_scr, *, combos, grid_rows):
    h = pl.program_id(0)
    n_dr, n_dc = 2 * NA_WIN_H - 1, 2 * NA_WIN_W - 1
    kc = lax.broadcasted_iota(jnp.int32, (GRID_W, 2 * GRID_W), 0)
    lane = lax.broadcasted_iota(jnp.int32, (GRID_W, 2 * GRID_W), 1)
    qc = lane & (GRID_W - 1)
    dc = kc - qc + (NA_WIN_W - 1)
    cs = jnp.clip(qc - NA_WIN_W // 2, 0, GRID_W - NA_WIN_W)
    col_ok = jnp.logical_and(kc >= cs, kc < cs + NA_WIN_W)

    def fill(a, carry):
        base = (h * n_dr + a) * n_dc
        g = jnp.full(kc.shape, rpb_ref[base], F32)
        for b in range(1, n_dc):
            g = jnp.where(dc == b, rpb_ref[base + b], g)
        col_scr[a] = jnp.where(col_ok, g * LOG2E, NEG_INF)
        return carry

    lax.fori_loop(0, n_dr, fill, 0)

    rows_per_tile = TM // GRID_W
    left = lane < GRID_W
    masked = jnp.full(kc.shape, NEG_INF, F32)
    for ci, (tile, off) in enumerate(combos):
        for kr in range(rows_per_tile):
            k_row = (tile + off) * rows_per_tile + kr
            for j in range(rows_per_tile // 2):
                halves = []
                for qr in (2 * j, 2 * j + 1):
                    q_row = tile * rows_per_tile + qr
                    rs = min(max(q_row - NA_WIN_H // 2, 0), grid_rows - NA_WIN_H)
                    ok = rs <= k_row < rs + NA_WIN_H
                    halves.append(col_scr[k_row - q_row + NA_WIN_H - 1] if ok else masked)
                o_ref[0, ci, kr * GRID_W:(kr + 1) * GRID_W, j * 2 * GRID_W:(j + 1) * 2 * GRID_W] = (
                    jnp.where(left, halves[0], halves[1]))


def _na_bias_tables(rpb, n_lat):
    n_heads = rpb.shape[0]
    combos = ((0, 0), (0, 1), (1, -1), (1, 0), (1, 1), (n_lat - 1, -1), (n_lat - 1, 0))
    return pl.pallas_call(
        functools.partial(_na_bias_kernel, combos=combos, grid_rows=n_lat * TM // GRID_W),
        grid=(n_heads,),
        in_specs=[pl.BlockSpec(memory_space=pltpu.SMEM)],
        out_specs=pl.BlockSpec((1, len(combos), TM, TM), lambda h: (h, 0, 0, 0)),
        out_shape=jax.ShapeDtypeStruct((n_heads, len(combos), TM, TM), F32),
        scratch_shapes=[pltpu.VMEM((2 * NA_WIN_H - 1, GRID_W, 2 * GRID_W), F32)],
        compiler_params=_cparams(1),
        name="na_bias",
    )(rpb.reshape(-1))


def _global_attn_kernel(q_ref, k_ref, v_ref, x_ref, mod_ref, wo_ref, y_ref, o_ref, s_scr, *, n_lat):
    t = pl.program_id(1)
    group = N_HEADS // N_KV_HEADS
    width = group * TM
    for n in range(N_KV_HEADS):
        qp = _padded_queries(q_ref, n * group, group, n)
        lanes = _k_lanes(n)

        def scores(c, qp=qp, lanes=lanes):
            return jnp.dot(k_ref[0, c, :, lanes], qp, preferred_element_type=F32)

        def absorb(s, c, m, acc, n=n):
            m_new = jnp.maximum(m, jnp.max(s, axis=0, keepdims=True))
            p = jnp.exp2(s - m_new).astype(BF16)
            pv = jnp.dot(_values_and_ones(v_ref[0, c, n * HEAD_DIM:(n + 1) * HEAD_DIM, :]), p,
                         preferred_element_type=F32)
            return m_new, jnp.exp2(m - m_new) * acc + pv

        def finish(acc, n=n):
            _store_heads(o_ref, acc[:HEAD_DIM] / acc[HEAD_DIM:HEAD_DIM + 1], n * group, group)

        m0 = jnp.full((1, width), NEG_INF, F32)
        acc0 = jnp.zeros((HEAD_DIM + ONES_ROWS, width), F32)

        @pl.when(t < n_lat)
        def _():
            s_scr[0] = scores(0)

            def two_tiles(i, carry):
                m, acc = carry
                s_scr[1] = scores(2 * i + 1)
                m, acc = absorb(s_scr[0], 2 * i, m, acc)
                s_scr[0] = scores(2 * i + 2)
                m, acc = absorb(s_scr[1], 2 * i + 1, m, acc)
                return m, acc

            m, acc = lax.fori_loop(0, n_lat // 2, two_tiles, (m0, acc0), unroll=True)
            m, acc = absorb(s_scr[0], n_lat, m, acc)
            finish(acc)

        @pl.when(t >= n_lat)
        def _():
            m, acc = absorb(scores(n_lat), n_lat, m0, acc0)
            finish(acc)

    _project_out(o_ref, x_ref, mod_ref, wo_ref, y_ref)


def _global_attention(q_t, k_tok, v_t, x, mod, w_o, *, nt_out):
    bsz, nt, d, _ = q_t.shape
    kvw = k_tok.shape[-1]
    n_lat = nt - 1
    assert n_lat % 2 == 0 and kvw == N_KV_HEADS * HEAD_DIM
    return pl.pallas_call(
        functools.partial(_global_attn_kernel, n_lat=n_lat),
        grid=(bsz, nt_out),
        in_specs=[pl.BlockSpec((1, 1, d, TM), lambda b, t: (b, t, 0, 0)),
                  pl.BlockSpec((1, nt, TM, kvw), lambda b, t: (b, 0, 0, 0)),
                  pl.BlockSpec((1, nt, kvw, TM), lambda b, t: (b, 0, 0, 0)),
                  pl.BlockSpec((1, TM, d), lambda b, t: (b, t, 0)),
                  pl.BlockSpec((1, 1, 6, d), lambda b, t: (b, t // n_lat, 0, 0)),
                  _resident((d, d), lambda b, t: (0, 0))],
        out_specs=pl.BlockSpec((1, TM, d), lambda b, t: (b, t, 0)),
        out_shape=jax.ShapeDtypeStruct((bsz, nt_out * TM, d), F32),
        scratch_shapes=[pltpu.VMEM((1, 1, d, TM), BF16),
                        pltpu.VMEM((2, TM, N_HEADS // N_KV_HEADS * TM), F32)],
        compiler_params=_cparams(2),
        name="global_attention",
    )(q_t, k_tok, v_t, x, mod, w_o.T.astype(BF16))


def _ffn_kernel(x_ref, xp_ref, xn_ref, mod0_ref, mod1_ref, g_ref, wup_ref, cw_ref, cb_ref, wdn_ref,
                y_ref, h_scr, a_scr, *, n_lat, tiles_per_batch):
    rows = 2 * TM
    j = pl.program_id(0)
    g = g_ref[...]
    mods = (mod0_ref[0, 0], mod1_ref[0, 0])
    x = x_ref[...]
    for s in range(2):
        h_scr[s * TM:(s + 1) * TM] = _norm_mod(
            x[s * TM:(s + 1) * TM], g, mods[s][3:4], mods[s][4:5]).astype(BF16)
    h_scr[rows:rows + 2 * HALO] = jnp.concatenate(
        [_norm_mod(xp_ref[...], g, mods[0][3:4], mods[0][4:5]),
         _norm_mod(xn_ref[...], g, mods[1][3:4], mods[1][4:5])], axis=0).astype(BF16)

    tile = [(2 * j + s) % tiles_per_batch for s in range(2)]
    prev_ok = [jnp.logical_and(t >= 1, t < n_lat) for t in tile]
    next_ok = [t <= n_lat - 2 for t in tile]
    row = lax.broadcasted_iota(jnp.int32, (rows, FF_CHUNK), 0)
    seam_cut_up = jnp.logical_and(row == TM, jnp.logical_not(prev_ok[1]))
    seam_cut_dn = jnp.logical_and(row == TM - 1, jnp.logical_not(next_ok[0]))
    h = h_scr[...]
    d_ff = wdn_ref.shape[0]

    def up_conv(cols):
        u = jnp.dot(h, wup_ref[:, cols], preferred_element_type=F32)
        um = u[0:rows]
        u_before = jnp.where(prev_ok[0], u[rows + HALO - 1:rows + HALO], 0.0)
        u_after = jnp.where(next_ok[1], u[rows + HALO:rows + HALO + 1], 0.0)
        up = jnp.where(row == 0, u_before, jnp.where(seam_cut_up, 0.0, pltpu.roll(um, 1, 0)))
        dn = jnp.where(row == rows - 1, u_after, jnp.where(seam_cut_dn, 0.0, pltpu.roll(um, rows - 1, 0)))
        cw = cw_ref[:, cols]
        return up * cw[0:1] + um * cw[1:2] + dn * cw[2:3] + cb_ref[:, cols]

    for c in range(d_ff // FF_CHUNK):
        a = up_conv(slice(c * FF_CHUNK, (c + 1) * FF_CHUNK))
        gate = up_conv(slice(d_ff + c * FF_CHUNK, d_ff + (c + 1) * FF_CHUNK))
        a_scr[:, c * FF_CHUNK:(c + 1) * FF_CHUNK] = (a * (gate * _sigmoid(gate))).astype(BF16)
    down = jnp.dot(a_scr[...], wdn_ref[...], preferred_element_type=F32)
    for s in range(2):
        y_ref[s * TM:(s + 1) * TM] = x[s * TM:(s + 1) * TM] + mods[s][5:6] * down[s * TM:(s + 1) * TM]


def _conv_ffn(x, mod, g, w_up, conv_w, conv_b, w_down, *, n_lat):
    bsz, t_all, d = x.shape
    tiles_per_batch = t_all // TM
    n_steps = bsz * tiles_per_batch // 2
    assert bsz * tiles_per_batch % 2 == 0
    d_ff = w_down.shape[0]
    rows = 2 * TM
    halos_per_step = rows // HALO
    last_halo = bsz * t_all // HALO - 1

    def mod_map(s):
        def index(j):
            tile = 2 * j + s
            return (tile // tiles_per_batch, (tile % tiles_per_batch) // n_lat, 0, 0)
        return index

    out = pl.pallas_call(
        functools.partial(_ffn_kernel, n_lat=n_lat, tiles_per_batch=tiles_per_batch),
        grid=(n_steps,),
        in_specs=[pl.BlockSpec((rows, d), lambda j: (j, 0)),
                  pl.BlockSpec((HALO, d), lambda j: (jnp.maximum(j * halos_per_step - 1, 0), 0)),
                  pl.BlockSpec((HALO, d), lambda j: (jnp.minimum((j + 1) * halos_per_step, last_halo), 0)),
                  pl.BlockSpec((1, 1, 6, d), mod_map(0)),
                  pl.BlockSpec((1, 1, 6, d), mod_map(1)),
                  pl.BlockSpec((1, d), lambda j: (0, 0)),
                  _resident((d, 2 * d_ff), lambda j: (0, 0)),
                  pl.BlockSpec((CONV_W, 2 * d_ff), lambda j: (0, 0)),
                  pl.BlockSpec((1, 2 * d_ff), lambda j: (0, 0)),
                  _resident((d_ff, d), lambda j: (0, 0))],
        out_specs=pl.BlockSpec((rows, d), lambda j: (j, 0)),
        out_shape=jax.ShapeDtypeStruct((bsz * t_all, d), F32),
        scratch_shapes=[pltpu.VMEM((rows + 2 * HALO, d), BF16), pltpu.VMEM((rows, d_ff), BF16)],
        compiler_params=_cparams(1),
        name="conv_ffn",
    )(*(x.reshape(bsz * t_all, d),) * 3, mod, mod, g.reshape(1, d),
      w_up.astype(BF16), conv_w, conv_b.reshape(1, 2 * d_ff), w_down.astype(BF16))
    return out.reshape(bsz, t_all, d)


def _rope_tables(seq, n_ctx):
    t = jnp.arange(seq)
    row = (t // GRID_W).astype(F32)
    col = (t % GRID_W).astype(F32)
    n_axis = HEAD_DIM // 4
    inv = ROPE_THETA ** (-jnp.arange(n_axis, dtype=F32) / n_axis)
    ang = jnp.concatenate([row[:, None] * inv, col[:, None] * inv], axis=-1)
    cos = jnp.concatenate([jnp.cos(ang), jnp.ones((n_ctx, HEAD_DIM // 2), F32)], axis=0)
    sin = jnp.concatenate([jnp.sin(ang), jnp.zeros((n_ctx, HEAD_DIM // 2), F32)], axis=0)
    nt = (seq + n_ctx) // TM
    to_tiles = lambda a: a.reshape(nt, TM, HEAD_DIM // 2).transpose(0, 2, 1)
    return to_tiles(cos), to_tiles(sin)


def kernel(x, c, ctx, c_ctx, w_mod, b_mod, g_attn, g_ffn, na_w_qkv, na_g_q, na_g_k, na_rpb, na_w_o, swa_w_qkv, swa_g_q, swa_g_k, swa_sink, swa_w_o, ga_w_qkv, ga_g_q, ga_g_k, ga_w_o, ffn_w_up, ffn_conv_w, ffn_conv_b, ffn_w_down):
    bsz, seq, d = x.shape
    n_ctx = ctx.shape[1]
    depth = w_mod.shape[0]
    assert d == D_MODEL and seq % TM == 0 and n_ctx == TM and seq == GRID_W * GRID_W
    n_lat = seq // TM
    nt = n_lat + 1
    group = N_HEADS // N_KV_HEADS

    cvec = jnp.zeros((8, d), F32).at[:bsz].set(c).at[bsz].set(c_ctx)
    mods = _modulation(cvec, w_mod, b_mod)
    mod_lat = mods[:, :bsz].reshape(depth, bsz, 1, 6, d)
    mod_ctx = jnp.broadcast_to(mods[:, bsz].reshape(depth, 1, 1, 6, d), (depth, bsz, 1, 6, d))
    mod_all = jnp.concatenate([mod_lat, mod_ctx], axis=2)

    cos_t, sin_t = _rope_tables(seq, n_ctx)
    stream = jnp.concatenate([x, ctx], axis=1)

    for i in range(depth):
        last = i == depth - 1
        kind, j = i % N_MIXERS, i // N_MIXERS
        nt_out = n_lat if last else nt
        mod = mod_all[i]
        if kind == 0:
            w_in, w_out, gq, gk = na_w_qkv[j], na_w_o[j], na_g_q[j], na_g_k[j]
        elif kind == 1:
            w_in, w_out, gq, gk = swa_w_qkv[j], swa_w_o[j], swa_g_q[j], swa_g_k[j]
        else:
            w_in, w_out, gq, gk = ga_w_qkv[j], ga_w_o[j], ga_g_q[j], ga_g_k[j]
        q_t, k_tok, v_t = _qkv(stream, mod, g_attn[i], w_in, gq, gk, cos_t, sin_t, rope=kind != 0)
        if kind == 0:
            stream = _local_attention(q_t, k_tok, v_t, _na_bias_tables(na_rpb[j], n_lat),
                                      stream, mod, w_out, kind=0, nt_out=nt_out)
        elif kind == 1:
            sink = jnp.broadcast_to((swa_sink[j] * LOG2E).reshape(N_KV_HEADS, 1, group, 1),
                                    (N_KV_HEADS, 1, group, TM)).reshape(N_KV_HEADS, 1, group * TM)
            stream = _local_attention(q_t, k_tok, v_t, sink, stream, mod, w_out, kind=1, nt_out=nt_out)
        else:
            stream = _global_attention(q_t, k_tok, v_t, stream, mod, w_out, nt_out=nt_out)
        stream = _conv_ffn(stream, mod, g_ffn[i], ffn_w_up[i], ffn_conv_w[i], ffn_conv_b[i],
                           ffn_w_down[i], n_lat=n_lat)
    return stream
```

```python
import functools

import jax
import jax.numpy as jnp
from jax import lax
from jax.experimental import pallas as pl
from jax.experimental.pallas import tpu as pltpu

D_MODEL = 1024
GRID_W = 64
HEAD_DIM = 64
N_HEADS = D_MODEL // HEAD_DIM
N_KV_HEADS = 4
N_MIXERS = 3
NA_WIN_H = 8
NA_WIN_W = 16
SWA_RADIUS = 128
ROPE_THETA = 10000.0
CONV_W = 3
RMS_EPS = 1e-6
NEG_INF = -1e30

TM = 256
HALO = 8
FF_CHUNK = 256
MOD_BLOCK = 1536
HEAD_PAIR = 2 * HEAD_DIM
QUAD = 4
TILES_AHEAD = 1
ONES_ROWS = 16
LOG2E = 1.4426950408889634
VMEM_LIMIT = 56 * 1024 * 1024

BF16 = jnp.bfloat16
F32 = jnp.float32


def _cparams(n_axes):
    return pltpu.CompilerParams(dimension_semantics=("arbitrary",) * n_axes,
                                vmem_limit_bytes=VMEM_LIMIT)


def _resident(block_shape, index_map):
    return pl.BlockSpec(block_shape, index_map, pipeline_mode=pl.Buffered(1))


def _sigmoid(z):
    return 1.0 / (1.0 + jnp.exp(-z))


def _norm_mod(x, g, shift, scale):
    var = jnp.mean(x * x, axis=-1, keepdims=True)
    return (x * lax.rsqrt(var + RMS_EPS) * g) * (1.0 + scale) + shift


def _mod_kernel(c_ref, w_ref, b_ref, o_ref):
    c = c_ref[...]
    sc = (c * _sigmoid(c)).astype(BF16)
    o_ref[0] = jnp.dot(sc, w_ref[0].astype(BF16), preferred_element_type=F32) + b_ref[0]


def _modulation(cvec, w_mod, b_mod):
    depth, d, n = w_mod.shape
    bn = MOD_BLOCK
    return pl.pallas_call(
        _mod_kernel,
        grid=(depth, n // bn),
        in_specs=[pl.BlockSpec((8, d), lambda i, j: (0, 0)),
                  pl.BlockSpec((1, d, bn), lambda i, j: (i, 0, j)),
                  pl.BlockSpec((1, 1, bn), lambda i, j: (i, 0, j))],
        out_specs=pl.BlockSpec((1, 8, bn), lambda i, j: (i, 0, j)),
        out_shape=jax.ShapeDtypeStruct((depth, 8, n), F32),
        compiler_params=_cparams(2),
        name="modulation",
    )(cvec, w_mod, b_mod.reshape(depth, 1, n))


def _qkv_kernel(x_ref, mod0_ref, mod1_ref, g_ref, wt_ref, gq_ref, gk_ref,
                cos0_ref, sin0_ref, cos1_ref, sin1_ref, q_ref, k_ref, v_ref, *, kvw, rope):
    g = g_ref[...]
    x = x_ref[...]
    mods = (mod0_ref[0, 0], mod1_ref[0, 0])
    h = jnp.concatenate(
        [_norm_mod(x[s * TM:(s + 1) * TM], g, mods[s][0:1], mods[s][1:2]).astype(BF16) for s in range(2)],
        axis=0)
    def project(rows):
        return lax.dot_general(wt_ref[rows], h, (((1,), (1,)), ((), ())), preferred_element_type=F32)

    yq = project(slice(0, D_MODEL))
    yk = project(slice(D_MODEL, D_MODEL + kvw))
    yv = project(slice(D_MODEL + kvw, D_MODEL + 2 * kvw))

    def head_norm(y, gain, n_heads, cos_sin):
        y3 = y.reshape(n_heads, HEAD_DIM, TM)
        ms = jnp.mean(y3 * y3, axis=1, keepdims=True)
        y3 = y3 * lax.rsqrt(ms + RMS_EPS) * gain
        if rope:
            half = HEAD_DIM // 2
            x1, x2 = y3[:, :half], y3[:, half:]
            c, s = cos_sin
            y3 = jnp.concatenate([x1 * c - x2 * s, x2 * c + x1 * s], axis=1)
        return y3.reshape(n_heads * HEAD_DIM, TM)

    tables = ((cos0_ref[0], sin0_ref[0]), (cos1_ref[0], sin1_ref[0]))
    for s in range(2):
        q = head_norm(yq[:, s * TM:(s + 1) * TM], gq_ref[...], N_HEADS, tables[s])
        q_ref[s] = (q * (HEAD_DIM ** -0.5 * LOG2E)).astype(BF16)
    for s in range(2):
        k = head_norm(yk[:, s * TM:(s + 1) * TM], gk_ref[...], kvw // HEAD_DIM, tables[s])
        k_ref[s] = k.T.astype(BF16)
    for s in range(2):
        v_ref[s] = yv[:, s * TM:(s + 1) * TM].astype(BF16)


def _qkv(x, mod, g, w_qkv, g_q, g_k, cos_t, sin_t, *, rope):
    bsz, t_all, d = x.shape
    nt = t_all // TM
    n_tiles = bsz * nt
    assert n_tiles % 2 == 0
    n = w_qkv.shape[1]
    kvw = (n - d) // 2
    wt = w_qkv.T.astype(BF16)
    gq = jnp.broadcast_to(g_q[:, None], (HEAD_DIM, TM))
    gk = jnp.broadcast_to(g_k[:, None], (HEAD_DIM, TM))
    half = HEAD_DIM // 2

    def mod_map(s):
        return lambda j: ((2 * j + s) // nt, ((2 * j + s) % nt) // (nt - 1), 0, 0)

    def rope_map(s):
        return lambda j: ((2 * j + s) % nt, 0, 0)

    q_t, k_tok, v_t = pl.pallas_call(
        functools.partial(_qkv_kernel, kvw=kvw, rope=rope),
        grid=(n_tiles // 2,),
        in_specs=[pl.BlockSpec((2 * TM, d), lambda j: (j, 0)),
                  pl.BlockSpec((1, 1, 6, d), mod_map(0)),
                  pl.BlockSpec((1, 1, 6, d), mod_map(1)),
                  pl.BlockSpec((1, d), lambda j: (0, 0)),
                  _resident((n, d), lambda j: (0, 0)),
                  pl.BlockSpec((HEAD_DIM, TM), lambda j: (0, 0)),
                  pl.BlockSpec((HEAD_DIM, TM), lambda j: (0, 0)),
                  pl.BlockSpec((1, half, TM), rope_map(0)),
                  pl.BlockSpec((1, half, TM), rope_map(0)),
                  pl.BlockSpec((1, half, TM), rope_map(1)),
                  pl.BlockSpec((1, half, TM), rope_map(1))],
        out_specs=[pl.BlockSpec((2, d, TM), lambda j: (j, 0, 0)),
                   pl.BlockSpec((2, TM, kvw), lambda j: (j, 0, 0)),
                   pl.BlockSpec((2, kvw, TM), lambda j: (j, 0, 0))],
        out_shape=[jax.ShapeDtypeStruct((n_tiles, d, TM), BF16),
                   jax.ShapeDtypeStruct((n_tiles, TM, kvw), BF16),
                   jax.ShapeDtypeStruct((n_tiles, kvw, TM), BF16)],
        compiler_params=_cparams(1),
        name="qkv",
    )(x.reshape(bsz * t_all, d), mod, mod, g.reshape(1, d), wt, gq, gk, cos_t, sin_t, cos_t, sin_t)
    return (q_t.reshape(bsz, nt, d, TM), k_tok.reshape(bsz, nt, TM, kvw), v_t.reshape(bsz, nt, kvw, TM))


def _k_lanes(kv_head):
    g = kv_head // (HEAD_PAIR // HEAD_DIM)
    return slice(g * HEAD_PAIR, (g + 1) * HEAD_PAIR)


def _padded_queries(q_ref, first_head, n_q_heads, kv_head):
    qs = [q_ref[0, 0, (first_head + j) * HEAD_DIM:(first_head + j + 1) * HEAD_DIM, :]
          for j in range(n_q_heads)]
    qcat = qs[0] if n_q_heads == 1 else jnp.concatenate(qs, axis=1)
    slots = HEAD_PAIR // HEAD_DIM
    blocks = [jnp.zeros_like(qcat)] * slots
    blocks[kv_head % slots] = qcat
    return jnp.concatenate(blocks, axis=0)


def _store_heads(o_ref, o, first_head, n_q_heads):
    for j in range(n_q_heads):
        o_ref[0, 0, (first_head + j) * HEAD_DIM:(first_head + j + 1) * HEAD_DIM, :] = (
            o[:, j * TM:(j + 1) * TM].astype(o_ref.dtype))


def _project_out(o_scr, x_ref, mod_ref, wo_ref, y_ref):
    out_t = jnp.dot(wo_ref[...], o_scr[0, 0], preferred_element_type=F32)
    y_ref[0] = x_ref[0] + mod_ref[0, 0][2:3] * out_t.T


def _values_and_ones(v_rows):
    return jnp.concatenate([v_rows, jnp.ones((ONES_ROWS, v_rows.shape[1]), v_rows.dtype)], axis=0)


def _local_attn_kernel(*refs, kind, group, n_lat):
    q_ref = refs[0]
    k_refs = refs[1:5]
    v_refs = refs[5:9]
    extra_ref, x_ref, mod_ref, wo_ref, y_ref, o_ref = refs[9:15]
    t = pl.program_id(1)
    is_ctx = t >= n_lat
    valid = [jnp.logical_not(is_ctx),
             jnp.logical_and(t >= 1, jnp.logical_not(is_ctx)),
             t <= n_lat - 2,
             None]
    pen = [None if v is None else jnp.where(v, 0.0, NEG_INF).astype(F32) for v in valid]

    if kind == 1:
        rows = [(0, TM), (TM - SWA_RADIUS, SWA_RADIUS), (0, SWA_RADIUS), (0, TM)]
        band = []
        for (r0, nr), off in zip(rows[:3], (0, -1, 1)):
            key_i = lax.broadcasted_iota(jnp.int32, (nr, TM), 0) + (r0 + off * TM)
            qry_j = lax.broadcasted_iota(jnp.int32, (nr, TM), 1)
            band.append(jnp.where(jnp.abs(key_i - qry_j) <= SWA_RADIUS, 0.0, NEG_INF).astype(F32))
    else:
        rows = [(0, TM)] * 4
        tab = [jnp.where(t == 0, 0, jnp.where(t >= n_lat - 1, 6, 3)),
               jnp.where(t >= n_lat - 1, 5, 2),
               jnp.where(t == 0, 1, 4)]

    quad = QUAD
    width = quad * TM
    own_kv = group == 1

    def quad_queries(g):
        if not own_kv:
            return _padded_queries(q_ref, g * quad, quad, g)
        zeros = jnp.zeros((HEAD_DIM, TM), BF16)
        return jnp.concatenate(
            [jnp.concatenate([q_ref[0, 0, (g * quad + h) * HEAD_DIM:(g * quad + h + 1) * HEAD_DIM, :]
                              if j == h else zeros for j in range(quad)], axis=1)
             for h in range(quad)], axis=0)

    queries = {}

    def scores(g, c):
        if g not in queries:
            queries[g] = quad_queries(g)
        r0, nr = rows[c]
        lanes = slice(g * quad * HEAD_DIM, (g + 1) * quad * HEAD_DIM) if own_kv else _k_lanes(g)
        s = jnp.dot(k_refs[c][0, 0, r0:r0 + nr, lanes], queries[g], preferred_element_type=F32)
        if c < 3:
            if kind == 1:
                s = s + jnp.concatenate([band[c] + pen[c]] * quad, axis=1)
            else:
                s = jnp.concatenate([s[:, h * TM:(h + 1) * TM] + extra_ref[g * quad + h, tab[c]]
                                     for h in range(quad)], axis=1) + pen[c]
        return s

    def values(g, h, c):
        r0, nr = rows[c]
        n = g * quad + h if own_kv else g
        return _values_and_ones(v_refs[c][0, 0, n * HEAD_DIM:(n + 1) * HEAD_DIM, r0:r0 + nr])

    slots = (3, 0, 1, 2)
    order = [(g, c) for g in range(N_HEADS // quad) for c in slots]
    s_next = scores(*order[0])
    for i, (g, c) in enumerate(order):
        s = s_next
        if i + 1 < len(order):
            s_next = scores(*order[i + 1])
        if c == slots[0]:
            m = jnp.full((1, width), NEG_INF, F32)
            acc = jnp.zeros((HEAD_DIM + ONES_ROWS, width), F32)
        m_new = jnp.maximum(m, jnp.max(s, axis=0, keepdims=True))
        p = jnp.exp2(s - m_new).astype(BF16)
        if own_kv:
            pv = jnp.concatenate([jnp.dot(values(g, h, c), p[:, h * TM:(h + 1) * TM],
                                          preferred_element_type=F32) for h in range(quad)], axis=1)
        else:
            pv = jnp.dot(values(g, 0, c), p, preferred_element_type=F32)
        acc = jnp.exp2(m - m_new) * acc + pv
        m = m_new
        if c == slots[-1]:
            l = acc[HEAD_DIM:HEAD_DIM + 1]
            if kind == 1:
                sink = extra_ref[g]
                l = l + jnp.exp2(sink - m)
            _store_heads(o_ref, acc[:HEAD_DIM] / l, g * quad, quad)
    _project_out(o_ref, x_ref, mod_ref, wo_ref, y_ref)


def _local_attention(q_t, k_tok, v_t, extra, x, mod, w_o, *, kind, nt_out):
    bsz, nt, d, _ = q_t.shape
    n_lat = nt - 1
    kvw = k_tok.shape[-1]
    ctx_idx = nt - 1
    kern = functools.partial(_local_attn_kernel, kind=kind, group=N_HEADS * HEAD_DIM // kvw, n_lat=n_lat)
    chunk_maps = [lambda b, t: (b, t, 0, 0),
                  lambda b, t: (b, jnp.maximum(t - 1, 0), 0, 0),
                  lambda b, t: (b, jnp.minimum(t + 1, n_lat - 1), 0, 0),
                  lambda b, t: (b, ctx_idx, 0, 0)]
    return pl.pallas_call(
        kern,
        grid=(bsz, nt_out),
        in_specs=([pl.BlockSpec((1, 1, d, TM), lambda b, t: (b, t, 0, 0))]
                  + [pl.BlockSpec((1, 1, TM, kvw), f) for f in chunk_maps]
                  + [pl.BlockSpec((1, 1, kvw, TM), f) for f in chunk_maps]
                  + [_resident(extra.shape, lambda b, t: (0,) * extra.ndim),
                     pl.BlockSpec((1, TM, d), lambda b, t: (b, t, 0)),
                     pl.BlockSpec((1, 1, 6, d), lambda b, t: (b, t // n_lat, 0, 0)),
                     _resident((d, d), lambda b, t: (0, 0))]),
        out_specs=pl.BlockSpec((1, TM, d), lambda b, t: (b, t, 0)),
        out_shape=jax.ShapeDtypeStruct((bsz, nt_out * TM, d), F32),
        scratch_shapes=[pltpu.VMEM((1, 1, d, TM), BF16)],
        compiler_params=_cparams(2),
        name="na_attention" if kind == 0 else "swa_attention",
    )(q_t, k_tok, k_tok, k_tok, k_tok, v_t, v_t, v_t, v_t, extra, x, mod, w_o.T.astype(BF16))


def _na_bias_kernel(rpb_ref, o_ref, col_scr, *, combos, grid_rows):
    h = pl.program_id(0)
    n_dr, n_dc = 2 * NA_WIN_H - 1, 2 * NA_WIN_W - 1
    kc = lax.broadcasted_iota(jnp.int32, (GRID_W, 2 * GRID_W), 0)
    lane = lax.broadcasted_iota(jnp.int32, (GRID_W, 2 * GRID_W), 1)
    qc = lane & (GRID_W - 1)
    dc = kc - qc + (NA_WIN_W - 1)
    cs = jnp.clip(qc - NA_WIN_W // 2, 0, GRID_W - NA_WIN_W)
    col_ok = jnp.logical_and(kc >= cs, kc < cs + NA_WIN_W)

    def fill(a, carry):
        base = (h * n_dr + a) * n_dc
        g = jnp.full(kc.shape, rpb_ref[base], F32)
        for b in range(1, n_dc):
            g = jnp.where(dc == b, rpb_ref[base + b], g)
        col_scr[a] = jnp.where(col_ok, g * LOG2E, NEG_INF)
        return carry

    lax.fori_loop(0, n_dr, fill, 0)

    rows_per_tile = TM // GRID_W
    left = lane < GRID_W
    masked = jnp.full(kc.shape, NEG_INF, F32)
    for ci, (tile, off) in enumerate(combos):
        for kr in range(rows_per_tile):
            k_row = (tile + off) * rows_per_tile + kr
            for j in range(rows_per_tile // 2):
                halves = []
                for qr in (2 * j, 2 * j + 1):
                    q_row = tile * rows_per_tile + qr
                    rs = min(max(q_row - NA_WIN_H // 2, 0), grid_rows - NA_WIN_H)
                    ok = rs <= k_row < rs + NA_WIN_H
                    halves.append(col_scr[k_row - q_row + NA_WIN_H - 1] if ok else masked)
                o_ref[0, ci, kr * GRID_W:(kr + 1) * GRID_W, j * 2 * GRID_W:(j + 1) * 2 * GRID_W] = (
                    jnp.where(left, halves[0], halves[1]))


def _na_bias_tables(rpb, n_lat):
    n_heads = rpb.shape[0]
    combos = ((0, 0), (0, 1), (1, -1), (1, 0), (1, 1), (n_lat - 1, -1), (n_lat - 1, 0))
    return pl.pallas_call(
        functools.partial(_na_bias_kernel, combos=combos, grid_rows=n_lat * TM // GRID_W),
        grid=(n_heads,),
        in_specs=[pl.BlockSpec(memory_space=pltpu.SMEM)],
        out_specs=pl.BlockSpec((1, len(combos), TM, TM), lambda h: (h, 0, 0, 0)),
        out_shape=jax.ShapeDtypeStruct((n_heads, len(combos), TM, TM), F32),
        scratch_shapes=[pltpu.VMEM((2 * NA_WIN_H - 1, GRID_W, 2 * GRID_W), F32)],
        compiler_params=_cparams(1),
        name="na_bias",
    )(rpb.reshape(-1))


def _global_attn_kernel(q_ref, k_ref, v_ref, x_ref, mod_ref, wo_ref, y_ref, o_ref, s_scr, *, n_lat):
    t = pl.program_id(1)
    group = N_HEADS // N_KV_HEADS
    width = group * TM
    n_buf = TILES_AHEAD + 1
    queries = [_padded_queries(q_ref, n * group, group, n) for n in range(N_KV_HEADS)]

    def attend(first_tile):
        units = [(n, c) for n in range(N_KV_HEADS) for c in range(first_tile, n_lat + 1)]

        def scores(i):
            n, c = units[i]
            s = jnp.dot(k_ref[0, c, :, _k_lanes(n)], queries[n], preferred_element_type=F32)
            s_scr[i % n_buf] = s
            return jnp.max(s, axis=0, keepdims=True)

        tile_max = {i: scores(i) for i in range(min(TILES_AHEAD, len(units)))}
        for i, (n, c) in enumerate(units):
            if i + TILES_AHEAD < len(units):
                tile_max[i + TILES_AHEAD] = scores(i + TILES_AHEAD)
            if c == first_tile:
                m = jnp.full((1, width), NEG_INF, F32)
                acc = jnp.zeros((HEAD_DIM + ONES_ROWS, width), F32)
            m_new = jnp.maximum(m, tile_max.pop(i))
            p = jnp.exp2(s_scr[i % n_buf] - m_new).astype(BF16)
            pv = jnp.dot(_values_and_ones(v_ref[0, c, n * HEAD_DIM:(n + 1) * HEAD_DIM, :]), p,
                         preferred_element_type=F32)
            acc = jnp.exp2(m - m_new) * acc + pv
            m = m_new
            if c == n_lat:
                _store_heads(o_ref, acc[:HEAD_DIM] / acc[HEAD_DIM:HEAD_DIM + 1], n * group, group)

    pl.when(t < n_lat)(lambda: attend(0))
    pl.when(t >= n_lat)(lambda: attend(n_lat))
    _project_out(o_ref, x_ref, mod_ref, wo_ref, y_ref)


def _global_attention(q_t, k_tok, v_t, x, mod, w_o, *, nt_out):
    bsz, nt, d, _ = q_t.shape
    kvw = k_tok.shape[-1]
    n_lat = nt - 1
    assert n_lat % 2 == 0 and kvw == N_KV_HEADS * HEAD_DIM
    return pl.pallas_call(
        functools.partial(_global_attn_kernel, n_lat=n_lat),
        grid=(bsz, nt_out),
        in_specs=[pl.BlockSpec((1, 1, d, TM), lambda b, t: (b, t, 0, 0)),
                  pl.BlockSpec((1, nt, TM, kvw), lambda b, t: (b, 0, 0, 0)),
                  pl.BlockSpec((1, nt, kvw, TM), lambda b, t: (b, 0, 0, 0)),
                  pl.BlockSpec((1, TM, d), lambda b, t: (b, t, 0)),
                  pl.BlockSpec((1, 1, 6, d), lambda b, t: (b, t // n_lat, 0, 0)),
                  _resident((d, d), lambda b, t: (0, 0))],
        out_specs=pl.BlockSpec((1, TM, d), lambda b, t: (b, t, 0)),
        out_shape=jax.ShapeDtypeStruct((bsz, nt_out * TM, d), F32),
        scratch_shapes=[pltpu.VMEM((1, 1, d, TM), BF16),
                        pltpu.VMEM((TILES_AHEAD + 1, TM, N_HEADS // N_KV_HEADS * TM), F32)],
        compiler_params=_cparams(2),
        name="global_attention",
    )(q_t, k_tok, v_t, x, mod, w_o.T.astype(BF16))


def _ffn_kernel(x_ref, xp_ref, xn_ref, mod0_ref, mod1_ref, g_ref, wup_ref, cw_ref, cb_ref, wdn_ref,
                y_ref, h_scr, a_scr, *, n_lat, tiles_per_batch):
    rows = 2 * TM
    j = pl.program_id(0)
    g = g_ref[...]
    mods = (mod0_ref[0, 0], mod1_ref[0, 0])
    x = x_ref[...]
    for s in range(2):
        h_scr[s * TM:(s + 1) * TM] = _norm_mod(
            x[s * TM:(s + 1) * TM], g, mods[s][3:4], mods[s][4:5]).astype(BF16)
    h_scr[rows:rows + 2 * HALO] = jnp.concatenate(
        [_norm_mod(xp_ref[...], g, mods[0][3:4], mods[0][4:5]),
         _norm_mod(xn_ref[...], g, mods[1][3:4], mods[1][4:5])], axis=0).astype(BF16)

    tile = [(2 * j + s) % tiles_per_batch for s in range(2)]
    prev_ok = [jnp.logical_and(t >= 1, t < n_lat) for t in tile]
    next_ok = [t <= n_lat - 2 for t in tile]
    row = lax.broadcasted_iota(jnp.int32, (rows, FF_CHUNK), 0)
    seam_cut_up = jnp.logical_and(row == TM, jnp.logical_not(prev_ok[1]))
    seam_cut_dn = jnp.logical_and(row == TM - 1, jnp.logical_not(next_ok[0]))
    h = h_scr[...]
    d_ff = wdn_ref.shape[0]

    def up_conv(cols):
        u = jnp.dot(h, wup_ref[:, cols], preferred_element_type=F32)
        um = u[0:rows]
        u_before = jnp.where(prev_ok[0], u[rows + HALO - 1:rows + HALO], 0.0)
        u_after = jnp.where(next_ok[1], u[rows + HALO:rows + HALO + 1], 0.0)
        up = jnp.where(row == 0, u_before, jnp.where(seam_cut_up, 0.0, pltpu.roll(um, 1, 0)))
        dn = jnp.where(row == rows - 1, u_after, jnp.where(seam_cut_dn, 0.0, pltpu.roll(um, rows - 1, 0)))
        cw = cw_ref[:, cols]
        return up * cw[0:1] + um * cw[1:2] + dn * cw[2:3] + cb_ref[:, cols]

    for c in range(d_ff // FF_CHUNK):
        a = up_conv(slice(c * FF_CHUNK, (c + 1) * FF_CHUNK))
        gate = up_conv(slice(d_ff + c * FF_CHUNK, d_ff + (c + 1) * FF_CHUNK))
        a_scr[:, c * FF_CHUNK:(c + 1) * FF_CHUNK] = (a * (gate * _sigmoid(gate))).astype(BF16)
    down = jnp.dot(a_scr[...], wdn_ref[...], preferred_element_type=F32)
    for s in range(2):
        y_ref[s * TM:(s + 1) * TM] = x[s * TM:(s + 1) * TM] + mods[s][5:6] * down[s * TM:(s + 1) * TM]


def _conv_ffn(x, mod, g, w_up, conv_w, conv_b, w_down, *, n_lat):
    bsz, t_all, d = x.shape
    tiles_per_batch = t_all // TM
    n_steps = bsz * tiles_per_batch // 2
    assert bsz * tiles_per_batch % 2 == 0
    d_ff = w_down.shape[0]
    rows = 2 * TM
    halos_per_step = rows // HALO
    last_halo = bsz * t_all // HALO - 1

    def mod_map(s):
        def index(j):
            tile = 2 * j + s
            return (tile // tiles_per_batch, (tile % tiles_per_batch) // n_lat, 0, 0)
        return index

    out = pl.pallas_call(
        functools.partial(_ffn_kernel, n_lat=n_lat, tiles_per_batch=tiles_per_batch),
        grid=(n_steps,),
        in_specs=[pl.BlockSpec((rows, d), lambda j: (j, 0)),
                  pl.BlockSpec((HALO, d), lambda j: (jnp.maximum(j * halos_per_step - 1, 0), 0)),
                  pl.BlockSpec((HALO, d), lambda j: (jnp.minimum((j + 1) * halos_per_step, last_halo), 0)),
                  pl.BlockSpec((1, 1, 6, d), mod_map(0)),
                  pl.BlockSpec((1, 1, 6, d), mod_map(1)),
                  pl.BlockSpec((1, d), lambda j: (0, 0)),
                  _resident((d, 2 * d_ff), lambda j: (0, 0)),
                  pl.BlockSpec((CONV_W, 2 * d_ff), lambda j: (0, 0)),
                  pl.BlockSpec((1, 2 * d_ff), lambda j: (0, 0)),
                  _resident((d_ff, d), lambda j: (0, 0))],
        out_specs=pl.BlockSpec((rows, d), lambda j: (j, 0)),
        out_shape=jax.ShapeDtypeStruct((bsz * t_all, d), F32),
        scratch_shapes=[pltpu.VMEM((rows + 2 * HALO, d), BF16), pltpu.VMEM((rows, d_ff), BF16)],
        compiler_params=_cparams(1),
        name="conv_ffn",
    )(*(x.reshape(bsz * t_all, d),) * 3, mod, mod, g.reshape(1, d),
      w_up.astype(BF16), conv_w, conv_b.reshape(1, 2 * d_ff), w_down.astype(BF16))
    return out.reshape(bsz, t_all, d)


def _rope_tables(seq, n_ctx):
    t = jnp.arange(seq)
    row = (t // GRID_W).astype(F32)
    col = (t % GRID_W).astype(F32)
    n_axis = HEAD_DIM // 4
    inv = ROPE_THETA ** (-jnp.arange(n_axis, dtype=F32) / n_axis)
    ang = jnp.concatenate([row[:, None] * inv, col[:, None] * inv], axis=-1)
    cos = jnp.concatenate([jnp.cos(ang), jnp.ones((n_ctx, HEAD_DIM // 2), F32)], axis=0)
    sin = jnp.concatenate([jnp.sin(ang), jnp.zeros((n_ctx, HEAD_DIM // 2), F32)], axis=0)
    nt = (seq + n_ctx) // TM
    to_tiles = lambda a: a.reshape(nt, TM, HEAD_DIM // 2).transpose(0, 2, 1)
    return to_tiles(cos), to_tiles(sin)


def kernel(x, c, ctx, c_ctx, w_mod, b_mod, g_attn, g_ffn, na_w_qkv, na_g_q, na_g_k, na_rpb, na_w_o, swa_w_qkv, swa_g_q, swa_g_k, swa_sink, swa_w_o, ga_w_qkv, ga_g_q, ga_g_k, ga_w_o, ffn_w_up, ffn_conv_w, ffn_conv_b, ffn_w_down):
    bsz, seq, d = x.shape
    n_ctx = ctx.shape[1]
    depth = w_mod.shape[0]
    assert d == D_MODEL and seq % TM == 0 and n_ctx == TM and seq == GRID_W * GRID_W
    n_lat = seq // TM
    nt = n_lat + 1
    group = N_HEADS // N_KV_HEADS

    cvec = jnp.zeros((8, d), F32).at[:bsz].set(c).at[bsz].set(c_ctx)
    mods = _modulation(cvec, w_mod, b_mod)
    mod_lat = mods[:, :bsz].reshape(depth, bsz, 1, 6, d)
    mod_ctx = jnp.broadcast_to(mods[:, bsz].reshape(depth, 1, 1, 6, d), (depth, bsz, 1, 6, d))
    mod_all = jnp.concatenate([mod_lat, mod_ctx], axis=2)

    cos_t, sin_t = _rope_tables(seq, n_ctx)
    stream = jnp.concatenate([x, ctx], axis=1)

    for i in range(depth):
        last = i == depth - 1
        kind, j = i % N_MIXERS, i // N_MIXERS
        nt_out = n_lat if last else nt
        mod = mod_all[i]
        if kind == 0:
            w_in, w_out, gq, gk = na_w_qkv[j], na_w_o[j], na_g_q[j], na_g_k[j]
        elif kind == 1:
            w_in, w_out, gq, gk = swa_w_qkv[j], swa_w_o[j], swa_g_q[j], swa_g_k[j]
        else:
            w_in, w_out, gq, gk = ga_w_qkv[j], ga_w_o[j], ga_g_q[j], ga_g_k[j]
        q_t, k_tok, v_t = _qkv(stream, mod, g_attn[i], w_in, gq, gk, cos_t, sin_t, rope=kind != 0)
        if kind == 0:
            stream = _local_attention(q_t, k_tok, v_t, _na_bias_tables(na_rpb[j], n_lat),
                                      stream, mod, w_out, kind=0, nt_out=nt_out)
        elif kind == 1:
            sink = jnp.broadcast_to((swa_sink[j] * LOG2E).reshape(N_KV_HEADS, 1, group, 1),
                                    (N_KV_HEADS, 1, group, TM)).reshape(N_KV_HEADS, 1, group * TM)
            stream = _local_attention(q_t, k_tok, v_t, sink, stream, mod, w_out, kind=1, nt_out=nt_out)
        else:
            stream = _global_attention(q_t, k_tok, v_t, stream, mod, w_out, nt_out=nt_out)
        stream = _conv_ffn(stream, mod, g_ffn[i], ffn_w_up[i], ffn_conv_w[i], ffn_conv_b[i],
                           ffn_w_down[i], n_lat=n_lat)
    return stream
```

```python
import functools

import jax
import jax.numpy as jnp
from jax import lax
from jax.experimental import pallas as pl
from jax.experimental.pallas import tpu as pltpu

D_MODEL = 1024
GRID_W = 64
HEAD_DIM = 64
N_HEADS = D_MODEL // HEAD_DIM
N_KV_HEADS = 4
N_MIXERS = 3
NA_WIN_H = 8
NA_WIN_W = 16
SWA_RADIUS = 128
ROPE_THETA = 10000.0
CONV_W = 3
RMS_EPS = 1e-6
NEG_INF = -1e30

TM = 256
HALO = 8
FF_CHUNK = 256
FFN_TILES = 2
W_CHUNK = 256
MOD_BLOCK = 1536
HEAD_PAIR = 2 * HEAD_DIM
QUAD = 4
TILES_AHEAD = 1
ONES_ROWS = 16
LOG2E = 1.4426950408889634
VMEM_LIMIT = 56 * 1024 * 1024

BF16 = jnp.bfloat16
F32 = jnp.float32


def _cparams(n_axes):
    return pltpu.CompilerParams(dimension_semantics=("arbitrary",) * n_axes,
                                vmem_limit_bytes=VMEM_LIMIT)


def _resident(block_shape, index_map):
    return pl.BlockSpec(block_shape, index_map, pipeline_mode=pl.Buffered(1))


def _sigmoid(z):
    return 1.0 / (1.0 + jnp.exp(-z))


def _norm_mod(x, g, shift, scale):
    var = jnp.mean(x * x, axis=-1, keepdims=True)
    return (x * lax.rsqrt(var + RMS_EPS) * g) * (1.0 + scale) + shift


def _mod_kernel(c_ref, w_ref, b_ref, o_ref):
    c = c_ref[...]
    sc = (c * _sigmoid(c)).astype(BF16)
    o_ref[0] = jnp.dot(sc, w_ref[0].astype(BF16), preferred_element_type=F32) + b_ref[0]


def _modulation(cvec, w_mod, b_mod):
    depth, d, n = w_mod.shape
    bn = MOD_BLOCK
    return pl.pallas_call(
        _mod_kernel,
        grid=(depth, n // bn),
        in_specs=[pl.BlockSpec((8, d), lambda i, j: (0, 0)),
                  pl.BlockSpec((1, d, bn), lambda i, j: (i, 0, j)),
                  pl.BlockSpec((1, 1, bn), lambda i, j: (i, 0, j))],
        out_specs=pl.BlockSpec((1, 8, bn), lambda i, j: (i, 0, j)),
        out_shape=jax.ShapeDtypeStruct((depth, 8, n), F32),
        compiler_params=_cparams(2),
        name="modulation",
    )(cvec, w_mod, b_mod.reshape(depth, 1, n))


def _transpose_weight(w_ref, wt_scr):
    for c in range(wt_scr.shape[0] // W_CHUNK):
        cols = slice(c * W_CHUNK, (c + 1) * W_CHUNK)
        wt_scr[cols, :] = w_ref[0, :, cols].T.astype(BF16)


def _qkv_kernel(x_ref, mod0_ref, mod1_ref, g_ref, w_ref, gq_ref, gk_ref,
                cos0_ref, sin0_ref, cos1_ref, sin1_ref, q_ref, k_ref, v_ref, wt_ref, *, kvw, rope):
    pl.when(pl.program_id(0) == 0)(lambda: _transpose_weight(w_ref, wt_ref))
    g = g_ref[...]
    x = x_ref[...]
    mods = (mod0_ref[0, 0], mod1_ref[0, 0])
    h = jnp.concatenate(
        [_norm_mod(x[s * TM:(s + 1) * TM], g, mods[s][0:1], mods[s][1:2]).astype(BF16) for s in range(2)],
        axis=0)
    def project(rows):
        return lax.dot_general(wt_ref[rows], h, (((1,), (1,)), ((), ())), preferred_element_type=F32)

    yq = project(slice(0, D_MODEL))
    yk = project(slice(D_MODEL, D_MODEL + kvw))
    yv = project(slice(D_MODEL + kvw, D_MODEL + 2 * kvw))

    def head_norm(y, gain, n_heads, cos_sin):
        y3 = y.reshape(n_heads, HEAD_DIM, TM)
        ms = jnp.mean(y3 * y3, axis=1, keepdims=True)
        y3 = y3 * lax.rsqrt(ms + RMS_EPS) * gain
        if rope:
            half = HEAD_DIM // 2
            x1, x2 = y3[:, :half], y3[:, half:]
            c, s = cos_sin
            y3 = jnp.concatenate([x1 * c - x2 * s, x2 * c + x1 * s], axis=1)
        return y3.reshape(n_heads * HEAD_DIM, TM)

    tables = ((cos0_ref[0], sin0_ref[0]), (cos1_ref[0], sin1_ref[0]))
    for s in range(2):
        q = head_norm(yq[:, s * TM:(s + 1) * TM], gq_ref[...], N_HEADS, tables[s])
        q_ref[s] = (q * (HEAD_DIM ** -0.5 * LOG2E)).astype(BF16)
    for s in range(2):
        k = head_norm(yk[:, s * TM:(s + 1) * TM], gk_ref[...], kvw // HEAD_DIM, tables[s])
        k_ref[s] = k.T.astype(BF16)
    for s in range(2):
        v_ref[s] = yv[:, s * TM:(s + 1) * TM].astype(BF16)


def _qkv(x, mod, g, w_qkv, layer, g_q, g_k, cos_t, sin_t, *, rope):
    bsz, t_all, d = x.shape
    nt = t_all // TM
    n_tiles = bsz * nt
    assert n_tiles % 2 == 0
    n = w_qkv.shape[2]
    kvw = (n - d) // 2
    gq = jnp.broadcast_to(g_q[:, None], (HEAD_DIM, TM))
    gk = jnp.broadcast_to(g_k[:, None], (HEAD_DIM, TM))
    half = HEAD_DIM // 2

    def mod_map(s):
        return lambda j: ((2 * j + s) // nt, ((2 * j + s) % nt) // (nt - 1), 0, 0)

    def rope_map(s):
        return lambda j: ((2 * j + s) % nt, 0, 0)

    q_t, k_tok, v_t = pl.pallas_call(
        functools.partial(_qkv_kernel, kvw=kvw, rope=rope),
        grid=(n_tiles // 2,),
        in_specs=[pl.BlockSpec((2 * TM, d), lambda j: (j, 0)),
                  pl.BlockSpec((1, 1, 6, d), mod_map(0)),
                  pl.BlockSpec((1, 1, 6, d), mod_map(1)),
                  pl.BlockSpec((1, d), lambda j: (0, 0)),
                  _resident((1, d, n), lambda j: (layer, 0, 0)),
                  pl.BlockSpec((HEAD_DIM, TM), lambda j: (0, 0)),
                  pl.BlockSpec((HEAD_DIM, TM), lambda j: (0, 0)),
                  pl.BlockSpec((1, half, TM), rope_map(0)),
                  pl.BlockSpec((1, half, TM), rope_map(0)),
                  pl.BlockSpec((1, half, TM), rope_map(1)),
                  pl.BlockSpec((1, half, TM), rope_map(1))],
        out_specs=[pl.BlockSpec((2, d, TM), lambda j: (j, 0, 0)),
                   pl.BlockSpec((2, TM, kvw), lambda j: (j, 0, 0)),
                   pl.BlockSpec((2, kvw, TM), lambda j: (j, 0, 0))],
        out_shape=[jax.ShapeDtypeStruct((n_tiles, d, TM), BF16),
                   jax.ShapeDtypeStruct((n_tiles, TM, kvw), BF16),
                   jax.ShapeDtypeStruct((n_tiles, kvw, TM), BF16)],
        scratch_shapes=[pltpu.VMEM((n, d), BF16)],
        compiler_params=_cparams(1),
        name="qkv",
    )(x.reshape(bsz * t_all, d), mod, mod, g.reshape(1, d), w_qkv, gq, gk, cos_t, sin_t, cos_t, sin_t)
    return (q_t.reshape(bsz, nt, d, TM), k_tok.reshape(bsz, nt, TM, kvw), v_t.reshape(bsz, nt, kvw, TM))


def _k_lanes(kv_head):
    g = kv_head // (HEAD_PAIR // HEAD_DIM)
    return slice(g * HEAD_PAIR, (g + 1) * HEAD_PAIR)


def _padded_queries(q_ref, first_head, n_q_heads, kv_head):
    qs = [q_ref[0, 0, (first_head + j) * HEAD_DIM:(first_head + j + 1) * HEAD_DIM, :]
          for j in range(n_q_heads)]
    qcat = qs[0] if n_q_heads == 1 else jnp.concatenate(qs, axis=1)
    slots = HEAD_PAIR // HEAD_DIM
    blocks = [jnp.zeros_like(qcat)] * slots
    blocks[kv_head % slots] = qcat
    return jnp.concatenate(blocks, axis=0)


def _store_heads(o_ref, o, first_head, n_q_heads):
    for j in range(n_q_heads):
        o_ref[0, 0, (first_head + j) * HEAD_DIM:(first_head + j + 1) * HEAD_DIM, :] = (
            o[:, j * TM:(j + 1) * TM].astype(o_ref.dtype))


def _first_grid_step():
    return functools.reduce(jnp.logical_and, [pl.program_id(a) == 0 for a in range(2)])


def _project_out(o_scr, x_ref, mod_ref, wot_scr, y_ref):
    out_t = jnp.dot(wot_scr[...], o_scr[0, 0], preferred_element_type=F32)
    y_ref[0] = x_ref[0] + mod_ref[0, 0][2:3] * out_t.T


def _values_and_ones(v_rows):
    return jnp.concatenate([v_rows, jnp.ones((ONES_ROWS, v_rows.shape[1]), v_rows.dtype)], axis=0)


def _local_attn_kernel(*refs, kind, group, n_lat):
    q_ref = refs[0]
    k_refs = refs[1:5]
    v_refs = refs[5:9]
    extra_ref, x_ref, mod_ref, wo_ref, y_ref, o_ref, wot_scr = refs[9:16]
    pl.when(_first_grid_step())(lambda: _transpose_weight(wo_ref, wot_scr))
    t = pl.program_id(1)
    is_ctx = t >= n_lat
    valid = [jnp.logical_not(is_ctx),
             jnp.logical_and(t >= 1, jnp.logical_not(is_ctx)),
             t <= n_lat - 2,
             None]
    pen = [None if v is None else jnp.where(v, 0.0, NEG_INF).astype(F32) for v in valid]

    if kind == 1:
        rows = [(0, TM), (TM - SWA_RADIUS, SWA_RADIUS), (0, SWA_RADIUS), (0, TM)]
        band = []
        for (r0, nr), off in zip(rows[:3], (0, -1, 1)):
            key_i = lax.broadcasted_iota(jnp.int32, (nr, TM), 0) + (r0 + off * TM)
            qry_j = lax.broadcasted_iota(jnp.int32, (nr, TM), 1)
            band.append(jnp.where(jnp.abs(key_i - qry_j) <= SWA_RADIUS, 0.0, NEG_INF).astype(F32))
    else:
        rows = [(0, TM)] * 4
        tab = [jnp.where(t == 0, 0, jnp.where(t >= n_lat - 1, 6, 3)),
               jnp.where(t >= n_lat - 1, 5, 2),
               jnp.where(t == 0, 1, 4)]

    quad = QUAD
    width = quad * TM
    own_kv = group == 1

    def quad_queries(g):
        if not own_kv:
            return _padded_queries(q_ref, g * quad, quad, g)
        zeros = jnp.zeros((HEAD_DIM, TM), BF16)
        return jnp.concatenate(
            [jnp.concatenate([q_ref[0, 0, (g * quad + h) * HEAD_DIM:(g * quad + h + 1) * HEAD_DIM, :]
                              if j == h else zeros for j in range(quad)], axis=1)
             for h in range(quad)], axis=0)

    queries = {}

    def scores(g, c):
        if g not in queries:
            queries[g] = quad_queries(g)
        r0, nr = rows[c]
        lanes = slice(g * quad * HEAD_DIM, (g + 1) * quad * HEAD_DIM) if own_kv else _k_lanes(g)
        s = jnp.dot(k_refs[c][0, 0, r0:r0 + nr, lanes], queries[g], preferred_element_type=F32)
        if c < 3:
            if kind == 1:
                s = s + jnp.concatenate([band[c] + pen[c]] * quad, axis=1)
            else:
                s = jnp.concatenate([s[:, h * TM:(h + 1) * TM] + extra_ref[g * quad + h, tab[c]]
                                     for h in range(quad)], axis=1) + pen[c]
        return s

    def values(g, h, c):
        r0, nr = rows[c]
        n = g * quad + h if own_kv else g
        return _values_and_ones(v_refs[c][0, 0, n * HEAD_DIM:(n + 1) * HEAD_DIM, r0:r0 + nr])

    slots = (3, 0, 1, 2)
    order = [(g, c) for g in range(N_HEADS // quad) for c in slots]
    s_next = scores(*order[0])
    for i, (g, c) in enumerate(order):
        s = s_next
        if i + 1 < len(order):
            s_next = scores(*order[i + 1])
        if c == slots[0]:
            m = jnp.full((1, width), NEG_INF, F32)
            acc = jnp.zeros((HEAD_DIM + ONES_ROWS, width), F32)
        m_new = jnp.maximum(m, jnp.max(s, axis=0, keepdims=True))
        p = jnp.exp2(s - m_new).astype(BF16)
        if own_kv:
            pv = jnp.concatenate([jnp.dot(values(g, h, c), p[:, h * TM:(h + 1) * TM],
                                          preferred_element_type=F32) for h in range(quad)], axis=1)
        else:
            pv = jnp.dot(values(g, 0, c), p, preferred_element_type=F32)
        acc = jnp.exp2(m - m_new) * acc + pv
        m = m_new
        if c == slots[-1]:
            l = acc[HEAD_DIM:HEAD_DIM + 1]
            if kind == 1:
                sink = extra_ref[g]
                l = l + jnp.exp2(sink - m)
            _store_heads(o_ref, acc[:HEAD_DIM] / l, g * quad, quad)
    _project_out(o_ref, x_ref, mod_ref, wot_scr, y_ref)


def _local_attention(q_t, k_tok, v_t, extra, x, mod, w_o, layer, *, kind, nt_out):
    bsz, nt, d, _ = q_t.shape
    n_lat = nt - 1
    kvw = k_tok.shape[-1]
    ctx_idx = nt - 1
    kern = functools.partial(_local_attn_kernel, kind=kind, group=N_HEADS * HEAD_DIM // kvw, n_lat=n_lat)
    chunk_maps = [lambda b, t: (b, t, 0, 0),
                  lambda b, t: (b, jnp.maximum(t - 1, 0), 0, 0),
                  lambda b, t: (b, jnp.minimum(t + 1, n_lat - 1), 0, 0),
                  lambda b, t: (b, ctx_idx, 0, 0)]
    return pl.pallas_call(
        kern,
        grid=(bsz, nt_out),
        in_specs=([pl.BlockSpec((1, 1, d, TM), lambda b, t: (b, t, 0, 0))]
                  + [pl.BlockSpec((1, 1, TM, kvw), f) for f in chunk_maps]
                  + [pl.BlockSpec((1, 1, kvw, TM), f) for f in chunk_maps]
                  + [_resident(extra.shape, lambda b, t: (0,) * extra.ndim),
                     pl.BlockSpec((1, TM, d), lambda b, t: (b, t, 0)),
                     pl.BlockSpec((1, 1, 6, d), lambda b, t: (b, t // n_lat, 0, 0)),
                     _resident((1, d, d), lambda b, t: (layer, 0, 0))]),
        out_specs=pl.BlockSpec((1, TM, d), lambda b, t: (b, t, 0)),
        out_shape=jax.ShapeDtypeStruct((bsz, nt_out * TM, d), F32),
        scratch_shapes=[pltpu.VMEM((1, 1, d, TM), BF16), pltpu.VMEM((d, d), BF16)],
        compiler_params=_cparams(2),
        name="na_attention" if kind == 0 else "swa_attention",
    )(q_t, k_tok, k_tok, k_tok, k_tok, v_t, v_t, v_t, v_t, extra, x, mod, w_o)


def _na_bias_kernel(rpb_ref, o_ref, col_scr, *, combos, grid_rows):
    h = pl.program_id(0)
    n_dr, n_dc = 2 * NA_WIN_H - 1, 2 * NA_WIN_W - 1
    kc = lax.broadcasted_iota(jnp.int32, (GRID_W, 2 * GRID_W), 0)
    lane = lax.broadcasted_iota(jnp.int32, (GRID_W, 2 * GRID_W), 1)
    qc = lane & (GRID_W - 1)
    dc = kc - qc + (NA_WIN_W - 1)
    cs = jnp.clip(qc - NA_WIN_W // 2, 0, GRID_W - NA_WIN_W)
    col_ok = jnp.logical_and(kc >= cs, kc < cs + NA_WIN_W)

    def fill(a, carry):
        base = (h * n_dr + a) * n_dc
        g = jnp.full(kc.shape, rpb_ref[base], F32)
        for b in range(1, n_dc):
            g = jnp.where(dc == b, rpb_ref[base + b], g)
        col_scr[a] = jnp.where(col_ok, g * LOG2E, NEG_INF)
        return carry

    lax.fori_loop(0, n_dr, fill, 0)

    rows_per_tile = TM // GRID_W
    left = lane < GRID_W
    masked = jnp.full(kc.shape, NEG_INF, F32)
    for ci, (tile, off) in enumerate(combos):
        for kr in range(rows_per_tile):
            k_row = (tile + off) * rows_per_tile + kr
            for j in range(rows_per_tile // 2):
                halves = []
                for qr in (2 * j, 2 * j + 1):
                    q_row = tile * rows_per_tile + qr
                    rs = min(max(q_row - NA_WIN_H // 2, 0), grid_rows - NA_WIN_H)
                    ok = rs <= k_row < rs + NA_WIN_H
                    halves.append(col_scr[k_row - q_row + NA_WIN_H - 1] if ok else masked)
                o_ref[0, ci, kr * GRID_W:(kr + 1) * GRID_W, j * 2 * GRID_W:(j + 1) * 2 * GRID_W] = (
                    jnp.where(left, halves[0], halves[1]))


def _na_bias_tables(rpb, n_lat):
    n_heads = rpb.shape[0]
    combos = ((0, 0), (0, 1), (1, -1), (1, 0), (1, 1), (n_lat - 1, -1), (n_lat - 1, 0))
    return pl.pallas_call(
        functools.partial(_na_bias_kernel, combos=combos, grid_rows=n_lat * TM // GRID_W),
        grid=(n_heads,),
        in_specs=[pl.BlockSpec(memory_space=pltpu.SMEM)],
        out_specs=pl.BlockSpec((1, len(combos), TM, TM), lambda h: (h, 0, 0, 0)),
        out_shape=jax.ShapeDtypeStruct((n_heads, len(combos), TM, TM), F32),
        scratch_shapes=[pltpu.VMEM((2 * NA_WIN_H - 1, GRID_W, 2 * GRID_W), F32)],
        compiler_params=_cparams(1),
        name="na_bias",
    )(rpb.reshape(-1))


def _global_attn_kernel(q_ref, k_ref, v_ref, x_ref, mod_ref, wo_ref, y_ref, o_ref, wot_scr, s_scr, *, n_lat):
    pl.when(_first_grid_step())(lambda: _transpose_weight(wo_ref, wot_scr))
    t = pl.program_id(1)
    group = N_HEADS // N_KV_HEADS
    width = group * TM
    n_buf = TILES_AHEAD + 1
    queries = [_padded_queries(q_ref, n * group, group, n) for n in range(N_KV_HEADS)]

    def attend(first_tile):
        units = [(n, c) for n in range(N_KV_HEADS) for c in range(first_tile, n_lat + 1)]

        def scores(i):
            n, c = units[i]
            s = jnp.dot(k_ref[0, c, :, _k_lanes(n)], queries[n], preferred_element_type=F32)
            s_scr[i % n_buf] = s
            return jnp.max(s, axis=0, keepdims=True)

        tile_max = {i: scores(i) for i in range(min(TILES_AHEAD, len(units)))}
        for i, (n, c) in enumerate(units):
            if i + TILES_AHEAD < len(units):
                tile_max[i + TILES_AHEAD] = scores(i + TILES_AHEAD)
            if c == first_tile:
                m = jnp.full((1, width), NEG_INF, F32)
                acc = jnp.zeros((HEAD_DIM + ONES_ROWS, width), F32)
            m_new = jnp.maximum(m, tile_max.pop(i))
            p = jnp.exp2(s_scr[i % n_buf] - m_new).astype(BF16)
            pv = jnp.dot(_values_and_ones(v_ref[0, c, n * HEAD_DIM:(n + 1) * HEAD_DIM, :]), p,
                         preferred_element_type=F32)
            acc = jnp.exp2(m - m_new) * acc + pv
            m = m_new
            if c == n_lat:
                _store_heads(o_ref, acc[:HEAD_DIM] / acc[HEAD_DIM:HEAD_DIM + 1], n * group, group)

    pl.when(t < n_lat)(lambda: attend(0))
    pl.when(t >= n_lat)(lambda: attend(n_lat))
    _project_out(o_ref, x_ref, mod_ref, wot_scr, y_ref)


def _global_attention(q_t, k_tok, v_t, x, mod, w_o, layer, *, nt_out):
    bsz, nt, d, _ = q_t.shape
    kvw = k_tok.shape[-1]
    n_lat = nt - 1
    assert n_lat % 2 == 0 and kvw == N_KV_HEADS * HEAD_DIM
    return pl.pallas_call(
        functools.partial(_global_attn_kernel, n_lat=n_lat),
        grid=(bsz, nt_out),
        in_specs=[pl.BlockSpec((1, 1, d, TM), lambda b, t: (b, t, 0, 0)),
                  pl.BlockSpec((1, nt, TM, kvw), lambda b, t: (b, 0, 0, 0)),
                  pl.BlockSpec((1, nt, kvw, TM), lambda b, t: (b, 0, 0, 0)),
                  pl.BlockSpec((1, TM, d), lambda b, t: (b, t, 0)),
                  pl.BlockSpec((1, 1, 6, d), lambda b, t: (b, t // n_lat, 0, 0)),
                  _resident((1, d, d), lambda b, t: (layer, 0, 0))],
        out_specs=pl.BlockSpec((1, TM, d), lambda b, t: (b, t, 0)),
        out_shape=jax.ShapeDtypeStruct((bsz, nt_out * TM, d), F32),
        scratch_shapes=[pltpu.VMEM((1, 1, d, TM), BF16), pltpu.VMEM((d, d), BF16),
                        pltpu.VMEM((TILES_AHEAD + 1, TM, N_HEADS // N_KV_HEADS * TM), F32)],
        compiler_params=_cparams(2),
        name="global_attention",
    )(q_t, k_tok, v_t, x, mod, w_o)


def _ffn_kernel(*refs, n_lat, tiles_per_batch):
    x_ref, xp_ref, xn_ref = refs[:3]
    mod_refs = refs[3:3 + FFN_TILES]
    g_ref, wup_ref, cw_ref, cb_ref, wdn_ref, y_ref, h_scr, a_scr = refs[3 + FFN_TILES:]
    rows = FFN_TILES * TM
    j = pl.program_id(0)
    g = g_ref[...]
    mods = [ref[0, 0] for ref in mod_refs]
    x = x_ref[...]
    for s in range(FFN_TILES):
        h_scr[s * TM:(s + 1) * TM] = _norm_mod(
            x[s * TM:(s + 1) * TM], g, mods[s][3:4], mods[s][4:5]).astype(BF16)
    h_scr[rows:rows + 2 * HALO] = jnp.concatenate(
        [_norm_mod(xp_ref[...], g, mods[0][3:4], mods[0][4:5]),
         _norm_mod(xn_ref[...], g, mods[-1][3:4], mods[-1][4:5])], axis=0).astype(BF16)

    tile = [(FFN_TILES * j + s) % tiles_per_batch for s in range(FFN_TILES)]
    prev_ok = [jnp.logical_and(t >= 1, t < n_lat) for t in tile]
    next_ok = [t <= n_lat - 2 for t in tile]
    row = lax.broadcasted_iota(jnp.int32, (rows, FF_CHUNK), 0)
    seam_cut_up = functools.reduce(jnp.logical_or, [
        jnp.logical_and(row == s * TM, jnp.logical_not(prev_ok[s])) for s in range(1, FFN_TILES)])
    seam_cut_dn = functools.reduce(jnp.logical_or, [
        jnp.logical_and(row == s * TM - 1, jnp.logical_not(next_ok[s - 1])) for s in range(1, FFN_TILES)])
    h = h_scr[...]
    d_ff = wdn_ref.shape[0]

    def up_conv(cols):
        u = jnp.dot(h, wup_ref[:, cols], preferred_element_type=F32)
        um = u[0:rows]
        u_before = jnp.where(prev_ok[0], u[rows + HALO - 1:rows + HALO], 0.0)
        u_after = jnp.where(next_ok[-1], u[rows + HALO:rows + HALO + 1], 0.0)
        up = jnp.where(row == 0, u_before, jnp.where(seam_cut_up, 0.0, pltpu.roll(um, 1, 0)))
        dn = jnp.where(row == rows - 1, u_after, jnp.where(seam_cut_dn, 0.0, pltpu.roll(um, rows - 1, 0)))
        cw = cw_ref[:, cols]
        return up * cw[0:1] + um * cw[1:2] + dn * cw[2:3] + cb_ref[:, cols]

    for c in range(d_ff // FF_CHUNK):
        a = up_conv(slice(c * FF_CHUNK, (c + 1) * FF_CHUNK))
        gate = up_conv(slice(d_ff + c * FF_CHUNK, d_ff + (c + 1) * FF_CHUNK))
        a_scr[:, c * FF_CHUNK:(c + 1) * FF_CHUNK] = (a * (gate * _sigmoid(gate))).astype(BF16)
    down = jnp.dot(a_scr[...], wdn_ref[...], preferred_element_type=F32)
    for s in range(FFN_TILES):
        y_ref[s * TM:(s + 1) * TM] = x[s * TM:(s + 1) * TM] + mods[s][5:6] * down[s * TM:(s + 1) * TM]


def _conv_ffn(x, mod, g, w_up, conv_w, conv_b, w_down, *, n_lat):
    bsz, t_all, d = x.shape
    tiles_per_batch = t_all // TM
    assert bsz * tiles_per_batch % FFN_TILES == 0
    n_steps = bsz * tiles_per_batch // FFN_TILES
    d_ff = w_down.shape[0]
    rows = FFN_TILES * TM
    halos_per_step = rows // HALO
    last_halo = bsz * t_all // HALO - 1

    def mod_map(s):
        def index(j):
            tile = FFN_TILES * j + s
            return (tile // tiles_per_batch, (tile % tiles_per_batch) // n_lat, 0, 0)
        return index

    out = pl.pallas_call(
        functools.partial(_ffn_kernel, n_lat=n_lat, tiles_per_batch=tiles_per_batch),
        grid=(n_steps,),
        in_specs=([pl.BlockSpec((rows, d), lambda j: (j, 0)),
                   pl.BlockSpec((HALO, d), lambda j: (jnp.maximum(j * halos_per_step - 1, 0), 0)),
                   pl.BlockSpec((HALO, d), lambda j: (jnp.minimum((j + 1) * halos_per_step, last_halo), 0))]
                  + [pl.BlockSpec((1, 1, 6, d), mod_map(s)) for s in range(FFN_TILES)]
                  + [pl.BlockSpec((1, d), lambda j: (0, 0)),
                     _resident((d, 2 * d_ff), lambda j: (0, 0)),
                     pl.BlockSpec((CONV_W, 2 * d_ff), lambda j: (0, 0)),
                     pl.BlockSpec((1, 2 * d_ff), lambda j: (0, 0)),
                     _resident((d_ff, d), lambda j: (0, 0))]),
        out_specs=pl.BlockSpec((rows, d), lambda j: (j, 0)),
        out_shape=jax.ShapeDtypeStruct((bsz * t_all, d), F32),
        scratch_shapes=[pltpu.VMEM((rows + 2 * HALO, d), BF16), pltpu.VMEM((rows, d_ff), BF16)],
        compiler_params=_cparams(1),
        name="conv_ffn",
    )(*(x.reshape(bsz * t_all, d),) * 3, *(mod,) * FFN_TILES, g.reshape(1, d),
      w_up.astype(BF16), conv_w, conv_b.reshape(1, 2 * d_ff), w_down.astype(BF16))
    return out.reshape(bsz, t_all, d)


def _rope_tables(seq, n_ctx):
    t = jnp.arange(seq)
    row = (t // GRID_W).astype(F32)
    col = (t % GRID_W).astype(F32)
    n_axis = HEAD_DIM // 4
    inv = ROPE_THETA ** (-jnp.arange(n_axis, dtype=F32) / n_axis)
    ang = jnp.concatenate([row[:, None] * inv, col[:, None] * inv], axis=-1)
    cos = jnp.concatenate([jnp.cos(ang), jnp.ones((n_ctx, HEAD_DIM // 2), F32)], axis=0)
    sin = jnp.concatenate([jnp.sin(ang), jnp.zeros((n_ctx, HEAD_DIM // 2), F32)], axis=0)
    nt = (seq + n_ctx) // TM
    to_tiles = lambda a: a.reshape(nt, TM, HEAD_DIM // 2).transpose(0, 2, 1)
    return to_tiles(cos), to_tiles(sin)


def kernel(x, c, ctx, c_ctx, w_mod, b_mod, g_attn, g_ffn, na_w_qkv, na_g_q, na_g_k, na_rpb, na_w_o, swa_w_qkv, swa_g_q, swa_g_k, swa_sink, swa_w_o, ga_w_qkv, ga_g_q, ga_g_k, ga_w_o, ffn_w_up, ffn_conv_w, ffn_conv_b, ffn_w_down):
    bsz, seq, d = x.shape
    n_ctx = ctx.shape[1]
    depth = w_mod.shape[0]
    assert d == D_MODEL and seq % TM == 0 and n_ctx == TM and seq == GRID_W * GRID_W
    n_lat = seq // TM
    nt = n_lat + 1
    group = N_HEADS // N_KV_HEADS

    cvec = jnp.zeros((8, d), F32).at[:bsz].set(c).at[bsz].set(c_ctx)
    mods = _modulation(cvec, w_mod, b_mod)
    mod_lat = mods[:, :bsz].reshape(depth, bsz, 1, 6, d)
    mod_ctx = jnp.broadcast_to(mods[:, bsz].reshape(depth, 1, 1, 6, d), (depth, bsz, 1, 6, d))
    mod_all = jnp.concatenate([mod_lat, mod_ctx], axis=2)

    cos_t, sin_t = _rope_tables(seq, n_ctx)
    stream = jnp.concatenate([x, ctx], axis=1)

    for i in range(depth):
        last = i == depth - 1
        kind, j = i % N_MIXERS, i // N_MIXERS
        nt_out = n_lat if last else nt
        mod = mod_all[i]
        if kind == 0:
            w_in, w_out, gq, gk = na_w_qkv, na_w_o, na_g_q[j], na_g_k[j]
        elif kind == 1:
            w_in, w_out, gq, gk = swa_w_qkv, swa_w_o, swa_g_q[j], swa_g_k[j]
        else:
            w_in, w_out, gq, gk = ga_w_qkv, ga_w_o, ga_g_q[j], ga_g_k[j]
        q_t, k_tok, v_t = _qkv(stream, mod, g_attn[i], w_in, j, gq, gk, cos_t, sin_t, rope=kind != 0)
        if kind == 0:
            stream = _local_attention(q_t, k_tok, v_t, _na_bias_tables(na_rpb[j], n_lat),
                                      stream, mod, w_out, j, kind=0, nt_out=nt_out)
        elif kind == 1:
            sink = jnp.broadcast_to((swa_sink[j] * LOG2E).reshape(N_KV_HEADS, 1, group, 1),
                                    (N_KV_HEADS, 1, group, TM)).reshape(N_KV_HEADS, 1, group * TM)
            stream = _local_attention(q_t, k_tok, v_t, sink, stream, mod, w_out, j, kind=1, nt_out=nt_out)
        else:
            stream = _global_attention(q_t, k_tok, v_t, stream, mod, w_out, j, nt_out=nt_out)
        stream = _conv_ffn(stream, mod, g_ffn[i], ffn_w_up[i], ffn_conv_w[i], ffn_conv_b[i],
                           ffn_w_down[i], n_lat=n_lat)
    return stream
```

```python
import functools

import jax
import jax.numpy as jnp
from jax import lax
from jax.experimental import pallas as pl
from jax.experimental.pallas import tpu as pltpu

D_MODEL = 1024
GRID_W = 64
HEAD_DIM = 64
N_HEADS = D_MODEL // HEAD_DIM
N_KV_HEADS = 4
N_MIXERS = 3
NA_WIN_H = 8
NA_WIN_W = 16
SWA_RADIUS = 128
ROPE_THETA = 10000.0
CONV_W = 3
RMS_EPS = 1e-6
NEG_INF = -1e30

TM = 256
HALO = 8
FF_CHUNK = 256
FFN_TILES = 2
W_CHUNK = 256
MOD_BLOCK = 1536
HEAD_PAIR = 2 * HEAD_DIM
QUAD = 4
TILES_AHEAD = 1
ONES_ROWS = 16
LOG2E = 1.4426950408889634
VMEM_LIMIT = 56 * 1024 * 1024

BF16 = jnp.bfloat16
F32 = jnp.float32


def _cparams(n_axes):
    return pltpu.CompilerParams(dimension_semantics=("arbitrary",) * n_axes,
                                vmem_limit_bytes=VMEM_LIMIT)


def _resident(block_shape, index_map):
    return pl.BlockSpec(block_shape, index_map, pipeline_mode=pl.Buffered(1))


def _sigmoid(z):
    return 1.0 / (1.0 + jnp.exp(-z))


def _norm_mod(x, g, shift, scale):
    var = jnp.mean(x * x, axis=-1, keepdims=True)
    return (x * lax.rsqrt(var + RMS_EPS) * g) * (1.0 + scale) + shift


def _mod_kernel(c_ref, w_ref, b_ref, o_ref):
    c = c_ref[...]
    sc = (c * _sigmoid(c)).astype(BF16)
    o_ref[0] = jnp.dot(sc, w_ref[0].astype(BF16), preferred_element_type=F32) + b_ref[0]


def _modulation(cvec, w_mod, b_mod):
    depth, d, n = w_mod.shape
    bn = MOD_BLOCK
    return pl.pallas_call(
        _mod_kernel,
        grid=(depth, n // bn),
        in_specs=[pl.BlockSpec((8, d), lambda i, j: (0, 0)),
                  pl.BlockSpec((1, d, bn), lambda i, j: (i, 0, j)),
                  pl.BlockSpec((1, 1, bn), lambda i, j: (i, 0, j))],
        out_specs=pl.BlockSpec((1, 8, bn), lambda i, j: (i, 0, j)),
        out_shape=jax.ShapeDtypeStruct((depth, 8, n), F32),
        compiler_params=_cparams(2),
        name="modulation",
    )(cvec, w_mod, b_mod.reshape(depth, 1, n))


def _transpose_weight(w_ref, wt_scr):
    for c in range(wt_scr.shape[0] // W_CHUNK):
        cols = slice(c * W_CHUNK, (c + 1) * W_CHUNK)
        wt_scr[cols, :] = w_ref[0, :, cols].T.astype(BF16)


def _qkv_kernel(*refs, kvw, rope, tiles_per_batch, split_input):
    n_x = 4 if split_input else 1
    x_refs = refs[:n_x]
    (mod0_ref, mod1_ref, g_ref, w_ref, gq_ref, gk_ref, cos0_ref, sin0_ref, cos1_ref, sin1_ref,
     q_ref, k_ref, v_ref, wt_ref) = refs[n_x:]
    pl.when(pl.program_id(0) == 0)(lambda: _transpose_weight(w_ref, wt_ref))
    g = g_ref[...]
    if split_input:
        is_ctx = [(2 * pl.program_id(0) + s) % tiles_per_batch == tiles_per_batch - 1 for s in range(2)]
        tiles = [jnp.where(is_ctx[s], x_refs[2 + s][...], x_refs[s][...]) for s in range(2)]
    else:
        tiles = [x_refs[0][s * TM:(s + 1) * TM] for s in range(2)]
    mods = (mod0_ref[0, 0], mod1_ref[0, 0])
    h = jnp.concatenate(
        [_norm_mod(tiles[s], g, mods[s][0:1], mods[s][1:2]).astype(BF16) for s in range(2)], axis=0)
    def project(rows):
        return lax.dot_general(wt_ref[rows], h, (((1,), (1,)), ((), ())), preferred_element_type=F32)

    yq = project(slice(0, D_MODEL))
    yk = project(slice(D_MODEL, D_MODEL + kvw))
    yv = project(slice(D_MODEL + kvw, D_MODEL + 2 * kvw))

    def head_norm(y, gain, n_heads, cos_sin):
        y3 = y.reshape(n_heads, HEAD_DIM, TM)
        ms = jnp.mean(y3 * y3, axis=1, keepdims=True)
        y3 = y3 * lax.rsqrt(ms + RMS_EPS) * gain
        if rope:
            half = HEAD_DIM // 2
            x1, x2 = y3[:, :half], y3[:, half:]
            c, s = cos_sin
            y3 = jnp.concatenate([x1 * c - x2 * s, x2 * c + x1 * s], axis=1)
        return y3.reshape(n_heads * HEAD_DIM, TM)

    tables = ((cos0_ref[0], sin0_ref[0]), (cos1_ref[0], sin1_ref[0]))
    for s in range(2):
        q = head_norm(yq[:, s * TM:(s + 1) * TM], gq_ref[...], N_HEADS, tables[s])
        q_ref[s] = (q * (HEAD_DIM ** -0.5 * LOG2E)).astype(BF16)
    for s in range(2):
        k = head_norm(yk[:, s * TM:(s + 1) * TM], gk_ref[...], kvw // HEAD_DIM, tables[s])
        k_ref[s] = k.T.astype(BF16)
    for s in range(2):
        v_ref[s] = yv[:, s * TM:(s + 1) * TM].astype(BF16)


def _qkv(x, mod, g, w_qkv, layer, g_q, g_k, cos_t, sin_t, *, rope):
    split_input = isinstance(x, tuple)
    bsz, _, d = (x[0] if split_input else x).shape
    nt = (x[0].shape[1] + x[1].shape[1]) // TM if split_input else x.shape[1] // TM
    n_lat = nt - 1
    n_tiles = bsz * nt
    assert n_tiles % 2 == 0
    n = w_qkv.shape[2]
    kvw = (n - d) // 2
    gq = jnp.broadcast_to(g_q[:, None], (HEAD_DIM, TM))
    gk = jnp.broadcast_to(g_k[:, None], (HEAD_DIM, TM))
    half = HEAD_DIM // 2

    def mod_map(s):
        return lambda j: ((2 * j + s) // nt, ((2 * j + s) % nt) // (nt - 1), 0, 0)

    def rope_map(s):
        return lambda j: ((2 * j + s) % nt, 0, 0)

    if split_input:
        def lat_map(s):
            return lambda j: ((2 * j + s) // nt * n_lat + jnp.minimum((2 * j + s) % nt, n_lat - 1), 0)

        def ctx_map(s):
            return lambda j: ((2 * j + s) // nt, 0)

        x_args = [x[0].reshape(bsz * n_lat * TM, d)] * 2 + [x[1].reshape(bsz * TM, d)] * 2
        x_specs = [pl.BlockSpec((TM, d), f(s)) for f in (lat_map, ctx_map) for s in range(2)]
    else:
        x_args = [x.reshape(bsz * nt * TM, d)]
        x_specs = [pl.BlockSpec((2 * TM, d), lambda j: (j, 0))]

    q_t, k_tok, v_t = pl.pallas_call(
        functools.partial(_qkv_kernel, kvw=kvw, rope=rope, tiles_per_batch=nt, split_input=split_input),
        grid=(n_tiles // 2,),
        in_specs=x_specs + [
                  pl.BlockSpec((1, 1, 6, d), mod_map(0)),
                  pl.BlockSpec((1, 1, 6, d), mod_map(1)),
                  pl.BlockSpec((1, d), lambda j: (0, 0)),
                  _resident((1, d, n), lambda j: (layer, 0, 0)),
                  pl.BlockSpec((HEAD_DIM, TM), lambda j: (0, 0)),
                  pl.BlockSpec((HEAD_DIM, TM), lambda j: (0, 0)),
                  pl.BlockSpec((1, half, TM), rope_map(0)),
                  pl.BlockSpec((1, half, TM), rope_map(0)),
                  pl.BlockSpec((1, half, TM), rope_map(1)),
                  pl.BlockSpec((1, half, TM), rope_map(1))],
        out_specs=[pl.BlockSpec((2, d, TM), lambda j: (j, 0, 0)),
                   pl.BlockSpec((2, TM, kvw), lambda j: (j, 0, 0)),
                   pl.BlockSpec((2, kvw, TM), lambda j: (j, 0, 0))],
        out_shape=[jax.ShapeDtypeStruct((n_tiles, d, TM), BF16),
                   jax.ShapeDtypeStruct((n_tiles, TM, kvw), BF16),
                   jax.ShapeDtypeStruct((n_tiles, kvw, TM), BF16)],
        scratch_shapes=[pltpu.VMEM((n, d), BF16)],
        compiler_params=_cparams(1),
        name="qkv",
    )(*x_args, mod, mod, g.reshape(1, d), w_qkv, gq, gk, cos_t, sin_t, cos_t, sin_t)
    return (q_t.reshape(bsz, nt, d, TM), k_tok.reshape(bsz, nt, TM, kvw), v_t.reshape(bsz, nt, kvw, TM))


def _k_lanes(kv_head):
    g = kv_head // (HEAD_PAIR // HEAD_DIM)
    return slice(g * HEAD_PAIR, (g + 1) * HEAD_PAIR)


def _padded_queries(q_ref, first_head, n_q_heads, kv_head):
    qs = [q_ref[0, 0, (first_head + j) * HEAD_DIM:(first_head + j + 1) * HEAD_DIM, :]
          for j in range(n_q_heads)]
    qcat = qs[0] if n_q_heads == 1 else jnp.concatenate(qs, axis=1)
    slots = HEAD_PAIR // HEAD_DIM
    blocks = [jnp.zeros_like(qcat)] * slots
    blocks[kv_head % slots] = qcat
    return jnp.concatenate(blocks, axis=0)


def _store_heads(o_ref, o, first_head, n_q_heads):
    for j in range(n_q_heads):
        o_ref[0, 0, (first_head + j) * HEAD_DIM:(first_head + j + 1) * HEAD_DIM, :] = (
            o[:, j * TM:(j + 1) * TM].astype(o_ref.dtype))


def _first_grid_step():
    return functools.reduce(jnp.logical_and, [pl.program_id(a) == 0 for a in range(2)])


def _project_out(o_scr, x_tile, mod_ref, wot_scr, y_ref):
    out_t = jnp.dot(wot_scr[...], o_scr[0, 0], preferred_element_type=F32)
    y_ref[0] = x_tile + mod_ref[0, 0][2:3] * out_t.T


def _values_and_ones(v_rows):
    return jnp.concatenate([v_rows, jnp.ones((ONES_ROWS, v_rows.shape[1]), v_rows.dtype)], axis=0)


def _local_attn_kernel(*refs, kind, group, n_lat, split_input):
    q_ref = refs[0]
    k_refs = refs[1:5]
    v_refs = refs[5:9]
    extra_ref = refs[9]
    x_refs = refs[10:12] if split_input else refs[10:11]
    mod_ref, wo_ref, y_ref, o_ref, wot_scr = refs[10 + len(x_refs):]
    pl.when(_first_grid_step())(lambda: _transpose_weight(wo_ref, wot_scr))
    t = pl.program_id(1)
    is_ctx = t >= n_lat
    x_tile = jnp.where(is_ctx, x_refs[1][0], x_refs[0][0]) if split_input else x_refs[0][0]
    valid = [jnp.logical_not(is_ctx),
             jnp.logical_and(t >= 1, jnp.logical_not(is_ctx)),
             t <= n_lat - 2,
             None]
    pen = [None if v is None else jnp.where(v, 0.0, NEG_INF).astype(F32) for v in valid]

    if kind == 1:
        rows = [(0, TM), (TM - SWA_RADIUS, SWA_RADIUS), (0, SWA_RADIUS), (0, TM)]
        band = []
        for (r0, nr), off in zip(rows[:3], (0, -1, 1)):
            key_i = lax.broadcasted_iota(jnp.int32, (nr, TM), 0) + (r0 + off * TM)
            qry_j = lax.broadcasted_iota(jnp.int32, (nr, TM), 1)
            band.append(jnp.where(jnp.abs(key_i - qry_j) <= SWA_RADIUS, 0.0, NEG_INF).astype(F32))
    else:
        rows = [(0, TM)] * 4
        tab = [jnp.where(t == 0, 0, jnp.where(t >= n_lat - 1, 6, 3)),
               jnp.where(t >= n_lat - 1, 5, 2),
               jnp.where(t == 0, 1, 4)]

    quad = QUAD
    width = quad * TM
    own_kv = group == 1

    def quad_queries(g):
        if not own_kv:
            return _padded_queries(q_ref, g * quad, quad, g)
        zeros = jnp.zeros((HEAD_DIM, TM), BF16)
        return jnp.concatenate(
            [jnp.concatenate([q_ref[0, 0, (g * quad + h) * HEAD_DIM:(g * quad + h + 1) * HEAD_DIM, :]
                              if j == h else zeros for j in range(quad)], axis=1)
             for h in range(quad)], axis=0)

    queries = {}

    def scores(g, c):
        if g not in queries:
            queries[g] = quad_queries(g)
        r0, nr = rows[c]
        lanes = slice(g * quad * HEAD_DIM, (g + 1) * quad * HEAD_DIM) if own_kv else _k_lanes(g)
        s = jnp.dot(k_refs[c][0, 0, r0:r0 + nr, lanes], queries[g], preferred_element_type=F32)
        if c < 3:
            if kind == 1:
                s = s + jnp.concatenate([band[c] + pen[c]] * quad, axis=1)
            else:
                s = jnp.concatenate([s[:, h * TM:(h + 1) * TM] + extra_ref[g * quad + h, tab[c]]
                                     for h in range(quad)], axis=1) + pen[c]
        return s

    def values(g, h, c):
        r0, nr = rows[c]
        n = g * quad + h if own_kv else g
        return _values_and_ones(v_refs[c][0, 0, n * HEAD_DIM:(n + 1) * HEAD_DIM, r0:r0 + nr])

    slots = (3, 0, 1, 2)
    order = [(g, c) for g in range(N_HEADS // quad) for c in slots]
    s_next = scores(*order[0])
    for i, (g, c) in enumerate(order):
        s = s_next
        if i + 1 < len(order):
            s_next = scores(*order[i + 1])
        if c == slots[0]:
            m = jnp.full((1, width), NEG_INF, F32)
            acc = jnp.zeros((HEAD_DIM + ONES_ROWS, width), F32)
        m_new = jnp.maximum(m, jnp.max(s, axis=0, keepdims=True))
        p = jnp.exp2(s - m_new).astype(BF16)
        if own_kv:
            pv = jnp.concatenate([jnp.dot(values(g, h, c), p[:, h * TM:(h + 1) * TM],
                                          preferred_element_type=F32) for h in range(quad)], axis=1)
        else:
            pv = jnp.dot(values(g, 0, c), p, preferred_element_type=F32)
        acc = jnp.exp2(m - m_new) * acc + pv
        m = m_new
        if c == slots[-1]:
            l = acc[HEAD_DIM:HEAD_DIM + 1]
            if kind == 1:
                sink = extra_ref[g]
                l = l + jnp.exp2(sink - m)
            _store_heads(o_ref, acc[:HEAD_DIM] / l, g * quad, quad)
    _project_out(o_ref, x_tile, mod_ref, wot_scr, y_ref)


def _local_attention(q_t, k_tok, v_t, extra, x, mod, w_o, layer, *, kind, nt_out):
    bsz, nt, d, _ = q_t.shape
    n_lat = nt - 1
    kvw = k_tok.shape[-1]
    ctx_idx = nt - 1
    split_input = isinstance(x, tuple)
    if split_input:
        x_args = list(x)
        x_specs = [pl.BlockSpec((1, TM, d), lambda b, t: (b, jnp.minimum(t, n_lat - 1), 0)),
                   pl.BlockSpec((1, TM, d), lambda b, t: (b, 0, 0))]
    else:
        x_args = [x]
        x_specs = [pl.BlockSpec((1, TM, d), lambda b, t: (b, t, 0))]
    kern = functools.partial(_local_attn_kernel, kind=kind, group=N_HEADS * HEAD_DIM // kvw, n_lat=n_lat,
                             split_input=split_input)
    chunk_maps = [lambda b, t: (b, t, 0, 0),
                  lambda b, t: (b, jnp.maximum(t - 1, 0), 0, 0),
                  lambda b, t: (b, jnp.minimum(t + 1, n_lat - 1), 0, 0),
                  lambda b, t: (b, ctx_idx, 0, 0)]
    return pl.pallas_call(
        kern,
        grid=(bsz, nt_out),
        in_specs=([pl.BlockSpec((1, 1, d, TM), lambda b, t: (b, t, 0, 0))]
                  + [pl.BlockSpec((1, 1, TM, kvw), f) for f in chunk_maps]
                  + [pl.BlockSpec((1, 1, kvw, TM), f) for f in chunk_maps]
                  + [_resident(extra.shape, lambda b, t: (0,) * extra.ndim)]
                  + x_specs
                  + [pl.BlockSpec((1, 1, 6, d), lambda b, t: (b, t // n_lat, 0, 0)),
                     _resident((1, d, d), lambda b, t: (layer, 0, 0))]),
        out_specs=pl.BlockSpec((1, TM, d), lambda b, t: (b, t, 0)),
        out_shape=jax.ShapeDtypeStruct((bsz, nt_out * TM, d), F32),
        scratch_shapes=[pltpu.VMEM((1, 1, d, TM), BF16), pltpu.VMEM((d, d), BF16)],
        compiler_params=_cparams(2),
        name="na_attention" if kind == 0 else "swa_attention",
    )(q_t, k_tok, k_tok, k_tok, k_tok, v_t, v_t, v_t, v_t, extra, *x_args, mod, w_o)


def _na_bias_kernel(rpb_ref, o_ref, col_scr, *, combos, grid_rows):
    h = pl.program_id(0)
    n_dr, n_dc = 2 * NA_WIN_H - 1, 2 * NA_WIN_W - 1
    kc = lax.broadcasted_iota(jnp.int32, (GRID_W, 2 * GRID_W), 0)
    lane = lax.broadcasted_iota(jnp.int32, (GRID_W, 2 * GRID_W), 1)
    qc = lane & (GRID_W - 1)
    dc = kc - qc + (NA_WIN_W - 1)
    cs = jnp.clip(qc - NA_WIN_W // 2, 0, GRID_W - NA_WIN_W)
    col_ok = jnp.logical_and(kc >= cs, kc < cs + NA_WIN_W)

    def fill(a, carry):
        base = (h * n_dr + a) * n_dc
        g = jnp.full(kc.shape, rpb_ref[base], F32)
        for b in range(1, n_dc):
            g = jnp.where(dc == b, rpb_ref[base + b], g)
        col_scr[a] = jnp.where(col_ok, g * LOG2E, NEG_INF)
        return carry

    lax.fori_loop(0, n_dr, fill, 0)

    rows_per_tile = TM // GRID_W
    left = lane < GRID_W
    masked = jnp.full(kc.shape, NEG_INF, F32)
    for ci, (tile, off) in enumerate(combos):
        for kr in range(rows_per_tile):
            k_row = (tile + off) * rows_per_tile + kr
            for j in range(rows_per_tile // 2):
                halves = []
                for qr in (2 * j, 2 * j + 1):
                    q_row = tile * rows_per_tile + qr
                    rs = min(max(q_row - NA_WIN_H // 2, 0), grid_rows - NA_WIN_H)
                    ok = rs <= k_row < rs + NA_WIN_H
                    halves.append(col_scr[k_row - q_row + NA_WIN_H - 1] if ok else masked)
                o_ref[0, ci, kr * GRID_W:(kr + 1) * GRID_W, j * 2 * GRID_W:(j + 1) * 2 * GRID_W] = (
                    jnp.where(left, halves[0], halves[1]))


def _na_bias_tables(rpb, n_lat):
    n_heads = rpb.shape[0]
    combos = ((0, 0), (0, 1), (1, -1), (1, 0), (1, 1), (n_lat - 1, -1), (n_lat - 1, 0))
    return pl.pallas_call(
        functools.partial(_na_bias_kernel, combos=combos, grid_rows=n_lat * TM // GRID_W),
        grid=(n_heads,),
        in_specs=[pl.BlockSpec(memory_space=pltpu.SMEM)],
        out_specs=pl.BlockSpec((1, len(combos), TM, TM), lambda h: (h, 0, 0, 0)),
        out_shape=jax.ShapeDtypeStruct((n_heads, len(combos), TM, TM), F32),
        scratch_shapes=[pltpu.VMEM((2 * NA_WIN_H - 1, GRID_W, 2 * GRID_W), F32)],
        compiler_params=_cparams(1),
        name="na_bias",
    )(rpb.reshape(-1))


def _global_attn_kernel(q_ref, k_ref, v_ref, x_ref, mod_ref, wo_ref, y_ref, o_ref, wot_scr, s_scr, *, n_lat):
    pl.when(_first_grid_step())(lambda: _transpose_weight(wo_ref, wot_scr))
    t = pl.program_id(1)
    group = N_HEADS // N_KV_HEADS
    width = group * TM
    n_buf = TILES_AHEAD + 1
    queries = [_padded_queries(q_ref, n * group, group, n) for n in range(N_KV_HEADS)]

    def attend(first_tile):
        units = [(n, c) for n in range(N_KV_HEADS) for c in range(first_tile, n_lat + 1)]

        def scores(i):
            n, c = units[i]
            s = jnp.dot(k_ref[0, c, :, _k_lanes(n)], queries[n], preferred_element_type=F32)
            s_scr[i % n_buf] = s
            return jnp.max(s, axis=0, keepdims=True)

        tile_max = {i: scores(i) for i in range(min(TILES_AHEAD, len(units)))}
        for i, (n, c) in enumerate(units):
            if i + TILES_AHEAD < len(units):
                tile_max[i + TILES_AHEAD] = scores(i + TILES_AHEAD)
            if c == first_tile:
                m = jnp.full((1, width), NEG_INF, F32)
                acc = jnp.zeros((HEAD_DIM + ONES_ROWS, width), F32)
            m_new = jnp.maximum(m, tile_max.pop(i))
            p = jnp.exp2(s_scr[i % n_buf] - m_new).astype(BF16)
            pv = jnp.dot(_values_and_ones(v_ref[0, c, n * HEAD_DIM:(n + 1) * HEAD_DIM, :]), p,
                         preferred_element_type=F32)
            acc = jnp.exp2(m - m_new) * acc + pv
            m = m_new
            if c == n_lat:
                _store_heads(o_ref, acc[:HEAD_DIM] / acc[HEAD_DIM:HEAD_DIM + 1], n * group, group)

    pl.when(t < n_lat)(lambda: attend(0))
    pl.when(t >= n_lat)(lambda: attend(n_lat))
    _project_out(o_ref, x_ref[0], mod_ref, wot_scr, y_ref)


def _global_attention(q_t, k_tok, v_t, x, mod, w_o, layer, *, nt_out):
    bsz, nt, d, _ = q_t.shape
    kvw = k_tok.shape[-1]
    n_lat = nt - 1
    assert n_lat % 2 == 0 and kvw == N_KV_HEADS * HEAD_DIM
    return pl.pallas_call(
        functools.partial(_global_attn_kernel, n_lat=n_lat),
        grid=(bsz, nt_out),
        in_specs=[pl.BlockSpec((1, 1, d, TM), lambda b, t: (b, t, 0, 0)),
                  pl.BlockSpec((1, nt, TM, kvw), lambda b, t: (b, 0, 0, 0)),
                  pl.BlockSpec((1, nt, kvw, TM), lambda b, t: (b, 0, 0, 0)),
                  pl.BlockSpec((1, TM, d), lambda b, t: (b, t, 0)),
                  pl.BlockSpec((1, 1, 6, d), lambda b, t: (b, t // n_lat, 0, 0)),
                  _resident((1, d, d), lambda b, t: (layer, 0, 0))],
        out_specs=pl.BlockSpec((1, TM, d), lambda b, t: (b, t, 0)),
        out_shape=jax.ShapeDtypeStruct((bsz, nt_out * TM, d), F32),
        scratch_shapes=[pltpu.VMEM((1, 1, d, TM), BF16), pltpu.VMEM((d, d), BF16),
                        pltpu.VMEM((TILES_AHEAD + 1, TM, N_HEADS // N_KV_HEADS * TM), F32)],
        compiler_params=_cparams(2),
        name="global_attention",
    )(q_t, k_tok, v_t, x, mod, w_o)


def _ffn_kernel(*refs, n_lat, tiles_per_batch):
    x_ref, xp_ref, xn_ref = refs[:3]
    mod_refs = refs[3:3 + FFN_TILES]
    g_ref, wup_ref, cw_ref, cb_ref, wdn_ref, y_ref, h_scr, a_scr = refs[3 + FFN_TILES:]
    rows = FFN_TILES * TM
    j = pl.program_id(0)
    g = g_ref[...]
    mods = [ref[0, 0] for ref in mod_refs]
    x = x_ref[...]
    for s in range(FFN_TILES):
        h_scr[s * TM:(s + 1) * TM] = _norm_mod(
            x[s * TM:(s + 1) * TM], g, mods[s][3:4], mods[s][4:5]).astype(BF16)
    h_scr[rows:rows + 2 * HALO] = jnp.concatenate(
        [_norm_mod(xp_ref[...], g, mods[0][3:4], mods[0][4:5]),
         _norm_mod(xn_ref[...], g, mods[-1][3:4], mods[-1][4:5])], axis=0).astype(BF16)

    tile = [(FFN_TILES * j + s) % tiles_per_batch for s in range(FFN_TILES)]
    prev_ok = [jnp.logical_and(t >= 1, t < n_lat) for t in tile]
    next_ok = [t <= n_lat - 2 for t in tile]
    row = lax.broadcasted_iota(jnp.int32, (rows, FF_CHUNK), 0)
    seam_cut_up = functools.reduce(jnp.logical_or, [
        jnp.logical_and(row == s * TM, jnp.logical_not(prev_ok[s])) for s in range(1, FFN_TILES)])
    seam_cut_dn = functools.reduce(jnp.logical_or, [
        jnp.logical_and(row == s * TM - 1, jnp.logical_not(next_ok[s - 1])) for s in range(1, FFN_TILES)])
    h = h_scr[...]
    d_ff = wdn_ref.shape[1]

    def up_conv(cols):
        u = jnp.dot(h, wup_ref[0, :, cols], preferred_element_type=F32)
        um = u[0:rows]
        u_before = jnp.where(prev_ok[0], u[rows + HALO - 1:rows + HALO], 0.0)
        u_after = jnp.where(next_ok[-1], u[rows + HALO:rows + HALO + 1], 0.0)
        up = jnp.where(row == 0, u_before, jnp.where(seam_cut_up, 0.0, pltpu.roll(um, 1, 0)))
        dn = jnp.where(row == rows - 1, u_after, jnp.where(seam_cut_dn, 0.0, pltpu.roll(um, rows - 1, 0)))
        cw = cw_ref[:, cols]
        return up * cw[0:1] + um * cw[1:2] + dn * cw[2:3] + cb_ref[:, cols]

    for c in range(d_ff // FF_CHUNK):
        a = up_conv(slice(c * FF_CHUNK, (c + 1) * FF_CHUNK))
        gate = up_conv(slice(d_ff + c * FF_CHUNK, d_ff + (c + 1) * FF_CHUNK))
        a_scr[:, c * FF_CHUNK:(c + 1) * FF_CHUNK] = (a * (gate * _sigmoid(gate))).astype(BF16)
    down = jnp.dot(a_scr[...], wdn_ref[0], preferred_element_type=F32)
    for s in range(FFN_TILES):
        y_ref[s * TM:(s + 1) * TM] = x[s * TM:(s + 1) * TM] + mods[s][5:6] * down[s * TM:(s + 1) * TM]


def _conv_ffn(x, mod, g, w_up, conv_w, conv_b, w_down, layer, *, n_lat):
    bsz, t_all, d = x.shape
    tiles_per_batch = t_all // TM
    assert bsz * tiles_per_batch % FFN_TILES == 0
    n_steps = bsz * tiles_per_batch // FFN_TILES
    d_ff = w_down.shape[1]
    rows = FFN_TILES * TM
    halos_per_step = rows // HALO
    last_halo = bsz * t_all // HALO - 1

    def mod_map(s):
        def index(j):
            tile = FFN_TILES * j + s
            return (tile // tiles_per_batch, (tile % tiles_per_batch) // n_lat, 0, 0)
        return index

    out = pl.pallas_call(
        functools.partial(_ffn_kernel, n_lat=n_lat, tiles_per_batch=tiles_per_batch),
        grid=(n_steps,),
        in_specs=([pl.BlockSpec((rows, d), lambda j: (j, 0)),
                   pl.BlockSpec((HALO, d), lambda j: (jnp.maximum(j * halos_per_step - 1, 0), 0)),
                   pl.BlockSpec((HALO, d), lambda j: (jnp.minimum((j + 1) * halos_per_step, last_halo), 0))]
                  + [pl.BlockSpec((1, 1, 6, d), mod_map(s)) for s in range(FFN_TILES)]
                  + [pl.BlockSpec((1, d), lambda j: (0, 0)),
                     _resident((1, d, 2 * d_ff), lambda j: (layer, 0, 0)),
                     pl.BlockSpec((CONV_W, 2 * d_ff), lambda j: (0, 0)),
                     pl.BlockSpec((1, 2 * d_ff), lambda j: (0, 0)),
                     _resident((1, d_ff, d), lambda j: (layer, 0, 0))]),
        out_specs=pl.BlockSpec((rows, d), lambda j: (j, 0)),
        out_shape=jax.ShapeDtypeStruct((bsz * t_all, d), F32),
        scratch_shapes=[pltpu.VMEM((rows + 2 * HALO, d), BF16), pltpu.VMEM((rows, d_ff), BF16)],
        compiler_params=_cparams(1),
        name="conv_ffn",
    )(*(x.reshape(bsz * t_all, d),) * 3, *(mod,) * FFN_TILES, g.reshape(1, d),
      w_up, conv_w, conv_b.reshape(1, 2 * d_ff), w_down)
    return out.reshape(bsz, t_all, d)


def _rope_tables(seq, n_ctx):
    t = jnp.arange(seq)
    row = (t // GRID_W).astype(F32)
    col = (t % GRID_W).astype(F32)
    n_axis = HEAD_DIM // 4
    inv = ROPE_THETA ** (-jnp.arange(n_axis, dtype=F32) / n_axis)
    ang = jnp.concatenate([row[:, None] * inv, col[:, None] * inv], axis=-1)
    cos = jnp.concatenate([jnp.cos(ang), jnp.ones((n_ctx, HEAD_DIM // 2), F32)], axis=0)
    sin = jnp.concatenate([jnp.sin(ang), jnp.zeros((n_ctx, HEAD_DIM // 2), F32)], axis=0)
    nt = (seq + n_ctx) // TM
    to_tiles = lambda a: a.reshape(nt, TM, HEAD_DIM // 2).transpose(0, 2, 1)
    return to_tiles(cos), to_tiles(sin)


def kernel(x, c, ctx, c_ctx, w_mod, b_mod, g_attn, g_ffn, na_w_qkv, na_g_q, na_g_k, na_rpb, na_w_o, swa_w_qkv, swa_g_q, swa_g_k, swa_sink, swa_w_o, ga_w_qkv, ga_g_q, ga_g_k, ga_w_o, ffn_w_up, ffn_conv_w, ffn_conv_b, ffn_w_down):
    bsz, seq, d = x.shape
    n_ctx = ctx.shape[1]
    depth = w_mod.shape[0]
    assert d == D_MODEL and seq % TM == 0 and n_ctx == TM and seq == GRID_W * GRID_W
    n_lat = seq // TM
    nt = n_lat + 1
    group = N_HEADS // N_KV_HEADS

    cvec = jnp.zeros((8, d), F32).at[:bsz].set(c).at[bsz].set(c_ctx)
    mods = _modulation(cvec, w_mod, b_mod)
    mod_lat = mods[:, :bsz].reshape(depth, bsz, 1, 6, d)
    mod_ctx = jnp.broadcast_to(mods[:, bsz].reshape(depth, 1, 1, 6, d), (depth, bsz, 1, 6, d))
    mod_all = jnp.concatenate([mod_lat, mod_ctx], axis=2)

    cos_t, sin_t = _rope_tables(seq, n_ctx)
    assert depth >= 2
    stream = (x, ctx)
    w_up_bf16, w_down_bf16 = ffn_w_up.astype(BF16), ffn_w_down.astype(BF16)

    for i in range(depth):
        last = i == depth - 1
        kind, j = i % N_MIXERS, i // N_MIXERS
        nt_out = n_lat if last else nt
        mod = mod_all[i]
        if kind == 0:
            w_in, w_out, gq, gk = na_w_qkv, na_w_o, na_g_q[j], na_g_k[j]
        elif kind == 1:
            w_in, w_out, gq, gk = swa_w_qkv, swa_w_o, swa_g_q[j], swa_g_k[j]
        else:
            w_in, w_out, gq, gk = ga_w_qkv, ga_w_o, ga_g_q[j], ga_g_k[j]
        q_t, k_tok, v_t = _qkv(stream, mod, g_attn[i], w_in, j, gq, gk, cos_t, sin_t, rope=kind != 0)
        if kind == 0:
            stream = _local_attention(q_t, k_tok, v_t, _na_bias_tables(na_rpb[j], n_lat),
                                      stream, mod, w_out, j, kind=0, nt_out=nt_out)
        elif kind == 1:
            sink = jnp.broadcast_to((swa_sink[j] * LOG2E).reshape(N_KV_HEADS, 1, group, 1),
                                    (N_KV_HEADS, 1, group, TM)).reshape(N_KV_HEADS, 1, group * TM)
            stream = _local_attention(q_t, k_tok, v_t, sink, stream, mod, w_out, j, kind=1, nt_out=nt_out)
        else:
            stream = _global_attention(q_t, k_tok, v_t, stream, mod, w_out, j, nt_out=nt_out)
        stream = _conv_ffn(stream, mod, g_ffn[i], w_up_bf16, ffn_conv_w[i], ffn_conv_b[i],
                           w_down_bf16, i, n_lat=n_lat)
    return stream
```

```python
import functools

import jax
import jax.numpy as jnp
from jax import lax
from jax.experimental import pallas as pl
from jax.experimental.pallas import tpu as pltpu

D_MODEL = 1024
GRID_W = 64
HEAD_DIM = 64
N_HEADS = D_MODEL // HEAD_DIM
N_KV_HEADS = 4
N_MIXERS = 3
NA_WIN_H = 8
NA_WIN_W = 16
SWA_RADIUS = 128
ROPE_THETA = 10000.0
CONV_W = 3
RMS_EPS = 1e-6
NEG_INF = -1e30

TM = 256
HALO = 8
FF_CHUNK = 256
FFN_TILES = 2
W_CHUNK = 256
MOD_BLOCK = 1536
HEAD_PAIR = 2 * HEAD_DIM
QUAD = 4
TILES_AHEAD = 1
ONES_ROWS = 16
LOG2E = 1.4426950408889634
VMEM_LIMIT = 56 * 1024 * 1024

BF16 = jnp.bfloat16
F32 = jnp.float32


def _cparams(n_axes):
    return pltpu.CompilerParams(dimension_semantics=("arbitrary",) * n_axes,
                                vmem_limit_bytes=VMEM_LIMIT)


def _resident(block_shape, index_map):
    return pl.BlockSpec(block_shape, index_map, pipeline_mode=pl.Buffered(1))


def _sigmoid(z):
    return 1.0 / (1.0 + jnp.exp(-z))


def _norm_mod(x, g, shift, scale):
    var = jnp.mean(x * x, axis=-1, keepdims=True)
    return (x * lax.rsqrt(var + RMS_EPS) * g) * (1.0 + scale) + shift


def _mod_kernel(c_ref, w_ref, b_ref, o_ref):
    c = c_ref[...]
    sc = (c * _sigmoid(c)).astype(BF16)
    o_ref[0] = jnp.dot(sc, w_ref[0].astype(BF16), preferred_element_type=F32) + b_ref[0]


def _modulation(cvec, w_mod, b_mod):
    depth, d, n = w_mod.shape
    bn = MOD_BLOCK
    return pl.pallas_call(
        _mod_kernel,
        grid=(depth, n // bn),
        in_specs=[pl.BlockSpec((8, d), lambda i, j: (0, 0)),
                  pl.BlockSpec((1, d, bn), lambda i, j: (i, 0, j)),
                  pl.BlockSpec((1, 1, bn), lambda i, j: (i, 0, j))],
        out_specs=pl.BlockSpec((1, 8, bn), lambda i, j: (i, 0, j)),
        out_shape=jax.ShapeDtypeStruct((depth, 8, n), F32),
        compiler_params=_cparams(2),
        name="modulation",
    )(cvec, w_mod, b_mod.reshape(depth, 1, n))


def _transpose_weight(w_ref, wt_scr):
    for c in range(wt_scr.shape[0] // W_CHUNK):
        cols = slice(c * W_CHUNK, (c + 1) * W_CHUNK)
        wt_scr[cols, :] = w_ref[0, :, cols].T.astype(BF16)


def _qkv_kernel(*refs, kvw, rope, tiles_per_batch, split_input):
    n_x = 4 if split_input else 1
    x_refs = refs[:n_x]
    (mod0_ref, mod1_ref, g_ref, w_ref, gq_ref, gk_ref, cos0_ref, sin0_ref, cos1_ref, sin1_ref,
     q_ref, k_ref, v_ref, wt_ref) = refs[n_x:]
    pl.when(pl.program_id(0) == 0)(lambda: _transpose_weight(w_ref, wt_ref))
    g = g_ref[...]
    if split_input:
        is_ctx = [(2 * pl.program_id(0) + s) % tiles_per_batch == tiles_per_batch - 1 for s in range(2)]
        tiles = [jnp.where(is_ctx[s], x_refs[2 + s][...], x_refs[s][...]) for s in range(2)]
    else:
        tiles = [x_refs[0][s * TM:(s + 1) * TM] for s in range(2)]
    mods = (mod0_ref[0, 0], mod1_ref[0, 0])
    h = jnp.concatenate(
        [_norm_mod(tiles[s], g, mods[s][0:1], mods[s][1:2]).astype(BF16) for s in range(2)], axis=0)
    def project(rows):
        return lax.dot_general(wt_ref[rows], h, (((1,), (1,)), ((), ())), preferred_element_type=F32)

    yq = project(slice(0, D_MODEL))
    yk = project(slice(D_MODEL, D_MODEL + kvw))
    yv = project(slice(D_MODEL + kvw, D_MODEL + 2 * kvw))

    def head_norm(y, gain, n_heads, cos_sin):
        y3 = y.reshape(n_heads, HEAD_DIM, TM)
        ms = jnp.mean(y3 * y3, axis=1, keepdims=True)
        y3 = y3 * lax.rsqrt(ms + RMS_EPS) * gain
        if rope:
            half = HEAD_DIM // 2
            x1, x2 = y3[:, :half], y3[:, half:]
            c, s = cos_sin
            y3 = jnp.concatenate([x1 * c - x2 * s, x2 * c + x1 * s], axis=1)
        return y3.reshape(n_heads * HEAD_DIM, TM)

    tables = ((cos0_ref[0], sin0_ref[0]), (cos1_ref[0], sin1_ref[0]))
    for s in range(2):
        q = head_norm(yq[:, s * TM:(s + 1) * TM], gq_ref[...], N_HEADS, tables[s])
        q_ref[s] = (q * (HEAD_DIM ** -0.5 * LOG2E)).astype(BF16)
    for s in range(2):
        k = head_norm(yk[:, s * TM:(s + 1) * TM], gk_ref[...], kvw // HEAD_DIM, tables[s])
        k_ref[s] = k.T.astype(BF16)
    for s in range(2):
        v_ref[s] = yv[:, s * TM:(s + 1) * TM].astype(BF16)


def _qkv(x, mod, g, w_qkv, layer, g_q, g_k, cos_t, sin_t, *, rope):
    split_input = isinstance(x, tuple)
    bsz, _, d = (x[0] if split_input else x).shape
    nt = (x[0].shape[1] + x[1].shape[1]) // TM if split_input else x.shape[1] // TM
    n_lat = nt - 1
    n_tiles = bsz * nt
    assert n_tiles % 2 == 0
    n = w_qkv.shape[2]
    kvw = (n - d) // 2
    gq = jnp.broadcast_to(g_q[:, None], (HEAD_DIM, TM))
    gk = jnp.broadcast_to(g_k[:, None], (HEAD_DIM, TM))
    half = HEAD_DIM // 2

    def mod_map(s):
        return lambda j: ((2 * j + s) // nt, ((2 * j + s) % nt) // (nt - 1), 0, 0)

    def rope_map(s):
        return lambda j: ((2 * j + s) % nt, 0, 0)

    if split_input:
        def lat_map(s):
            return lambda j: ((2 * j + s) // nt * n_lat + jnp.minimum((2 * j + s) % nt, n_lat - 1), 0)

        def ctx_map(s):
            return lambda j: ((2 * j + s) // nt, 0)

        x_args = [x[0].reshape(bsz * n_lat * TM, d)] * 2 + [x[1].reshape(bsz * TM, d)] * 2
        x_specs = [pl.BlockSpec((TM, d), f(s)) for f in (lat_map, ctx_map) for s in range(2)]
    else:
        x_args = [x.reshape(bsz * nt * TM, d)]
        x_specs = [pl.BlockSpec((2 * TM, d), lambda j: (j, 0))]

    q_t, k_tok, v_t = pl.pallas_call(
        functools.partial(_qkv_kernel, kvw=kvw, rope=rope, tiles_per_batch=nt, split_input=split_input),
        grid=(n_tiles // 2,),
        in_specs=x_specs + [
                  pl.BlockSpec((1, 1, 6, d), mod_map(0)),
                  pl.BlockSpec((1, 1, 6, d), mod_map(1)),
                  pl.BlockSpec((1, d), lambda j: (0, 0)),
                  _resident((1, d, n), lambda j: (layer, 0, 0)),
                  pl.BlockSpec((HEAD_DIM, TM), lambda j: (0, 0)),
                  pl.BlockSpec((HEAD_DIM, TM), lambda j: (0, 0)),
                  pl.BlockSpec((1, half, TM), rope_map(0)),
                  pl.BlockSpec((1, half, TM), rope_map(0)),
                  pl.BlockSpec((1, half, TM), rope_map(1)),
                  pl.BlockSpec((1, half, TM), rope_map(1))],
        out_specs=[pl.BlockSpec((2, d, TM), lambda j: (j, 0, 0)),
                   pl.BlockSpec((2, TM, kvw), lambda j: (j, 0, 0)),
                   pl.BlockSpec((2, kvw, TM), lambda j: (j, 0, 0))],
        out_shape=[jax.ShapeDtypeStruct((n_tiles, d, TM), BF16),
                   jax.ShapeDtypeStruct((n_tiles, TM, kvw), BF16),
                   jax.ShapeDtypeStruct((n_tiles, kvw, TM), BF16)],
        scratch_shapes=[pltpu.VMEM((n, d), BF16)],
        compiler_params=_cparams(1),
        name="qkv",
    )(*x_args, mod, mod, g.reshape(1, d), w_qkv, gq, gk, cos_t, sin_t, cos_t, sin_t)
    return (q_t.reshape(bsz, nt, d, TM), k_tok.reshape(bsz, nt, TM, kvw), v_t.reshape(bsz, nt, kvw, TM))


def _k_lanes(kv_head):
    g = kv_head // (HEAD_PAIR // HEAD_DIM)
    return slice(g * HEAD_PAIR, (g + 1) * HEAD_PAIR)


def _padded_queries(q_ref, first_head, n_q_heads, kv_head):
    qs = [q_ref[0, 0, (first_head + j) * HEAD_DIM:(first_head + j + 1) * HEAD_DIM, :]
          for j in range(n_q_heads)]
    qcat = qs[0] if n_q_heads == 1 else jnp.concatenate(qs, axis=1)
    slots = HEAD_PAIR // HEAD_DIM
    blocks = [jnp.zeros_like(qcat)] * slots
    blocks[kv_head % slots] = qcat
    return jnp.concatenate(blocks, axis=0)


def _store_heads(o_ref, o, first_head, n_q_heads):
    for j in range(n_q_heads):
        o_ref[0, 0, (first_head + j) * HEAD_DIM:(first_head + j + 1) * HEAD_DIM, :] = (
            o[:, j * TM:(j + 1) * TM].astype(o_ref.dtype))


def _first_grid_step():
    return functools.reduce(jnp.logical_and, [pl.program_id(a) == 0 for a in range(2)])


def _project_out(o_scr, x_tile, mod_ref, wot_scr, y_ref):
    out_t = jnp.dot(wot_scr[...], o_scr[0, 0], preferred_element_type=F32)
    y_ref[0] = x_tile + mod_ref[0, 0][2:3] * out_t.T


def _values_and_ones(v_rows):
    return jnp.concatenate([v_rows, jnp.ones((ONES_ROWS, v_rows.shape[1]), v_rows.dtype)], axis=0)


def _local_attn_kernel(*refs, kind, group, n_lat, split_input):
    q_ref = refs[0]
    k_refs = refs[1:5]
    v_refs = refs[5:9]
    extra_ref = refs[9]
    x_refs = refs[10:12] if split_input else refs[10:11]
    mod_ref, wo_ref, y_ref, o_ref, wot_scr = refs[10 + len(x_refs):]
    pl.when(_first_grid_step())(lambda: _transpose_weight(wo_ref, wot_scr))
    t = pl.program_id(1)
    is_ctx = t >= n_lat
    x_tile = jnp.where(is_ctx, x_refs[1][0], x_refs[0][0]) if split_input else x_refs[0][0]
    pen = [None, jnp.where(t >= 1, 0.0, NEG_INF).astype(F32), jnp.where(t <= n_lat - 2, 0.0, NEG_INF).astype(F32)]

    if kind == 1:
        rows = [(0, TM), (TM - SWA_RADIUS, SWA_RADIUS), (0, SWA_RADIUS), (0, TM)]
        band = []
        for (r0, nr), off in zip(rows[:3], (0, -1, 1)):
            key_i = lax.broadcasted_iota(jnp.int32, (nr, TM), 0) + (r0 + off * TM)
            qry_j = lax.broadcasted_iota(jnp.int32, (nr, TM), 1)
            band.append(jnp.where(jnp.abs(key_i - qry_j) <= SWA_RADIUS, 0.0, NEG_INF).astype(F32))
    else:
        rows = [(0, TM)] * 4
        tab = [jnp.where(t == 0, 0, jnp.where(t >= n_lat - 1, 6, 3)),
               jnp.where(t >= n_lat - 1, 5, 2),
               jnp.where(t == 0, 1, 4)]

    quad = QUAD
    width = quad * TM
    own_kv = group == 1

    def quad_queries(g):
        if not own_kv:
            return _padded_queries(q_ref, g * quad, quad, g)
        zeros = jnp.zeros((HEAD_DIM, TM), BF16)
        return jnp.concatenate(
            [jnp.concatenate([q_ref[0, 0, (g * quad + h) * HEAD_DIM:(g * quad + h + 1) * HEAD_DIM, :]
                              if j == h else zeros for j in range(quad)], axis=1)
             for h in range(quad)], axis=0)

    queries = [quad_queries(g) for g in range(N_HEADS // quad)]

    def scores(g, c):
        r0, nr = rows[c]
        lanes = slice(g * quad * HEAD_DIM, (g + 1) * quad * HEAD_DIM) if own_kv else _k_lanes(g)
        s = jnp.dot(k_refs[c][0, 0, r0:r0 + nr, lanes], queries[g], preferred_element_type=F32)
        if c < 3:
            if kind == 1:
                s = s + jnp.concatenate([band[c] if pen[c] is None else band[c] + pen[c]] * quad, axis=1)
            else:
                s = jnp.concatenate([s[:, h * TM:(h + 1) * TM] + extra_ref[g * quad + h, tab[c]]
                                     for h in range(quad)], axis=1)
                if pen[c] is not None:
                    s = s + pen[c]
        return s

    def values(g, h, c):
        r0, nr = rows[c]
        n = g * quad + h if own_kv else g
        return _values_and_ones(v_refs[c][0, 0, n * HEAD_DIM:(n + 1) * HEAD_DIM, r0:r0 + nr])

    def attend(slots):
        order = [(g, c) for g in range(N_HEADS // quad) for c in slots]
        s_next = scores(*order[0])
        for i, (g, c) in enumerate(order):
            s = s_next
            if i + 1 < len(order):
                s_next = scores(*order[i + 1])
            if c == slots[0]:
                m = jnp.full((1, width), NEG_INF, F32)
                acc = jnp.zeros((HEAD_DIM + ONES_ROWS, width), F32)
            m_new = jnp.maximum(m, jnp.max(s, axis=0, keepdims=True))
            p = jnp.exp2(s - m_new).astype(BF16)
            if own_kv:
                pv = jnp.concatenate([jnp.dot(values(g, h, c), p[:, h * TM:(h + 1) * TM],
                                              preferred_element_type=F32) for h in range(quad)], axis=1)
            else:
                pv = jnp.dot(values(g, 0, c), p, preferred_element_type=F32)
            acc = jnp.exp2(m - m_new) * acc + pv
            m = m_new
            if c == slots[-1]:
                l = acc[HEAD_DIM:HEAD_DIM + 1]
                if kind == 1:
                    sink = extra_ref[g]
                    l = l + jnp.exp2(sink - m)
                _store_heads(o_ref, acc[:HEAD_DIM] / l, g * quad, quad)

    pl.when(jnp.logical_not(is_ctx))(lambda: attend((3, 0, 1, 2)))
    pl.when(is_ctx)(lambda: attend((3,)))
    _project_out(o_ref, x_tile, mod_ref, wot_scr, y_ref)


def _local_attention(q_t, k_tok, v_t, extra, x, mod, w_o, layer, *, kind, nt_out):
    bsz, nt, d, _ = q_t.shape
    n_lat = nt - 1
    kvw = k_tok.shape[-1]
    ctx_idx = nt - 1
    split_input = isinstance(x, tuple)
    if split_input:
        x_args = list(x)
        x_specs = [pl.BlockSpec((1, TM, d), lambda b, t: (b, jnp.minimum(t, n_lat - 1), 0)),
                   pl.BlockSpec((1, TM, d), lambda b, t: (b, 0, 0))]
    else:
        x_args = [x]
        x_specs = [pl.BlockSpec((1, TM, d), lambda b, t: (b, t, 0))]
    kern = functools.partial(_local_attn_kernel, kind=kind, group=N_HEADS * HEAD_DIM // kvw, n_lat=n_lat,
                             split_input=split_input)
    chunk_maps = [lambda b, t: (b, t, 0, 0),
                  lambda b, t: (b, jnp.maximum(t - 1, 0), 0, 0),
                  lambda b, t: (b, jnp.minimum(t + 1, n_lat - 1), 0, 0),
                  lambda b, t: (b, ctx_idx, 0, 0)]
    return pl.pallas_call(
        kern,
        grid=(bsz, nt_out),
        in_specs=([pl.BlockSpec((1, 1, d, TM), lambda b, t: (b, t, 0, 0))]
                  + [pl.BlockSpec((1, 1, TM, kvw), f) for f in chunk_maps]
                  + [pl.BlockSpec((1, 1, kvw, TM), f) for f in chunk_maps]
                  + [_resident(extra.shape, lambda b, t: (0,) * extra.ndim)]
                  + x_specs
                  + [pl.BlockSpec((1, 1, 6, d), lambda b, t: (b, t // n_lat, 0, 0)),
                     _resident((1, d, d), lambda b, t: (layer, 0, 0))]),
        out_specs=pl.BlockSpec((1, TM, d), lambda b, t: (b, t, 0)),
        out_shape=jax.ShapeDtypeStruct((bsz, nt_out * TM, d), F32),
        scratch_shapes=[pltpu.VMEM((1, 1, d, TM), BF16), pltpu.VMEM((d, d), BF16)],
        compiler_params=_cparams(2),
        name="na_attention" if kind == 0 else "swa_attention",
    )(q_t, k_tok, k_tok, k_tok, k_tok, v_t, v_t, v_t, v_t, extra, *x_args, mod, w_o)


def _na_bias_kernel(rpb_ref, o_ref, col_scr, *, combos, grid_rows):
    h = pl.program_id(0)
    n_dr, n_dc = 2 * NA_WIN_H - 1, 2 * NA_WIN_W - 1
    kc = lax.broadcasted_iota(jnp.int32, (GRID_W, 2 * GRID_W), 0)
    lane = lax.broadcasted_iota(jnp.int32, (GRID_W, 2 * GRID_W), 1)
    qc = lane & (GRID_W - 1)
    dc = kc - qc + (NA_WIN_W - 1)
    cs = jnp.clip(qc - NA_WIN_W // 2, 0, GRID_W - NA_WIN_W)
    col_ok = jnp.logical_and(kc >= cs, kc < cs + NA_WIN_W)

    def fill(a, carry):
        base = (h * n_dr + a) * n_dc
        g = jnp.full(kc.shape, rpb_ref[base], F32)
        for b in range(1, n_dc):
            g = jnp.where(dc == b, rpb_ref[base + b], g)
        col_scr[a] = jnp.where(col_ok, g * LOG2E, NEG_INF)
        return carry

    lax.fori_loop(0, n_dr, fill, 0)

    rows_per_tile = TM // GRID_W
    left = lane < GRID_W
    masked = jnp.full(kc.shape, NEG_INF, F32)
    for ci, (tile, off) in enumerate(combos):
        for kr in range(rows_per_tile):
            k_row = (tile + off) * rows_per_tile + kr
            for j in range(rows_per_tile // 2):
                halves = []
                for qr in (2 * j, 2 * j + 1):
                    q_row = tile * rows_per_tile + qr
                    rs = min(max(q_row - NA_WIN_H // 2, 0), grid_rows - NA_WIN_H)
                    ok = rs <= k_row < rs + NA_WIN_H
                    halves.append(col_scr[k_row - q_row + NA_WIN_H - 1] if ok else masked)
                o_ref[0, ci, kr * GRID_W:(kr + 1) * GRID_W, j * 2 * GRID_W:(j + 1) * 2 * GRID_W] = (
                    jnp.where(left, halves[0], halves[1]))


def _na_bias_tables(rpb, n_lat):
    n_heads = rpb.shape[0]
    combos = ((0, 0), (0, 1), (1, -1), (1, 0), (1, 1), (n_lat - 1, -1), (n_lat - 1, 0))
    return pl.pallas_call(
        functools.partial(_na_bias_kernel, combos=combos, grid_rows=n_lat * TM // GRID_W),
        grid=(n_heads,),
        in_specs=[pl.BlockSpec(memory_space=pltpu.SMEM)],
        out_specs=pl.BlockSpec((1, len(combos), TM, TM), lambda h: (h, 0, 0, 0)),
        out_shape=jax.ShapeDtypeStruct((n_heads, len(combos), TM, TM), F32),
        scratch_shapes=[pltpu.VMEM((2 * NA_WIN_H - 1, GRID_W, 2 * GRID_W), F32)],
        compiler_params=_cparams(1),
        name="na_bias",
    )(rpb.reshape(-1))


def _global_attn_kernel(q_ref, k_ref, v_ref, x_ref, mod_ref, wo_ref, y_ref, o_ref, wot_scr, s_scr, *, n_lat):
    pl.when(_first_grid_step())(lambda: _transpose_weight(wo_ref, wot_scr))
    t = pl.program_id(1)
    group = N_HEADS // N_KV_HEADS
    width = group * TM
    n_buf = TILES_AHEAD + 1
    queries = [_padded_queries(q_ref, n * group, group, n) for n in range(N_KV_HEADS)]

    def attend(first_tile):
        units = [(n, c) for n in range(N_KV_HEADS) for c in range(first_tile, n_lat + 1)]

        def scores(i):
            n, c = units[i]
            s = jnp.dot(k_ref[0, c, :, _k_lanes(n)], queries[n], preferred_element_type=F32)
            s_scr[i % n_buf] = s
            return jnp.max(s, axis=0, keepdims=True)

        tile_max = {i: scores(i) for i in range(min(TILES_AHEAD, len(units)))}
        for i, (n, c) in enumerate(units):
            if i + TILES_AHEAD < len(units):
                tile_max[i + TILES_AHEAD] = scores(i + TILES_AHEAD)
            if c == first_tile:
                m = jnp.full((1, width), NEG_INF, F32)
                acc = jnp.zeros((HEAD_DIM + ONES_ROWS, width), F32)
            m_new = jnp.maximum(m, tile_max.pop(i))
            p = jnp.exp2(s_scr[i % n_buf] - m_new).astype(BF16)
            pv = jnp.dot(_values_and_ones(v_ref[0, c, n * HEAD_DIM:(n + 1) * HEAD_DIM, :]), p,
                         preferred_element_type=F32)
            acc = jnp.exp2(m - m_new) * acc + pv
            m = m_new
            if c == n_lat:
                _store_heads(o_ref, acc[:HEAD_DIM] / acc[HEAD_DIM:HEAD_DIM + 1], n * group, group)

    pl.when(t < n_lat)(lambda: attend(0))
    pl.when(t >= n_lat)(lambda: attend(n_lat))
    _project_out(o_ref, x_ref[0], mod_ref, wot_scr, y_ref)


def _global_attention(q_t, k_tok, v_t, x, mod, w_o, layer, *, nt_out):
    bsz, nt, d, _ = q_t.shape
    kvw = k_tok.shape[-1]
    n_lat = nt - 1
    assert n_lat % 2 == 0 and kvw == N_KV_HEADS * HEAD_DIM
    return pl.pallas_call(
        functools.partial(_global_attn_kernel, n_lat=n_lat),
        grid=(bsz, nt_out),
        in_specs=[pl.BlockSpec((1, 1, d, TM), lambda b, t: (b, t, 0, 0)),
                  pl.BlockSpec((1, nt, TM, kvw), lambda b, t: (b, 0, 0, 0)),
                  pl.BlockSpec((1, nt, kvw, TM), lambda b, t: (b, 0, 0, 0)),
                  pl.BlockSpec((1, TM, d), lambda b, t: (b, t, 0)),
                  pl.BlockSpec((1, 1, 6, d), lambda b, t: (b, t // n_lat, 0, 0)),
                  _resident((1, d, d), lambda b, t: (layer, 0, 0))],
        out_specs=pl.BlockSpec((1, TM, d), lambda b, t: (b, t, 0)),
        out_shape=jax.ShapeDtypeStruct((bsz, nt_out * TM, d), F32),
        scratch_shapes=[pltpu.VMEM((1, 1, d, TM), BF16), pltpu.VMEM((d, d), BF16),
                        pltpu.VMEM((TILES_AHEAD + 1, TM, N_HEADS // N_KV_HEADS * TM), F32)],
        compiler_params=_cparams(2),
        name="global_attention",
    )(q_t, k_tok, v_t, x, mod, w_o)


def _ffn_kernel(*refs, n_lat, tiles_per_batch):
    x_ref, xp_ref, xn_ref = refs[:3]
    mod_refs = refs[3:3 + FFN_TILES]
    g_ref, wup_ref, cw_ref, cb_ref, wdn_ref, y_ref, h_scr, a_scr = refs[3 + FFN_TILES:]
    rows = FFN_TILES * TM
    j = pl.program_id(0)
    g = g_ref[...]
    mods = [ref[0, 0] for ref in mod_refs]
    x = x_ref[...]
    for s in range(FFN_TILES):
        h_scr[s * TM:(s + 1) * TM] = _norm_mod(
            x[s * TM:(s + 1) * TM], g, mods[s][3:4], mods[s][4:5]).astype(BF16)
    h_scr[rows:rows + 2 * HALO] = jnp.concatenate(
        [_norm_mod(xp_ref[...], g, mods[0][3:4], mods[0][4:5]),
         _norm_mod(xn_ref[...], g, mods[-1][3:4], mods[-1][4:5])], axis=0).astype(BF16)

    tile = [(FFN_TILES * j + s) % tiles_per_batch for s in range(FFN_TILES)]
    prev_ok = [jnp.logical_and(t >= 1, t < n_lat) for t in tile]
    next_ok = [t <= n_lat - 2 for t in tile]
    row = lax.broadcasted_iota(jnp.int32, (rows, FF_CHUNK), 0)
    seam_cut_up = functools.reduce(jnp.logical_or, [
        jnp.logical_and(row == s * TM, jnp.logical_not(prev_ok[s])) for s in range(1, FFN_TILES)])
    seam_cut_dn = functools.reduce(jnp.logical_or, [
        jnp.logical_and(row == s * TM - 1, jnp.logical_not(next_ok[s - 1])) for s in range(1, FFN_TILES)])
    h = h_scr[...]
    d_ff = wdn_ref.shape[1]

    def up_conv(cols):
        u = jnp.dot(h, wup_ref[0, :, cols], preferred_element_type=F32)
        um = u[0:rows]
        u_before = jnp.where(prev_ok[0], u[rows + HALO - 1:rows + HALO], 0.0)
        u_after = jnp.where(next_ok[-1], u[rows + HALO:rows + HALO + 1], 0.0)
        up = jnp.where(row == 0, u_before, jnp.where(seam_cut_up, 0.0, pltpu.roll(um, 1, 0)))
        dn = jnp.where(row == rows - 1, u_after, jnp.where(seam_cut_dn, 0.0, pltpu.roll(um, rows - 1, 0)))
        cw = cw_ref[:, cols]
        return up * cw[0:1] + um * cw[1:2] + dn * cw[2:3] + cb_ref[:, cols]

    for c in range(d_ff // FF_CHUNK):
        a = up_conv(slice(c * FF_CHUNK, (c + 1) * FF_CHUNK))
        gate = up_conv(slice(d_ff + c * FF_CHUNK, d_ff + (c + 1) * FF_CHUNK))
        a_scr[:, c * FF_CHUNK:(c + 1) * FF_CHUNK] = (a * (gate * _sigmoid(gate))).astype(BF16)
    down = jnp.dot(a_scr[...], wdn_ref[0], preferred_element_type=F32)
    for s in range(FFN_TILES):
        y_ref[s * TM:(s + 1) * TM] = x[s * TM:(s + 1) * TM] + mods[s][5:6] * down[s * TM:(s + 1) * TM]


def _conv_ffn(x, mod, g, w_up, conv_w, conv_b, w_down, layer, *, n_lat):
    bsz, t_all, d = x.shape
    tiles_per_batch = t_all // TM
    assert bsz * tiles_per_batch % FFN_TILES == 0
    n_steps = bsz * tiles_per_batch // FFN_TILES
    d_ff = w_down.shape[1]
    rows = FFN_TILES * TM
    halos_per_step = rows // HALO
    last_halo = bsz * t_all // HALO - 1

    def mod_map(s):
        def index(j):
            tile = FFN_TILES * j + s
            return (tile // tiles_per_batch, (tile % tiles_per_batch) // n_lat, 0, 0)
        return index

    out = pl.pallas_call(
        functools.partial(_ffn_kernel, n_lat=n_lat, tiles_per_batch=tiles_per_batch),
        grid=(n_steps,),
        in_specs=([pl.BlockSpec((rows, d), lambda j: (j, 0)),
                   pl.BlockSpec((HALO, d), lambda j: (jnp.maximum(j * halos_per_step - 1, 0), 0)),
                   pl.BlockSpec((HALO, d), lambda j: (jnp.minimum((j + 1) * halos_per_step, last_halo), 0))]
                  + [pl.BlockSpec((1, 1, 6, d), mod_map(s)) for s in range(FFN_TILES)]
                  + [pl.BlockSpec((1, d), lambda j: (0, 0)),
                     _resident((1, d, 2 * d_ff), lambda j: (layer, 0, 0)),
                     pl.BlockSpec((CONV_W, 2 * d_ff), lambda j: (0, 0)),
                     pl.BlockSpec((1, 2 * d_ff), lambda j: (0, 0)),
                     _resident((1, d_ff, d), lambda j: (layer, 0, 0))]),
        out_specs=pl.BlockSpec((rows, d), lambda j: (j, 0)),
        out_shape=jax.ShapeDtypeStruct((bsz * t_all, d), F32),
        scratch_shapes=[pltpu.VMEM((rows + 2 * HALO, d), BF16), pltpu.VMEM((rows, d_ff), BF16)],
        compiler_params=_cparams(1),
        name="conv_ffn",
    )(*(x.reshape(bsz * t_all, d),) * 3, *(mod,) * FFN_TILES, g.reshape(1, d),
      w_up, conv_w, conv_b.reshape(1, 2 * d_ff), w_down)
    return out.reshape(bsz, t_all, d)


def _rope_tables(seq, n_ctx):
    t = jnp.arange(seq)
    row = (t // GRID_W).astype(F32)
    col = (t % GRID_W).astype(F32)
    n_axis = HEAD_DIM // 4
    inv = ROPE_THETA ** (-jnp.arange(n_axis, dtype=F32) / n_axis)
    ang = jnp.concatenate([row[:, None] * inv, col[:, None] * inv], axis=-1)
    cos = jnp.concatenate([jnp.cos(ang), jnp.ones((n_ctx, HEAD_DIM // 2), F32)], axis=0)
    sin = jnp.concatenate([jnp.sin(ang), jnp.zeros((n_ctx, HEAD_DIM // 2), F32)], axis=0)
    nt = (seq + n_ctx) // TM
    to_tiles = lambda a: a.reshape(nt, TM, HEAD_DIM // 2).transpose(0, 2, 1)
    return to_tiles(cos), to_tiles(sin)


def kernel(x, c, ctx, c_ctx, w_mod, b_mod, g_attn, g_ffn, na_w_qkv, na_g_q, na_g_k, na_rpb, na_w_o, swa_w_qkv, swa_g_q, swa_g_k, swa_sink, swa_w_o, ga_w_qkv, ga_g_q, ga_g_k, ga_w_o, ffn_w_up, ffn_conv_w, ffn_conv_b, ffn_w_down):
    bsz, seq, d = x.shape
    n_ctx = ctx.shape[1]
    depth = w_mod.shape[0]
    assert d == D_MODEL and seq % TM == 0 and n_ctx == TM and seq == GRID_W * GRID_W
    n_lat = seq // TM
    nt = n_lat + 1
    group = N_HEADS // N_KV_HEADS

    cvec = jnp.zeros((8, d), F32).at[:bsz].set(c).at[bsz].set(c_ctx)
    mods = _modulation(cvec, w_mod, b_mod)
    mod_lat = mods[:, :bsz].reshape(depth, bsz, 1, 6, d)
    mod_ctx = jnp.broadcast_to(mods[:, bsz].reshape(depth, 1, 1, 6, d), (depth, bsz, 1, 6, d))
    mod_all = jnp.concatenate([mod_lat, mod_ctx], axis=2)

    cos_t, sin_t = _rope_tables(seq, n_ctx)
    assert depth >= 2
    stream = (x, ctx)
    w_up_bf16, w_down_bf16 = ffn_w_up.astype(BF16), ffn_w_down.astype(BF16)

    for i in range(depth):
        last = i == depth - 1
        kind, j = i % N_MIXERS, i // N_MIXERS
        nt_out = n_lat if last else nt
        mod = mod_all[i]
        if kind == 0:
            w_in, w_out, gq, gk = na_w_qkv, na_w_o, na_g_q[j], na_g_k[j]
        elif kind == 1:
            w_in, w_out, gq, gk = swa_w_qkv, swa_w_o, swa_g_q[j], swa_g_k[j]
        else:
            w_in, w_out, gq, gk = ga_w_qkv, ga_w_o, ga_g_q[j], ga_g_k[j]
        q_t, k_tok, v_t = _qkv(stream, mod, g_attn[i], w_in, j, gq, gk, cos_t, sin_t, rope=kind != 0)
        if kind == 0:
            stream = _local_attention(q_t, k_tok, v_t, _na_bias_tables(na_rpb[j], n_lat),
                                      stream, mod, w_out, j, kind=0, nt_out=nt_out)
        elif kind == 1:
            sink = jnp.broadcast_to((swa_sink[j] * LOG2E).reshape(N_KV_HEADS, 1, group, 1),
                                    (N_KV_HEADS, 1, group, TM)).reshape(N_KV_HEADS, 1, group * TM)
            stream = _local_attention(q_t, k_tok, v_t, sink, stream, mod, w_out, j, kind=1, nt_out=nt_out)
        else:
            stream = _global_attention(q_t, k_tok, v_t, stream, mod, w_out, j, nt_out=nt_out)
        stream = _conv_ffn(stream, mod, g_ffn[i], w_up_bf16, ffn_conv_w[i], ffn_conv_b[i],
                           w_down_bf16, i, n_lat=n_lat)
    return stream
```

```python
import functools

import jax
import jax.numpy as jnp
from jax import lax
from jax.experimental import pallas as pl
from jax.experimental.pallas import tpu as pltpu

D_MODEL = 1024
GRID_W = 64
HEAD_DIM = 64
N_HEADS = D_MODEL // HEAD_DIM
N_KV_HEADS = 4
N_MIXERS = 3
NA_WIN_H = 8
NA_WIN_W = 16
SWA_RADIUS = 128
ROPE_THETA = 10000.0
CONV_W = 3
RMS_EPS = 1e-6
NEG_INF = -1e30

TM = 256
HALO = 8
FF_CHUNK = 256
FFN_TILES = 2
W_CHUNK = 256
MOD_BLOCK = 1536
HEAD_PAIR = 2 * HEAD_DIM
QUAD = 4
TILES_AHEAD = 1
ONES_ROWS = 16
LOG2E = 1.4426950408889634
VMEM_LIMIT = 56 * 1024 * 1024

BF16 = jnp.bfloat16
F32 = jnp.float32


def _cparams(n_axes):
    return pltpu.CompilerParams(dimension_semantics=("arbitrary",) * n_axes,
                                vmem_limit_bytes=VMEM_LIMIT)


def _resident(block_shape, index_map):
    return pl.BlockSpec(block_shape, index_map, pipeline_mode=pl.Buffered(1))


def _sigmoid(z):
    return 1.0 / (1.0 + jnp.exp(-z))


def _norm_mod(x, g, shift, scale):
    var = jnp.mean(x * x, axis=-1, keepdims=True)
    return (x * lax.rsqrt(var + RMS_EPS) * g) * (1.0 + scale) + shift


def _mod_kernel(c_ref, w_ref, b_ref, o_ref):
    c = c_ref[...]
    sc = (c * _sigmoid(c)).astype(BF16)
    o_ref[0] = jnp.dot(sc, w_ref[0].astype(BF16), preferred_element_type=F32) + b_ref[0]


def _modulation(cvec, w_mod, b_mod):
    depth, d, n = w_mod.shape
    bn = MOD_BLOCK
    return pl.pallas_call(
        _mod_kernel,
        grid=(depth, n // bn),
        in_specs=[pl.BlockSpec((8, d), lambda i, j: (0, 0)),
                  pl.BlockSpec((1, d, bn), lambda i, j: (i, 0, j)),
                  pl.BlockSpec((1, 1, bn), lambda i, j: (i, 0, j))],
        out_specs=pl.BlockSpec((1, 8, bn), lambda i, j: (i, 0, j)),
        out_shape=jax.ShapeDtypeStruct((depth, 8, n), F32),
        compiler_params=_cparams(2),
        name="modulation",
    )(cvec, w_mod, b_mod.reshape(depth, 1, n))


def _transpose_weight(w_ref, wt_scr):
    for c in range(wt_scr.shape[0] // W_CHUNK):
        cols = slice(c * W_CHUNK, (c + 1) * W_CHUNK)
        wt_scr[cols, :] = w_ref[0, :, cols].T.astype(BF16)


def _qkv_kernel(*refs, kvw, rope, tiles_per_batch, split_input):
    n_x = 4 if split_input else 1
    x_refs = refs[:n_x]
    (mod0_ref, mod1_ref, g_ref, w_ref, gq_ref, gk_ref, cos0_ref, sin0_ref, cos1_ref, sin1_ref,
     q_ref, k_ref, v_ref, wt_ref) = refs[n_x:]
    pl.when(pl.program_id(0) == 0)(lambda: _transpose_weight(w_ref, wt_ref))
    g = g_ref[...]
    if split_input:
        is_ctx = [(2 * pl.program_id(0) + s) % tiles_per_batch == tiles_per_batch - 1 for s in range(2)]
        tiles = [jnp.where(is_ctx[s], x_refs[2 + s][...], x_refs[s][...]) for s in range(2)]
    else:
        tiles = [x_refs[0][s * TM:(s + 1) * TM] for s in range(2)]
    mods = (mod0_ref[0, 0], mod1_ref[0, 0])
    h = jnp.concatenate(
        [_norm_mod(tiles[s], g, mods[s][0:1], mods[s][1:2]).astype(BF16) for s in range(2)], axis=0)
    def project(rows):
        return lax.dot_general(wt_ref[rows], h, (((1,), (1,)), ((), ())), preferred_element_type=F32)

    yq = project(slice(0, D_MODEL))
    yk = project(slice(D_MODEL, D_MODEL + kvw))
    yv = project(slice(D_MODEL + kvw, D_MODEL + 2 * kvw))

    def head_norm(y, gain, n_heads, cos_sin):
        y3 = y.reshape(n_heads, HEAD_DIM, TM)
        ms = jnp.mean(y3 * y3, axis=1, keepdims=True)
        y3 = y3 * lax.rsqrt(ms + RMS_EPS) * gain
        if rope:
            half = HEAD_DIM // 2
            x1, x2 = y3[:, :half], y3[:, half:]
            c, s = cos_sin
            y3 = jnp.concatenate([x1 * c - x2 * s, x2 * c + x1 * s], axis=1)
        return y3.reshape(n_heads * HEAD_DIM, TM)

    tables = ((cos0_ref[0], sin0_ref[0]), (cos1_ref[0], sin1_ref[0]))
    for s in range(2):
        q = head_norm(yq[:, s * TM:(s + 1) * TM], gq_ref[...], N_HEADS, tables[s])
        q_ref[s] = (q * (HEAD_DIM ** -0.5 * LOG2E)).astype(BF16)
    for s in range(2):
        k = head_norm(yk[:, s * TM:(s + 1) * TM], gk_ref[...], kvw // HEAD_DIM, tables[s])
        k_ref[s] = k.T.astype(BF16)
    for s in range(2):
        v_ref[s] = yv[:, s * TM:(s + 1) * TM].astype(BF16)


def _qkv(x, mod, g, w_qkv, layer, g_q, g_k, cos_t, sin_t, *, rope):
    split_input = isinstance(x, tuple)
    bsz, _, d = (x[0] if split_input else x).shape
    nt = (x[0].shape[1] + x[1].shape[1]) // TM if split_input else x.shape[1] // TM
    n_lat = nt - 1
    n_tiles = bsz * nt
    assert n_tiles % 2 == 0
    n = w_qkv.shape[2]
    kvw = (n - d) // 2
    gq = jnp.broadcast_to(g_q[:, None], (HEAD_DIM, TM))
    gk = jnp.broadcast_to(g_k[:, None], (HEAD_DIM, TM))
    half = HEAD_DIM // 2

    def mod_map(s):
        return lambda j: ((2 * j + s) // nt, ((2 * j + s) % nt) // (nt - 1), 0, 0)

    def rope_map(s):
        return lambda j: ((2 * j + s) % nt, 0, 0)

    if split_input:
        def lat_map(s):
            return lambda j: ((2 * j + s) // nt * n_lat + jnp.minimum((2 * j + s) % nt, n_lat - 1), 0)

        def ctx_map(s):
            return lambda j: ((2 * j + s) // nt, 0)

        x_args = [x[0].reshape(bsz * n_lat * TM, d)] * 2 + [x[1].reshape(bsz * TM, d)] * 2
        x_specs = [pl.BlockSpec((TM, d), f(s)) for f in (lat_map, ctx_map) for s in range(2)]
    else:
        x_args = [x.reshape(bsz * nt * TM, d)]
        x_specs = [pl.BlockSpec((2 * TM, d), lambda j: (j, 0))]

    q_t, k_tok, v_t = pl.pallas_call(
        functools.partial(_qkv_kernel, kvw=kvw, rope=rope, tiles_per_batch=nt, split_input=split_input),
        grid=(n_tiles // 2,),
        in_specs=x_specs + [
                  pl.BlockSpec((1, 1, 6, d), mod_map(0)),
                  pl.BlockSpec((1, 1, 6, d), mod_map(1)),
                  pl.BlockSpec((1, d), lambda j: (0, 0)),
                  _resident((1, d, n), lambda j: (layer, 0, 0)),
                  pl.BlockSpec((HEAD_DIM, TM), lambda j: (0, 0)),
                  pl.BlockSpec((HEAD_DIM, TM), lambda j: (0, 0)),
                  pl.BlockSpec((1, half, TM), rope_map(0)),
                  pl.BlockSpec((1, half, TM), rope_map(0)),
                  pl.BlockSpec((1, half, TM), rope_map(1)),
                  pl.BlockSpec((1, half, TM), rope_map(1))],
        out_specs=[pl.BlockSpec((2, d, TM), lambda j: (j, 0, 0)),
                   pl.BlockSpec((2, TM, kvw), lambda j: (j, 0, 0)),
                   pl.BlockSpec((2, kvw, TM), lambda j: (j, 0, 0))],
        out_shape=[jax.ShapeDtypeStruct((n_tiles, d, TM), BF16),
                   jax.ShapeDtypeStruct((n_tiles, TM, kvw), BF16),
                   jax.ShapeDtypeStruct((n_tiles, kvw, TM), BF16)],
        scratch_shapes=[pltpu.VMEM((n, d), BF16)],
        compiler_params=_cparams(1),
        name="qkv",
    )(*x_args, mod, mod, g.reshape(1, d), w_qkv, gq, gk, cos_t, sin_t, cos_t, sin_t)
    return (q_t.reshape(bsz, nt, d, TM), k_tok.reshape(bsz, nt, TM, kvw), v_t.reshape(bsz, nt, kvw, TM))


def _k_lanes(kv_head):
    g = kv_head // (HEAD_PAIR // HEAD_DIM)
    return slice(g * HEAD_PAIR, (g + 1) * HEAD_PAIR)


def _padded_queries(q_ref, first_head, n_q_heads, kv_head):
    qs = [q_ref[0, 0, (first_head + j) * HEAD_DIM:(first_head + j + 1) * HEAD_DIM, :]
          for j in range(n_q_heads)]
    qcat = qs[0] if n_q_heads == 1 else jnp.concatenate(qs, axis=1)
    slots = HEAD_PAIR // HEAD_DIM
    blocks = [jnp.zeros_like(qcat)] * slots
    blocks[kv_head % slots] = qcat
    return jnp.concatenate(blocks, axis=0)


def _store_heads(o_ref, o, first_head, n_q_heads):
    for j in range(n_q_heads):
        o_ref[0, 0, (first_head + j) * HEAD_DIM:(first_head + j + 1) * HEAD_DIM, :] = (
            o[:, j * TM:(j + 1) * TM].astype(o_ref.dtype))


def _first_grid_step():
    return functools.reduce(jnp.logical_and, [pl.program_id(a) == 0 for a in range(2)])


def _project_out(o_scr, x_tile, mod_ref, wot_scr, y_ref):
    out_t = jnp.dot(wot_scr[...], o_scr[0, 0], preferred_element_type=F32)
    y_ref[0] = x_tile + mod_ref[0, 0][2:3] * out_t.T


def _values_and_ones(v_rows):
    return jnp.concatenate([v_rows, jnp.ones((ONES_ROWS, v_rows.shape[1]), v_rows.dtype)], axis=0)


def _local_attn_kernel(*refs, kind, group, n_lat, split_input):
    q_ref = refs[0]
    k_refs = refs[1:5]
    v_refs = refs[5:9]
    extra_ref = refs[9]
    x_refs = refs[10:12] if split_input else refs[10:11]
    mod_ref, wo_ref, y_ref, o_ref, wot_scr = refs[10 + len(x_refs):]
    pl.when(_first_grid_step())(lambda: _transpose_weight(wo_ref, wot_scr))
    t = pl.program_id(1)
    is_ctx = t >= n_lat
    x_tile = jnp.where(is_ctx, x_refs[1][0], x_refs[0][0]) if split_input else x_refs[0][0]
    pen = [None, jnp.where(t >= 1, 0.0, NEG_INF).astype(F32), jnp.where(t <= n_lat - 2, 0.0, NEG_INF).astype(F32)]

    if kind == 1:
        rows = [(0, TM), (TM - SWA_RADIUS, SWA_RADIUS), (0, SWA_RADIUS), (0, TM)]
        band = []
        for (r0, nr), off in zip(rows[:3], (0, -1, 1)):
            key_i = lax.broadcasted_iota(jnp.int32, (nr, TM), 0) + (r0 + off * TM)
            qry_j = lax.broadcasted_iota(jnp.int32, (nr, TM), 1)
            band.append(jnp.where(jnp.abs(key_i - qry_j) <= SWA_RADIUS, 0.0, NEG_INF).astype(F32))
    else:
        rows = [(0, TM)] * 4
        tab = [jnp.where(t == 0, 0, jnp.where(t >= n_lat - 1, 6, 3)),
               jnp.where(t >= n_lat - 1, 5, 2),
               jnp.where(t == 0, 1, 4)]

    quad = QUAD
    width = quad * TM
    own_kv = group == 1

    def quad_queries(g):
        if not own_kv:
            return _padded_queries(q_ref, g * quad, quad, g)
        zeros = jnp.zeros((HEAD_DIM, TM), BF16)
        return jnp.concatenate(
            [jnp.concatenate([q_ref[0, 0, (g * quad + h) * HEAD_DIM:(g * quad + h + 1) * HEAD_DIM, :]
                              if j == h else zeros for j in range(quad)], axis=1)
             for h in range(quad)], axis=0)

    queries = [quad_queries(g) for g in range(N_HEADS // quad)]

    def scores(g, c):
        r0, nr = rows[c]
        lanes = slice(g * quad * HEAD_DIM, (g + 1) * quad * HEAD_DIM) if own_kv else _k_lanes(g)
        s = jnp.dot(k_refs[c][0, 0, r0:r0 + nr, lanes], queries[g], preferred_element_type=F32)
        if c < 3:
            if kind == 1:
                s = s + jnp.concatenate([band[c] if pen[c] is None else band[c] + pen[c]] * quad, axis=1)
            else:
                s = jnp.concatenate([s[:, h * TM:(h + 1) * TM] + extra_ref[g * quad + h, tab[c]]
                                     for h in range(quad)], axis=1)
                if pen[c] is not None:
                    s = s + pen[c]
        return s

    def values(g, h, c):
        r0, nr = rows[c]
        n = g * quad + h if own_kv else g
        return _values_and_ones(v_refs[c][0, 0, n * HEAD_DIM:(n + 1) * HEAD_DIM, r0:r0 + nr])

    def attend(slots):
        order = [(g, c) for g in range(N_HEADS // quad) for c in slots]
        s_next = scores(*order[0])
        for i, (g, c) in enumerate(order):
            s = s_next
            if i + 1 < len(order):
                s_next = scores(*order[i + 1])
            if c == slots[0]:
                m = jnp.full((1, width), NEG_INF, F32)
                acc = jnp.zeros((HEAD_DIM + ONES_ROWS, width), F32)
            m_new = jnp.maximum(m, jnp.max(s, axis=0, keepdims=True))
            p = jnp.exp2(s - m_new).astype(BF16)
            if own_kv:
                pv = jnp.concatenate([jnp.dot(values(g, h, c), p[:, h * TM:(h + 1) * TM],
                                              preferred_element_type=F32) for h in range(quad)], axis=1)
            else:
                pv = jnp.dot(values(g, 0, c), p, preferred_element_type=F32)
            acc = jnp.exp2(m - m_new) * acc + pv
            m = m_new
            if c == slots[-1]:
                l = acc[HEAD_DIM:HEAD_DIM + 1]
                if kind == 1:
                    sink = extra_ref[g]
                    l = l + jnp.exp2(sink - m)
                _store_heads(o_ref, acc[:HEAD_DIM] / l, g * quad, quad)
        _project_out(o_ref, x_tile, mod_ref, wot_scr, y_ref)

    pl.when(jnp.logical_not(is_ctx))(lambda: attend((3, 0, 1, 2)))
    pl.when(is_ctx)(lambda: attend((3,)))


def _local_attention(q_t, k_tok, v_t, extra, x, mod, w_o, layer, *, kind, nt_out):
    bsz, nt, d, _ = q_t.shape
    n_lat = nt - 1
    kvw = k_tok.shape[-1]
    ctx_idx = nt - 1
    split_input = isinstance(x, tuple)
    if split_input:
        x_args = list(x)
        x_specs = [pl.BlockSpec((1, TM, d), lambda b, t: (b, jnp.minimum(t, n_lat - 1), 0)),
                   pl.BlockSpec((1, TM, d), lambda b, t: (b, 0, 0))]
    else:
        x_args = [x]
        x_specs = [pl.BlockSpec((1, TM, d), lambda b, t: (b, t, 0))]
    kern = functools.partial(_local_attn_kernel, kind=kind, group=N_HEADS * HEAD_DIM // kvw, n_lat=n_lat,
                             split_input=split_input)
    chunk_maps = [lambda b, t: (b, t, 0, 0),
                  lambda b, t: (b, jnp.maximum(t - 1, 0), 0, 0),
                  lambda b, t: (b, jnp.minimum(t + 1, n_lat - 1), 0, 0),
                  lambda b, t: (b, ctx_idx, 0, 0)]
    return pl.pallas_call(
        kern,
        grid=(bsz, nt_out),
        in_specs=([pl.BlockSpec((1, 1, d, TM), lambda b, t: (b, t, 0, 0))]
                  + [pl.BlockSpec((1, 1, TM, kvw), f) for f in chunk_maps]
                  + [pl.BlockSpec((1, 1, kvw, TM), f) for f in chunk_maps]
                  + [_resident(extra.shape, lambda b, t: (0,) * extra.ndim)]
                  + x_specs
                  + [pl.BlockSpec((1, 1, 6, d), lambda b, t: (b, t // n_lat, 0, 0)),
                     _resident((1, d, d), lambda b, t: (layer, 0, 0))]),
        out_specs=pl.BlockSpec((1, TM, d), lambda b, t: (b, t, 0)),
        out_shape=jax.ShapeDtypeStruct((bsz, nt_out * TM, d), F32),
        scratch_shapes=[pltpu.VMEM((1, 1, d, TM), BF16), pltpu.VMEM((d, d), BF16)],
        compiler_params=_cparams(2),
        name="na_attention" if kind == 0 else "swa_attention",
    )(q_t, k_tok, k_tok, k_tok, k_tok, v_t, v_t, v_t, v_t, extra, *x_args, mod, w_o)


def _na_bias_kernel(rpb_ref, o_ref, col_scr, *, combos, grid_rows):
    h = pl.program_id(0)
    n_dr, n_dc = 2 * NA_WIN_H - 1, 2 * NA_WIN_W - 1
    kc = lax.broadcasted_iota(jnp.int32, (GRID_W, 2 * GRID_W), 0)
    lane = lax.broadcasted_iota(jnp.int32, (GRID_W, 2 * GRID_W), 1)
    qc = lane & (GRID_W - 1)
    dc = kc - qc + (NA_WIN_W - 1)
    cs = jnp.clip(qc - NA_WIN_W // 2, 0, GRID_W - NA_WIN_W)
    col_ok = jnp.logical_and(kc >= cs, kc < cs + NA_WIN_W)

    def fill(a, carry):
        base = (h * n_dr + a) * n_dc
        g = jnp.full(kc.shape, rpb_ref[base], F32)
        for b in range(1, n_dc):
            g = jnp.where(dc == b, rpb_ref[base + b], g)
        col_scr[a] = jnp.where(col_ok, g * LOG2E, NEG_INF)
        return carry

    lax.fori_loop(0, n_dr, fill, 0)

    rows_per_tile = TM // GRID_W
    left = lane < GRID_W
    masked = jnp.full(kc.shape, NEG_INF, F32)
    for ci, (tile, off) in enumerate(combos):
        for kr in range(rows_per_tile):
            k_row = (tile + off) * rows_per_tile + kr
            for j in range(rows_per_tile // 2):
                halves = []
                for qr in (2 * j, 2 * j + 1):
                    q_row = tile * rows_per_tile + qr
                    rs = min(max(q_row - NA_WIN_H // 2, 0), grid_rows - NA_WIN_H)
                    ok = rs <= k_row < rs + NA_WIN_H
                    halves.append(col_scr[k_row - q_row + NA_WIN_H - 1] if ok else masked)
                o_ref[0, ci, kr * GRID_W:(kr + 1) * GRID_W, j * 2 * GRID_W:(j + 1) * 2 * GRID_W] = (
                    jnp.where(left, halves[0], halves[1]))


def _na_bias_tables(rpb, n_lat):
    n_heads = rpb.shape[0]
    combos = ((0, 0), (0, 1), (1, -1), (1, 0), (1, 1), (n_lat - 1, -1), (n_lat - 1, 0))
    return pl.pallas_call(
        functools.partial(_na_bias_kernel, combos=combos, grid_rows=n_lat * TM // GRID_W),
        grid=(n_heads,),
        in_specs=[pl.BlockSpec(memory_space=pltpu.SMEM)],
        out_specs=pl.BlockSpec((1, len(combos), TM, TM), lambda h: (h, 0, 0, 0)),
        out_shape=jax.ShapeDtypeStruct((n_heads, len(combos), TM, TM), F32),
        scratch_shapes=[pltpu.VMEM((2 * NA_WIN_H - 1, GRID_W, 2 * GRID_W), F32)],
        compiler_params=_cparams(1),
        name="na_bias",
    )(rpb.reshape(-1))


def _global_attn_kernel(q_ref, k_ref, v_ref, x_ref, mod_ref, wo_ref, y_ref, o_ref, wot_scr, s_scr, *, n_lat):
    pl.when(_first_grid_step())(lambda: _transpose_weight(wo_ref, wot_scr))
    t = pl.program_id(1)
    group = N_HEADS // N_KV_HEADS
    width = group * TM
    n_buf = TILES_AHEAD + 1
    queries = [_padded_queries(q_ref, n * group, group, n) for n in range(N_KV_HEADS)]

    def attend(first_tile):
        units = [(n, c) for n in range(N_KV_HEADS) for c in range(first_tile, n_lat + 1)]

        def scores(i):
            n, c = units[i]
            s = jnp.dot(k_ref[0, c, :, _k_lanes(n)], queries[n], preferred_element_type=F32)
            s_scr[i % n_buf] = s
            return jnp.max(s, axis=0, keepdims=True)

        tile_max = {i: scores(i) for i in range(min(TILES_AHEAD, len(units)))}
        for i, (n, c) in enumerate(units):
            if i + TILES_AHEAD < len(units):
                tile_max[i + TILES_AHEAD] = scores(i + TILES_AHEAD)
            if c == first_tile:
                m = jnp.full((1, width), NEG_INF, F32)
                acc = jnp.zeros((HEAD_DIM + ONES_ROWS, width), F32)
            m_new = jnp.maximum(m, tile_max.pop(i))
            p = jnp.exp2(s_scr[i % n_buf] - m_new).astype(BF16)
            pv = jnp.dot(_values_and_ones(v_ref[0, c, n * HEAD_DIM:(n + 1) * HEAD_DIM, :]), p,
                         preferred_element_type=F32)
            acc = jnp.exp2(m - m_new) * acc + pv
            m = m_new
            if c == n_lat:
                _store_heads(o_ref, acc[:HEAD_DIM] / acc[HEAD_DIM:HEAD_DIM + 1], n * group, group)
        _project_out(o_ref, x_ref[0], mod_ref, wot_scr, y_ref)

    pl.when(t < n_lat)(lambda: attend(0))
    pl.when(t >= n_lat)(lambda: attend(n_lat))


def _global_attention(q_t, k_tok, v_t, x, mod, w_o, layer, *, nt_out):
    bsz, nt, d, _ = q_t.shape
    kvw = k_tok.shape[-1]
    n_lat = nt - 1
    assert n_lat % 2 == 0 and kvw == N_KV_HEADS * HEAD_DIM
    return pl.pallas_call(
        functools.partial(_global_attn_kernel, n_lat=n_lat),
        grid=(bsz, nt_out),
        in_specs=[pl.BlockSpec((1, 1, d, TM), lambda b, t: (b, t, 0, 0)),
                  pl.BlockSpec((1, nt, TM, kvw), lambda b, t: (b, 0, 0, 0)),
                  pl.BlockSpec((1, nt, kvw, TM), lambda b, t: (b, 0, 0, 0)),
                  pl.BlockSpec((1, TM, d), lambda b, t: (b, t, 0)),
                  pl.BlockSpec((1, 1, 6, d), lambda b, t: (b, t // n_lat, 0, 0)),
                  _resident((1, d, d), lambda b, t: (layer, 0, 0))],
        out_specs=pl.BlockSpec((1, TM, d), lambda b, t: (b, t, 0)),
        out_shape=jax.ShapeDtypeStruct((bsz, nt_out * TM, d), F32),
        scratch_shapes=[pltpu.VMEM((1, 1, d, TM), BF16), pltpu.VMEM((d, d), BF16),
                        pltpu.VMEM((TILES_AHEAD + 1, TM, N_HEADS // N_KV_HEADS * TM), F32)],
        compiler_params=_cparams(2),
        name="global_attention",
    )(q_t, k_tok, v_t, x, mod, w_o)


def _ffn_kernel(*refs, n_lat, tiles_per_batch):
    x_ref, xp_ref, xn_ref = refs[:3]
    mod_refs = refs[3:3 + FFN_TILES]
    g_ref, wup_ref, cw_ref, cb_ref, wdn_ref, y_ref, h_scr, a_scr = refs[3 + FFN_TILES:]
    rows = FFN_TILES * TM
    j = pl.program_id(0)
    g = g_ref[...]
    mods = [ref[0, 0] for ref in mod_refs]
    x = x_ref[...]
    for s in range(FFN_TILES):
        h_scr[s * TM:(s + 1) * TM] = _norm_mod(
            x[s * TM:(s + 1) * TM], g, mods[s][3:4], mods[s][4:5]).astype(BF16)
    h_scr[rows:rows + 2 * HALO] = jnp.concatenate(
        [_norm_mod(xp_ref[...], g, mods[0][3:4], mods[0][4:5]),
         _norm_mod(xn_ref[...], g, mods[-1][3:4], mods[-1][4:5])], axis=0).astype(BF16)

    tile = [(FFN_TILES * j + s) % tiles_per_batch for s in range(FFN_TILES)]
    prev_ok = [jnp.logical_and(t >= 1, t < n_lat) for t in tile]
    next_ok = [t <= n_lat - 2 for t in tile]
    row = lax.broadcasted_iota(jnp.int32, (rows, FF_CHUNK), 0)
    seam_cut_up = functools.reduce(jnp.logical_or, [
        jnp.logical_and(row == s * TM, jnp.logical_not(prev_ok[s])) for s in range(1, FFN_TILES)])
    seam_cut_dn = functools.reduce(jnp.logical_or, [
        jnp.logical_and(row == s * TM - 1, jnp.logical_not(next_ok[s - 1])) for s in range(1, FFN_TILES)])
    h = h_scr[...]
    d_ff = wdn_ref.shape[1]

    def up_conv(cols):
        u = jnp.dot(h, wup_ref[0, :, cols], preferred_element_type=F32)
        um = u[0:rows]
        u_before = jnp.where(prev_ok[0], u[rows + HALO - 1:rows + HALO], 0.0)
        u_after = jnp.where(next_ok[-1], u[rows + HALO:rows + HALO + 1], 0.0)
        up = jnp.where(row == 0, u_before, jnp.where(seam_cut_up, 0.0, pltpu.roll(um, 1, 0)))
        dn = jnp.where(row == rows - 1, u_after, jnp.where(seam_cut_dn, 0.0, pltpu.roll(um, rows - 1, 0)))
        cw = cw_ref[:, cols]
        return up * cw[0:1] + um * cw[1:2] + dn * cw[2:3] + cb_ref[:, cols]

    for c in range(d_ff // FF_CHUNK):
        a = up_conv(slice(c * FF_CHUNK, (c + 1) * FF_CHUNK))
        gate = up_conv(slice(d_ff + c * FF_CHUNK, d_ff + (c + 1) * FF_CHUNK))
        a_scr[:, c * FF_CHUNK:(c + 1) * FF_CHUNK] = (a * (gate * _sigmoid(gate))).astype(BF16)
    down = jnp.dot(a_scr[...], wdn_ref[0], preferred_element_type=F32)
    for s in range(FFN_TILES):
        y_ref[s * TM:(s + 1) * TM] = x[s * TM:(s + 1) * TM] + mods[s][5:6] * down[s * TM:(s + 1) * TM]


def _conv_ffn(x, mod, g, w_up, conv_w, conv_b, w_down, layer, *, n_lat):
    bsz, t_all, d = x.shape
    tiles_per_batch = t_all // TM
    assert bsz * tiles_per_batch % FFN_TILES == 0
    n_steps = bsz * tiles_per_batch // FFN_TILES
    d_ff = w_down.shape[1]
    rows = FFN_TILES * TM
    halos_per_step = rows // HALO
    last_halo = bsz * t_all // HALO - 1

    def mod_map(s):
        def index(j):
            tile = FFN_TILES * j + s
            return (tile // tiles_per_batch, (tile % tiles_per_batch) // n_lat, 0, 0)
        return index

    out = pl.pallas_call(
        functools.partial(_ffn_kernel, n_lat=n_lat, tiles_per_batch=tiles_per_batch),
        grid=(n_steps,),
        in_specs=([pl.BlockSpec((rows, d), lambda j: (j, 0)),
                   pl.BlockSpec((HALO, d), lambda j: (jnp.maximum(j * halos_per_step - 1, 0), 0)),
                   pl.BlockSpec((HALO, d), lambda j: (jnp.minimum((j + 1) * halos_per_step, last_halo), 0))]
                  + [pl.BlockSpec((1, 1, 6, d), mod_map(s)) for s in range(FFN_TILES)]
                  + [pl.BlockSpec((1, d), lambda j: (0, 0)),
                     _resident((1, d, 2 * d_ff), lambda j: (layer, 0, 0)),
                     pl.BlockSpec((CONV_W, 2 * d_ff), lambda j: (0, 0)),
                     pl.BlockSpec((1, 2 * d_ff), lambda j: (0, 0)),
                     _resident((1, d_ff, d), lambda j: (layer, 0, 0))]),
        out_specs=pl.BlockSpec((rows, d), lambda j: (j, 0)),
        out_shape=jax.ShapeDtypeStruct((bsz * t_all, d), F32),
        scratch_shapes=[pltpu.VMEM((rows + 2 * HALO, d), BF16), pltpu.VMEM((rows, d_ff), BF16)],
        compiler_params=_cparams(1),
        name="conv_ffn",
    )(*(x.reshape(bsz * t_all, d),) * 3, *(mod,) * FFN_TILES, g.reshape(1, d),
      w_up, conv_w, conv_b.reshape(1, 2 * d_ff), w_down)
    return out.reshape(bsz, t_all, d)


def _rope_tables(seq, n_ctx):
    t = jnp.arange(seq)
    row = (t // GRID_W).astype(F32)
    col = (t % GRID_W).astype(F32)
    n_axis = HEAD_DIM // 4
    inv = ROPE_THETA ** (-jnp.arange(n_axis, dtype=F32) / n_axis)
    ang = jnp.concatenate([row[:, None] * inv, col[:, None] * inv], axis=-1)
    cos = jnp.concatenate([jnp.cos(ang), jnp.ones((n_ctx, HEAD_DIM // 2), F32)], axis=0)
    sin = jnp.concatenate([jnp.sin(ang), jnp.zeros((n_ctx, HEAD_DIM // 2), F32)], axis=0)
    nt = (seq + n_ctx) // TM
    to_tiles = lambda a: a.reshape(nt, TM, HEAD_DIM // 2).transpose(0, 2, 1)
    return to_tiles(cos), to_tiles(sin)


def kernel(x, c, ctx, c_ctx, w_mod, b_mod, g_attn, g_ffn, na_w_qkv, na_g_q, na_g_k, na_rpb, na_w_o, swa_w_qkv, swa_g_q, swa_g_k, swa_sink, swa_w_o, ga_w_qkv, ga_g_q, ga_g_k, ga_w_o, ffn_w_up, ffn_conv_w, ffn_conv_b, ffn_w_down):
    bsz, seq, d = x.shape
    n_ctx = ctx.shape[1]
    depth = w_mod.shape[0]
    assert d == D_MODEL and seq % TM == 0 and n_ctx == TM and seq == GRID_W * GRID_W
    n_lat = seq // TM
    nt = n_lat + 1
    group = N_HEADS // N_KV_HEADS

    cvec = jnp.zeros((8, d), F32).at[:bsz].set(c).at[bsz].set(c_ctx)
    mods = _modulation(cvec, w_mod, b_mod)
    mod_lat = mods[:, :bsz].reshape(depth, bsz, 1, 6, d)
    mod_ctx = jnp.broadcast_to(mods[:, bsz].reshape(depth, 1, 1, 6, d), (depth, bsz, 1, 6, d))
    mod_all = jnp.concatenate([mod_lat, mod_ctx], axis=2)

    cos_t, sin_t = _rope_tables(seq, n_ctx)
    assert depth >= 2
    stream = (x, ctx)
    w_up_bf16, w_down_bf16 = ffn_w_up.astype(BF16), ffn_w_down.astype(BF16)

    for i in range(depth):
        last = i == depth - 1
        kind, j = i % N_MIXERS, i // N_MIXERS
        nt_out = n_lat if last else nt
        mod = mod_all[i]
        if kind == 0:
            w_in, w_out, gq, gk = na_w_qkv, na_w_o, na_g_q[j], na_g_k[j]
        elif kind == 1:
            w_in, w_out, gq, gk = swa_w_qkv, swa_w_o, swa_g_q[j], swa_g_k[j]
        else:
            w_in, w_out, gq, gk = ga_w_qkv, ga_w_o, ga_g_q[j], ga_g_k[j]
        q_t, k_tok, v_t = _qkv(stream, mod, g_attn[i], w_in, j, gq, gk, cos_t, sin_t, rope=kind != 0)
        if kind == 0:
            stream = _local_attention(q_t, k_tok, v_t, _na_bias_tables(na_rpb[j], n_lat),
                                      stream, mod, w_out, j, kind=0, nt_out=nt_out)
        elif kind == 1:
            sink = jnp.broadcast_to((swa_sink[j] * LOG2E).reshape(N_KV_HEADS, 1, group, 1),
                                    (N_KV_HEADS, 1, group, TM)).reshape(N_KV_HEADS, 1, group * TM)
            stream = _local_attention(q_t, k_tok, v_t, sink, stream, mod, w_out, j, kind=1, nt_out=nt_out)
        else:
            stream = _global_attention(q_t, k_tok, v_t, stream, mod, w_out, j, nt_out=nt_out)
        stream = _conv_ffn(stream, mod, g_ffn[i], w_up_bf16, ffn_conv_w[i], ffn_conv_b[i],
                           w_down_bf16, i, n_lat=n_lat)
    return stream
```

```python
import functools

import jax
import jax.numpy as jnp
from jax import lax
from jax.experimental import pallas as pl
from jax.experimental.pallas import tpu as pltpu

D_MODEL = 1024
GRID_W = 64
HEAD_DIM = 64
N_HEADS = D_MODEL // HEAD_DIM
N_KV_HEADS = 4
N_MIXERS = 3
NA_WIN_H = 8
NA_WIN_W = 16
SWA_RADIUS = 128
ROPE_THETA = 10000.0
CONV_W = 3
RMS_EPS = 1e-6
NEG_INF = -1e30

TM = 256
HALO = 8
FF_CHUNK = 256
FFN_TILES = 4
W_CHUNK = 256
MOD_BLOCK = 1536
HEAD_PAIR = 2 * HEAD_DIM
QUAD = 4
TILES_AHEAD = 1
ONES_ROWS = 16
LOG2E = 1.4426950408889634
VMEM_LIMIT = 56 * 1024 * 1024

BF16 = jnp.bfloat16
F32 = jnp.float32


def _cparams(n_axes):
    return pltpu.CompilerParams(dimension_semantics=("arbitrary",) * n_axes,
                                vmem_limit_bytes=VMEM_LIMIT)


def _resident(block_shape, index_map):
    return pl.BlockSpec(block_shape, index_map, pipeline_mode=pl.Buffered(1))


def _sigmoid(z):
    return 1.0 / (1.0 + jnp.exp(-z))


def _norm_mod(x, g, shift, scale):
    var = jnp.mean(x * x, axis=-1, keepdims=True)
    return (x * lax.rsqrt(var + RMS_EPS) * g) * (1.0 + scale) + shift


def _mod_kernel(c_ref, w_ref, b_ref, o_ref):
    c = c_ref[...]
    sc = (c * _sigmoid(c)).astype(BF16)
    o_ref[0] = jnp.dot(sc, w_ref[0].astype(BF16), preferred_element_type=F32) + b_ref[0]


def _modulation(cvec, w_mod, b_mod):
    depth, d, n = w_mod.shape
    bn = MOD_BLOCK
    return pl.pallas_call(
        _mod_kernel,
        grid=(depth, n // bn),
        in_specs=[pl.BlockSpec((8, d), lambda i, j: (0, 0)),
                  pl.BlockSpec((1, d, bn), lambda i, j: (i, 0, j)),
                  pl.BlockSpec((1, 1, bn), lambda i, j: (i, 0, j))],
        out_specs=pl.BlockSpec((1, 8, bn), lambda i, j: (i, 0, j)),
        out_shape=jax.ShapeDtypeStruct((depth, 8, n), F32),
        compiler_params=_cparams(2),
        name="modulation",
    )(cvec, w_mod, b_mod.reshape(depth, 1, n))


def _transpose_weight(w_ref, wt_scr):
    for c in range(wt_scr.shape[0] // W_CHUNK):
        cols = slice(c * W_CHUNK, (c + 1) * W_CHUNK)
        wt_scr[cols, :] = w_ref[0, :, cols].T.astype(BF16)


def _qkv_kernel(*refs, kvw, rope, tiles_per_batch, split_input):
    n_x = 4 if split_input else 1
    x_refs = refs[:n_x]
    (mod0_ref, mod1_ref, g_ref, w_ref, gq_ref, gk_ref, cos0_ref, sin0_ref, cos1_ref, sin1_ref,
     q_ref, k_ref, v_ref, wt_ref) = refs[n_x:]
    pl.when(pl.program_id(0) == 0)(lambda: _transpose_weight(w_ref, wt_ref))
    g = g_ref[...]
    if split_input:
        is_ctx = [(2 * pl.program_id(0) + s) % tiles_per_batch == tiles_per_batch - 1 for s in range(2)]
        tiles = [jnp.where(is_ctx[s], x_refs[2 + s][...], x_refs[s][...]) for s in range(2)]
    else:
        tiles = [x_refs[0][s * TM:(s + 1) * TM] for s in range(2)]
    mods = (mod0_ref[0, 0], mod1_ref[0, 0])
    h = jnp.concatenate(
        [_norm_mod(tiles[s], g, mods[s][0:1], mods[s][1:2]).astype(BF16) for s in range(2)], axis=0)
    def project(rows):
        return lax.dot_general(wt_ref[rows], h, (((1,), (1,)), ((), ())), preferred_element_type=F32)

    yq = project(slice(0, D_MODEL))
    yk = project(slice(D_MODEL, D_MODEL + kvw))
    yv = project(slice(D_MODEL + kvw, D_MODEL + 2 * kvw))

    def head_norm(y, gain, n_heads, cos_sin):
        y3 = y.reshape(n_heads, HEAD_DIM, TM)
        ms = jnp.mean(y3 * y3, axis=1, keepdims=True)
        y3 = y3 * lax.rsqrt(ms + RMS_EPS) * gain
        if rope:
            half = HEAD_DIM // 2
            x1, x2 = y3[:, :half], y3[:, half:]
            c, s = cos_sin
            y3 = jnp.concatenate([x1 * c - x2 * s, x2 * c + x1 * s], axis=1)
        return y3.reshape(n_heads * HEAD_DIM, TM)

    tables = ((cos0_ref[0], sin0_ref[0]), (cos1_ref[0], sin1_ref[0]))
    for s in range(2):
        q = head_norm(yq[:, s * TM:(s + 1) * TM], gq_ref[...], N_HEADS, tables[s])
        q_ref[s] = (q * (HEAD_DIM ** -0.5 * LOG2E)).astype(BF16)
    for s in range(2):
        k = head_norm(yk[:, s * TM:(s + 1) * TM], gk_ref[...], kvw // HEAD_DIM, tables[s])
        k_ref[s] = k.T.astype(BF16)
    for s in range(2):
        v_ref[s] = yv[:, s * TM:(s + 1) * TM].astype(BF16)


def _qkv(x, mod, g, w_qkv, layer, g_q, g_k, cos_t, sin_t, *, rope):
    split_input = isinstance(x, tuple)
    bsz, _, d = (x[0] if split_input else x).shape
    nt = (x[0].shape[1] + x[1].shape[1]) // TM if split_input else x.shape[1] // TM
    n_lat = nt - 1
    n_tiles = bsz * nt
    assert n_tiles % 2 == 0
    n = w_qkv.shape[2]
    kvw = (n - d) // 2
    gq = jnp.broadcast_to(g_q[:, None], (HEAD_DIM, TM))
    gk = jnp.broadcast_to(g_k[:, None], (HEAD_DIM, TM))
    half = HEAD_DIM // 2

    def mod_map(s):
        return lambda j: ((2 * j + s) // nt, ((2 * j + s) % nt) // (nt - 1), 0, 0)

    def rope_map(s):
        return lambda j: ((2 * j + s) % nt, 0, 0)

    if split_input:
        def lat_map(s):
            return lambda j: ((2 * j + s) // nt * n_lat + jnp.minimum((2 * j + s) % nt, n_lat - 1), 0)

        def ctx_map(s):
            return lambda j: ((2 * j + s) // nt, 0)

        x_args = [x[0].reshape(bsz * n_lat * TM, d)] * 2 + [x[1].reshape(bsz * TM, d)] * 2
        x_specs = [pl.BlockSpec((TM, d), f(s)) for f in (lat_map, ctx_map) for s in range(2)]
    else:
        x_args = [x.reshape(bsz * nt * TM, d)]
        x_specs = [pl.BlockSpec((2 * TM, d), lambda j: (j, 0))]

    q_t, k_tok, v_t = pl.pallas_call(
        functools.partial(_qkv_kernel, kvw=kvw, rope=rope, tiles_per_batch=nt, split_input=split_input),
        grid=(n_tiles // 2,),
        in_specs=x_specs + [
                  pl.BlockSpec((1, 1, 6, d), mod_map(0)),
                  pl.BlockSpec((1, 1, 6, d), mod_map(1)),
                  pl.BlockSpec((1, d), lambda j: (0, 0)),
                  _resident((1, d, n), lambda j: (layer, 0, 0)),
                  pl.BlockSpec((HEAD_DIM, TM), lambda j: (0, 0)),
                  pl.BlockSpec((HEAD_DIM, TM), lambda j: (0, 0)),
                  pl.BlockSpec((1, half, TM), rope_map(0)),
                  pl.BlockSpec((1, half, TM), rope_map(0)),
                  pl.BlockSpec((1, half, TM), rope_map(1)),
                  pl.BlockSpec((1, half, TM), rope_map(1))],
        out_specs=[pl.BlockSpec((2, d, TM), lambda j: (j, 0, 0)),
                   pl.BlockSpec((2, TM, kvw), lambda j: (j, 0, 0)),
                   pl.BlockSpec((2, kvw, TM), lambda j: (j, 0, 0))],
        out_shape=[jax.ShapeDtypeStruct((n_tiles, d, TM), BF16),
                   jax.ShapeDtypeStruct((n_tiles, TM, kvw), BF16),
                   jax.ShapeDtypeStruct((n_tiles, kvw, TM), BF16)],
        scratch_shapes=[pltpu.VMEM((n, d), BF16)],
        compiler_params=_cparams(1),
        name="qkv",
    )(*x_args, mod, mod, g.reshape(1, d), w_qkv, gq, gk, cos_t, sin_t, cos_t, sin_t)
    return (q_t.reshape(bsz, nt, d, TM), k_tok.reshape(bsz, nt, TM, kvw), v_t.reshape(bsz, nt, kvw, TM))


def _k_lanes(kv_head):
    g = kv_head // (HEAD_PAIR // HEAD_DIM)
    return slice(g * HEAD_PAIR, (g + 1) * HEAD_PAIR)


def _padded_queries(q_ref, first_head, n_q_heads, kv_head):
    qs = [q_ref[0, 0, (first_head + j) * HEAD_DIM:(first_head + j + 1) * HEAD_DIM, :]
          for j in range(n_q_heads)]
    qcat = qs[0] if n_q_heads == 1 else jnp.concatenate(qs, axis=1)
    slots = HEAD_PAIR // HEAD_DIM
    blocks = [jnp.zeros_like(qcat)] * slots
    blocks[kv_head % slots] = qcat
    return jnp.concatenate(blocks, axis=0)


def _store_heads(o_ref, o, first_head, n_q_heads):
    for j in range(n_q_heads):
        o_ref[0, 0, (first_head + j) * HEAD_DIM:(first_head + j + 1) * HEAD_DIM, :] = (
            o[:, j * TM:(j + 1) * TM].astype(o_ref.dtype))


def _first_grid_step():
    return functools.reduce(jnp.logical_and, [pl.program_id(a) == 0 for a in range(2)])


def _project_out(o_scr, x_tile, mod_ref, wot_scr, y_ref):
    out_t = jnp.dot(wot_scr[...], o_scr[0, 0], preferred_element_type=F32)
    y_ref[0] = x_tile + mod_ref[0, 0][2:3] * out_t.T


def _values_and_ones(v_rows):
    return jnp.concatenate([v_rows, jnp.ones((ONES_ROWS, v_rows.shape[1]), v_rows.dtype)], axis=0)


def _local_attn_kernel(*refs, kind, group, n_lat, split_input):
    q_ref = refs[0]
    k_refs = refs[1:5]
    v_refs = refs[5:9]
    extra_ref = refs[9]
    x_refs = refs[10:12] if split_input else refs[10:11]
    mod_ref, wo_ref, y_ref, o_ref, wot_scr = refs[10 + len(x_refs):]
    pl.when(_first_grid_step())(lambda: _transpose_weight(wo_ref, wot_scr))
    t = pl.program_id(1)
    is_ctx = t >= n_lat
    x_tile = jnp.where(is_ctx, x_refs[1][0], x_refs[0][0]) if split_input else x_refs[0][0]
    pen = [None, jnp.where(t >= 1, 0.0, NEG_INF).astype(F32), jnp.where(t <= n_lat - 2, 0.0, NEG_INF).astype(F32)]

    if kind == 1:
        rows = [(0, TM), (TM - SWA_RADIUS, SWA_RADIUS), (0, SWA_RADIUS), (0, TM)]
        band = []
        for (r0, nr), off in zip(rows[:3], (0, -1, 1)):
            key_i = lax.broadcasted_iota(jnp.int32, (nr, TM), 0) + (r0 + off * TM)
            qry_j = lax.broadcasted_iota(jnp.int32, (nr, TM), 1)
            band.append(jnp.where(jnp.abs(key_i - qry_j) <= SWA_RADIUS, 0.0, NEG_INF).astype(F32))
    else:
        rows = [(0, TM)] * 4
        tab = [jnp.where(t == 0, 0, jnp.where(t >= n_lat - 1, 6, 3)),
               jnp.where(t >= n_lat - 1, 5, 2),
               jnp.where(t == 0, 1, 4)]

    quad = QUAD
    width = quad * TM
    own_kv = group == 1

    def quad_queries(g):
        if not own_kv:
            return _padded_queries(q_ref, g * quad, quad, g)
        zeros = jnp.zeros((HEAD_DIM, TM), BF16)
        return jnp.concatenate(
            [jnp.concatenate([q_ref[0, 0, (g * quad + h) * HEAD_DIM:(g * quad + h + 1) * HEAD_DIM, :]
                              if j == h else zeros for j in range(quad)], axis=1)
             for h in range(quad)], axis=0)

    queries = [quad_queries(g) for g in range(N_HEADS // quad)]

    def scores(g, c):
        r0, nr = rows[c]
        lanes = slice(g * quad * HEAD_DIM, (g + 1) * quad * HEAD_DIM) if own_kv else _k_lanes(g)
        s = jnp.dot(k_refs[c][0, 0, r0:r0 + nr, lanes], queries[g], preferred_element_type=F32)
        if c < 3:
            if kind == 1:
                s = s + jnp.concatenate([band[c] if pen[c] is None else band[c] + pen[c]] * quad, axis=1)
            else:
                s = jnp.concatenate([s[:, h * TM:(h + 1) * TM] + extra_ref[g * quad + h, tab[c]]
                                     for h in range(quad)], axis=1)
                if pen[c] is not None:
                    s = s + pen[c]
        return s

    def values(g, h, c):
        r0, nr = rows[c]
        n = g * quad + h if own_kv else g
        return _values_and_ones(v_refs[c][0, 0, n * HEAD_DIM:(n + 1) * HEAD_DIM, r0:r0 + nr])

    def attend(slots):
        order = [(g, c) for g in range(N_HEADS // quad) for c in slots]
        s_next = scores(*order[0])
        for i, (g, c) in enumerate(order):
            s = s_next
            if i + 1 < len(order):
                s_next = scores(*order[i + 1])
            if c == slots[0]:
                m = jnp.full((1, width), NEG_INF, F32)
                acc = jnp.zeros((HEAD_DIM + ONES_ROWS, width), F32)
            m_new = jnp.maximum(m, jnp.max(s, axis=0, keepdims=True))
            p = jnp.exp2(s - m_new).astype(BF16)
            if own_kv:
                pv = jnp.concatenate([jnp.dot(values(g, h, c), p[:, h * TM:(h + 1) * TM],
                                              preferred_element_type=F32) for h in range(quad)], axis=1)
            else:
                pv = jnp.dot(values(g, 0, c), p, preferred_element_type=F32)
            acc = jnp.exp2(m - m_new) * acc + pv
            m = m_new
            if c == slots[-1]:
                l = acc[HEAD_DIM:HEAD_DIM + 1]
                if kind == 1:
                    sink = extra_ref[g]
                    l = l + jnp.exp2(sink - m)
                _store_heads(o_ref, acc[:HEAD_DIM] / l, g * quad, quad)
        _project_out(o_ref, x_tile, mod_ref, wot_scr, y_ref)

    pl.when(jnp.logical_not(is_ctx))(lambda: attend((3, 0, 1, 2)))
    pl.when(is_ctx)(lambda: attend((3,)))


def _local_attention(q_t, k_tok, v_t, extra, x, mod, w_o, layer, *, kind, nt_out):
    bsz, nt, d, _ = q_t.shape
    n_lat = nt - 1
    kvw = k_tok.shape[-1]
    ctx_idx = nt - 1
    split_input = isinstance(x, tuple)
    if split_input:
        x_args = list(x)
        x_specs = [pl.BlockSpec((1, TM, d), lambda b, t: (b, jnp.minimum(t, n_lat - 1), 0)),
                   pl.BlockSpec((1, TM, d), lambda b, t: (b, 0, 0))]
    else:
        x_args = [x]
        x_specs = [pl.BlockSpec((1, TM, d), lambda b, t: (b, t, 0))]
    kern = functools.partial(_local_attn_kernel, kind=kind, group=N_HEADS * HEAD_DIM // kvw, n_lat=n_lat,
                             split_input=split_input)
    chunk_maps = [lambda b, t: (b, t, 0, 0),
                  lambda b, t: (b, jnp.maximum(t - 1, 0), 0, 0),
                  lambda b, t: (b, jnp.minimum(t + 1, n_lat - 1), 0, 0),
                  lambda b, t: (b, ctx_idx, 0, 0)]
    return pl.pallas_call(
        kern,
        grid=(bsz, nt_out),
        in_specs=([pl.BlockSpec((1, 1, d, TM), lambda b, t: (b, t, 0, 0))]
                  + [pl.BlockSpec((1, 1, TM, kvw), f) for f in chunk_maps]
                  + [pl.BlockSpec((1, 1, kvw, TM), f) for f in chunk_maps]
                  + [_resident(extra.shape, lambda b, t: (0,) * extra.ndim)]
                  + x_specs
                  + [pl.BlockSpec((1, 1, 6, d), lambda b, t: (b, t // n_lat, 0, 0)),
                     _resident((1, d, d), lambda b, t: (layer, 0, 0))]),
        out_specs=pl.BlockSpec((1, TM, d), lambda b, t: (b, t, 0)),
        out_shape=jax.ShapeDtypeStruct((bsz, nt_out * TM, d), F32),
        scratch_shapes=[pltpu.VMEM((1, 1, d, TM), BF16), pltpu.VMEM((d, d), BF16)],
        compiler_params=_cparams(2),
        name="na_attention" if kind == 0 else "swa_attention",
    )(q_t, k_tok, k_tok, k_tok, k_tok, v_t, v_t, v_t, v_t, extra, *x_args, mod, w_o)


def _na_bias_kernel(rpb_ref, o_ref, col_scr, *, combos, grid_rows):
    h = pl.program_id(0)
    n_dr, n_dc = 2 * NA_WIN_H - 1, 2 * NA_WIN_W - 1
    kc = lax.broadcasted_iota(jnp.int32, (GRID_W, 2 * GRID_W), 0)
    lane = lax.broadcasted_iota(jnp.int32, (GRID_W, 2 * GRID_W), 1)
    qc = lane & (GRID_W - 1)
    dc = kc - qc + (NA_WIN_W - 1)
    cs = jnp.clip(qc - NA_WIN_W // 2, 0, GRID_W - NA_WIN_W)
    col_ok = jnp.logical_and(kc >= cs, kc < cs + NA_WIN_W)

    def fill(a, carry):
        base = (h * n_dr + a) * n_dc
        g = jnp.full(kc.shape, rpb_ref[base], F32)
        for b in range(1, n_dc):
            g = jnp.where(dc == b, rpb_ref[base + b], g)
        col_scr[a] = jnp.where(col_ok, g * LOG2E, NEG_INF)
        return carry

    lax.fori_loop(0, n_dr, fill, 0)

    rows_per_tile = TM // GRID_W
    left = lane < GRID_W
    masked = jnp.full(kc.shape, NEG_INF, F32)
    for ci, (tile, off) in enumerate(combos):
        for kr in range(rows_per_tile):
            k_row = (tile + off) * rows_per_tile + kr
            for j in range(rows_per_tile // 2):
                halves = []
                for qr in (2 * j, 2 * j + 1):
                    q_row = tile * rows_per_tile + qr
                    rs = min(max(q_row - NA_WIN_H // 2, 0), grid_rows - NA_WIN_H)
                    ok = rs <= k_row < rs + NA_WIN_H
                    halves.append(col_scr[k_row - q_row + NA_WIN_H - 1] if ok else masked)
                o_ref[0, ci, kr * GRID_W:(kr + 1) * GRID_W, j * 2 * GRID_W:(j + 1) * 2 * GRID_W] = (
                    jnp.where(left, halves[0], halves[1]))


def _na_bias_tables(rpb, n_lat):
    n_heads = rpb.shape[0]
    combos = ((0, 0), (0, 1), (1, -1), (1, 0), (1, 1), (n_lat - 1, -1), (n_lat - 1, 0))
    return pl.pallas_call(
        functools.partial(_na_bias_kernel, combos=combos, grid_rows=n_lat * TM // GRID_W),
        grid=(n_heads,),
        in_specs=[pl.BlockSpec(memory_space=pltpu.SMEM)],
        out_specs=pl.BlockSpec((1, len(combos), TM, TM), lambda h: (h, 0, 0, 0)),
        out_shape=jax.ShapeDtypeStruct((n_heads, len(combos), TM, TM), F32),
        scratch_shapes=[pltpu.VMEM((2 * NA_WIN_H - 1, GRID_W, 2 * GRID_W), F32)],
        compiler_params=_cparams(1),
        name="na_bias",
    )(rpb.reshape(-1))


def _global_attn_kernel(q_ref, k_ref, v_ref, x_ref, mod_ref, wo_ref, y_ref, o_ref, wot_scr, s_scr, *, n_lat):
    pl.when(_first_grid_step())(lambda: _transpose_weight(wo_ref, wot_scr))
    t = pl.program_id(1)
    group = N_HEADS // N_KV_HEADS
    width = group * TM
    n_buf = TILES_AHEAD + 1
    queries = [_padded_queries(q_ref, n * group, group, n) for n in range(N_KV_HEADS)]

    def attend(first_tile):
        units = [(n, c) for n in range(N_KV_HEADS) for c in range(first_tile, n_lat + 1)]

        def scores(i):
            n, c = units[i]
            s = jnp.dot(k_ref[0, c, :, _k_lanes(n)], queries[n], preferred_element_type=F32)
            s_scr[i % n_buf] = s
            return jnp.max(s, axis=0, keepdims=True)

        tile_max = {i: scores(i) for i in range(min(TILES_AHEAD, len(units)))}
        for i, (n, c) in enumerate(units):
            if i + TILES_AHEAD < len(units):
                tile_max[i + TILES_AHEAD] = scores(i + TILES_AHEAD)
            if c == first_tile:
                m = jnp.full((1, width), NEG_INF, F32)
                acc = jnp.zeros((HEAD_DIM + ONES_ROWS, width), F32)
            m_new = jnp.maximum(m, tile_max.pop(i))
            p = jnp.exp2(s_scr[i % n_buf] - m_new).astype(BF16)
            pv = jnp.dot(_values_and_ones(v_ref[0, c, n * HEAD_DIM:(n + 1) * HEAD_DIM, :]), p,
                         preferred_element_type=F32)
            acc = jnp.exp2(m - m_new) * acc + pv
            m = m_new
            if c == n_lat:
                _store_heads(o_ref, acc[:HEAD_DIM] / acc[HEAD_DIM:HEAD_DIM + 1], n * group, group)
        _project_out(o_ref, x_ref[0], mod_ref, wot_scr, y_ref)

    pl.when(t < n_lat)(lambda: attend(0))
    pl.when(t >= n_lat)(lambda: attend(n_lat))


def _global_attention(q_t, k_tok, v_t, x, mod, w_o, layer, *, nt_out):
    bsz, nt, d, _ = q_t.shape
    kvw = k_tok.shape[-1]
    n_lat = nt - 1
    assert n_lat % 2 == 0 and kvw == N_KV_HEADS * HEAD_DIM
    return pl.pallas_call(
        functools.partial(_global_attn_kernel, n_lat=n_lat),
        grid=(bsz, nt_out),
        in_specs=[pl.BlockSpec((1, 1, d, TM), lambda b, t: (b, t, 0, 0)),
                  pl.BlockSpec((1, nt, TM, kvw), lambda b, t: (b, 0, 0, 0)),
                  pl.BlockSpec((1, nt, kvw, TM), lambda b, t: (b, 0, 0, 0)),
                  pl.BlockSpec((1, TM, d), lambda b, t: (b, t, 0)),
                  pl.BlockSpec((1, 1, 6, d), lambda b, t: (b, t // n_lat, 0, 0)),
                  _resident((1, d, d), lambda b, t: (layer, 0, 0))],
        out_specs=pl.BlockSpec((1, TM, d), lambda b, t: (b, t, 0)),
        out_shape=jax.ShapeDtypeStruct((bsz, nt_out * TM, d), F32),
        scratch_shapes=[pltpu.VMEM((1, 1, d, TM), BF16), pltpu.VMEM((d, d), BF16),
                        pltpu.VMEM((TILES_AHEAD + 1, TM, N_HEADS // N_KV_HEADS * TM), F32)],
        compiler_params=_cparams(2),
        name="global_attention",
    )(q_t, k_tok, v_t, x, mod, w_o)


def _ffn_kernel(*refs, n_lat, tiles_per_batch):
    x_ref, xp_ref, xn_ref = refs[:3]
    mod_refs = refs[3:3 + FFN_TILES]
    g_ref, wup_ref, cw_ref, cb_ref, wdn_ref, y_ref, h_scr, a_scr = refs[3 + FFN_TILES:]
    rows = FFN_TILES * TM
    j = pl.program_id(0)
    g = g_ref[...]
    mods = [ref[0, 0] for ref in mod_refs]
    x = x_ref[...]
    for s in range(FFN_TILES):
        h_scr[s * TM:(s + 1) * TM] = _norm_mod(
            x[s * TM:(s + 1) * TM], g, mods[s][3:4], mods[s][4:5]).astype(BF16)
    h_scr[rows:rows + 2 * HALO] = jnp.concatenate(
        [_norm_mod(xp_ref[...], g, mods[0][3:4], mods[0][4:5]),
         _norm_mod(xn_ref[...], g, mods[-1][3:4], mods[-1][4:5])], axis=0).astype(BF16)

    tile = [(FFN_TILES * j + s) % tiles_per_batch for s in range(FFN_TILES)]
    prev_ok = [jnp.logical_and(t >= 1, t < n_lat) for t in tile]
    next_ok = [t <= n_lat - 2 for t in tile]
    row = lax.broadcasted_iota(jnp.int32, (rows, FF_CHUNK), 0)
    seam_cut_up = functools.reduce(jnp.logical_or, [
        jnp.logical_and(row == s * TM, jnp.logical_not(prev_ok[s])) for s in range(1, FFN_TILES)])
    seam_cut_dn = functools.reduce(jnp.logical_or, [
        jnp.logical_and(row == s * TM - 1, jnp.logical_not(next_ok[s - 1])) for s in range(1, FFN_TILES)])
    h = h_scr[...]
    d_ff = wdn_ref.shape[1]

    def up_conv(cols):
        u = jnp.dot(h, wup_ref[0, :, cols], preferred_element_type=F32)
        um = u[0:rows]
        u_before = jnp.where(prev_ok[0], u[rows + HALO - 1:rows + HALO], 0.0)
        u_after = jnp.where(next_ok[-1], u[rows + HALO:rows + HALO + 1], 0.0)
        up = jnp.where(row == 0, u_before, jnp.where(seam_cut_up, 0.0, pltpu.roll(um, 1, 0)))
        dn = jnp.where(row == rows - 1, u_after, jnp.where(seam_cut_dn, 0.0, pltpu.roll(um, rows - 1, 0)))
        cw = cw_ref[:, cols]
        return up * cw[0:1] + um * cw[1:2] + dn * cw[2:3] + cb_ref[:, cols]

    for c in range(d_ff // FF_CHUNK):
        a = up_conv(slice(c * FF_CHUNK, (c + 1) * FF_CHUNK))
        gate = up_conv(slice(d_ff + c * FF_CHUNK, d_ff + (c + 1) * FF_CHUNK))
        a_scr[:, c * FF_CHUNK:(c + 1) * FF_CHUNK] = (a * (gate * _sigmoid(gate))).astype(BF16)
    down = jnp.dot(a_scr[...], wdn_ref[0], preferred_element_type=F32)
    for s in range(FFN_TILES):
        y_ref[s * TM:(s + 1) * TM] = x[s * TM:(s + 1) * TM] + mods[s][5:6] * down[s * TM:(s + 1) * TM]


def _conv_ffn(x, mod, g, w_up, conv_w, conv_b, w_down, layer, *, n_lat):
    bsz, t_all, d = x.shape
    tiles_per_batch = t_all // TM
    assert bsz * tiles_per_batch % FFN_TILES == 0
    n_steps = bsz * tiles_per_batch // FFN_TILES
    d_ff = w_down.shape[1]
    rows = FFN_TILES * TM
    halos_per_step = rows // HALO
    last_halo = bsz * t_all // HALO - 1

    def mod_map(s):
        def index(j):
            tile = FFN_TILES * j + s
            return (tile // tiles_per_batch, (tile % tiles_per_batch) // n_lat, 0, 0)
        return index

    out = pl.pallas_call(
        functools.partial(_ffn_kernel, n_lat=n_lat, tiles_per_batch=tiles_per_batch),
        grid=(n_steps,),
        in_specs=([pl.BlockSpec((rows, d), lambda j: (j, 0)),
                   pl.BlockSpec((HALO, d), lambda j: (jnp.maximum(j * halos_per_step - 1, 0), 0)),
                   pl.BlockSpec((HALO, d), lambda j: (jnp.minimum((j + 1) * halos_per_step, last_halo), 0))]
                  + [pl.BlockSpec((1, 1, 6, d), mod_map(s)) for s in range(FFN_TILES)]
                  + [pl.BlockSpec((1, d), lambda j: (0, 0)),
                     _resident((1, d, 2 * d_ff), lambda j: (layer, 0, 0)),
                     pl.BlockSpec((CONV_W, 2 * d_ff), lambda j: (0, 0)),
                     pl.BlockSpec((1, 2 * d_ff), lambda j: (0, 0)),
                     _resident((1, d_ff, d), lambda j: (layer, 0, 0))]),
        out_specs=pl.BlockSpec((rows, d), lambda j: (j, 0)),
        out_shape=jax.ShapeDtypeStruct((bsz * t_all, d), F32),
        scratch_shapes=[pltpu.VMEM((rows + 2 * HALO, d), BF16), pltpu.VMEM((rows, d_ff), BF16)],
        compiler_params=_cparams(1),
        name="conv_ffn",
    )(*(x.reshape(bsz * t_all, d),) * 3, *(mod,) * FFN_TILES, g.reshape(1, d),
      w_up, conv_w, conv_b.reshape(1, 2 * d_ff), w_down)
    return out.reshape(bsz, t_all, d)


def _rope_tables(seq, n_ctx):
    t = jnp.arange(seq)
    row = (t // GRID_W).astype(F32)
    col = (t % GRID_W).astype(F32)
    n_axis = HEAD_DIM // 4
    inv = ROPE_THETA ** (-jnp.arange(n_axis, dtype=F32) / n_axis)
    ang = jnp.concatenate([row[:, None] * inv, col[:, None] * inv], axis=-1)
    cos = jnp.concatenate([jnp.cos(ang), jnp.ones((n_ctx, HEAD_DIM // 2), F32)], axis=0)
    sin = jnp.concatenate([jnp.sin(ang), jnp.zeros((n_ctx, HEAD_DIM // 2), F32)], axis=0)
    nt = (seq + n_ctx) // TM
    to_tiles = lambda a: a.reshape(nt, TM, HEAD_DIM // 2).transpose(0, 2, 1)
    return to_tiles(cos), to_tiles(sin)


def kernel(x, c, ctx, c_ctx, w_mod, b_mod, g_attn, g_ffn, na_w_qkv, na_g_q, na_g_k, na_rpb, na_w_o, swa_w_qkv, swa_g_q, swa_g_k, swa_sink, swa_w_o, ga_w_qkv, ga_g_q, ga_g_k, ga_w_o, ffn_w_up, ffn_conv_w, ffn_conv_b, ffn_w_down):
    bsz, seq, d = x.shape
    n_ctx = ctx.shape[1]
    depth = w_mod.shape[0]
    assert d == D_MODEL and seq % TM == 0 and n_ctx == TM and seq == GRID_W * GRID_W
    n_lat = seq // TM
    nt = n_lat + 1
    group = N_HEADS // N_KV_HEADS

    cvec = jnp.zeros((8, d), F32).at[:bsz].set(c).at[bsz].set(c_ctx)
    mods = _modulation(cvec, w_mod, b_mod)
    mod_lat = mods[:, :bsz].reshape(depth, bsz, 1, 6, d)
    mod_ctx = jnp.broadcast_to(mods[:, bsz].reshape(depth, 1, 1, 6, d), (depth, bsz, 1, 6, d))
    mod_all = jnp.concatenate([mod_lat, mod_ctx], axis=2)

    cos_t, sin_t = _rope_tables(seq, n_ctx)
    assert depth >= 2
    stream = (x, ctx)
    w_up_bf16, w_down_bf16 = ffn_w_up.astype(BF16), ffn_w_down.astype(BF16)

    for i in range(depth):
        last = i == depth - 1
        kind, j = i % N_MIXERS, i // N_MIXERS
        nt_out = n_lat if last else nt
        mod = mod_all[i]
        if kind == 0:
            w_in, w_out, gq, gk = na_w_qkv, na_w_o, na_g_q[j], na_g_k[j]
        elif kind == 1:
            w_in, w_out, gq, gk = swa_w_qkv, swa_w_o, swa_g_q[j], swa_g_k[j]
        else:
            w_in, w_out, gq, gk = ga_w_qkv, ga_w_o, ga_g_q[j], ga_g_k[j]
        q_t, k_tok, v_t = _qkv(stream, mod, g_attn[i], w_in, j, gq, gk, cos_t, sin_t, rope=kind != 0)
        if kind == 0:
            stream = _local_attention(q_t, k_tok, v_t, _na_bias_tables(na_rpb[j], n_lat),
                                      stream, mod, w_out, j, kind=0, nt_out=nt_out)
        elif kind == 1:
            sink = jnp.broadcast_to((swa_sink[j] * LOG2E).reshape(N_KV_HEADS, 1, group, 1),
                                    (N_KV_HEADS, 1, group, TM)).reshape(N_KV_HEADS, 1, group * TM)
            stream = _local_attention(q_t, k_tok, v_t, sink, stream, mod, w_out, j, kind=1, nt_out=nt_out)
        else:
            stream = _global_attention(q_t, k_tok, v_t, stream, mod, w_out, j, nt_out=nt_out)
        stream = _conv_ffn(stream, mod, g_ffn[i], w_up_bf16, ffn_conv_w[i], ffn_conv_b[i],
                           w_down_bf16, i, n_lat=n_lat)
    return stream
```
